```python
import math
import jax, jax.numpy as jnp
from jax import lax
import numpy as np

D_MODEL = 1024
BATCH = 4
SEQ = 4096
DEPTH = 1
DEC_BATCH = 32
DEC_SEQ = 8
PAST_LEN = 16384
PAGE_SIZE = 128

A_HEADS = 8
A_KV_GROUPS = 2
A_HPG = A_HEADS // A_KV_GROUPS
A_HEAD_DIM = 64
A_WIDTH = A_HEADS * A_HEAD_DIM
A_KV_WIDTH = A_KV_GROUPS * A_HEAD_DIM
CMP_BLOCK = 32
CMP_STRIDE = 16
SLC_BLOCK = 64
N_SELECT = 16
WINDOW = 512
Q_BLOCK = 128
ROPE_THETA = 10000.0
FORCE_SCORE = 1e4
NEG_INF = -1e30
B_HEADS = 4
B_KEY_DIM = 128
B_VAL_DIM = 128
B_KWIDTH = B_HEADS * B_KEY_DIM
B_WIDTH = B_HEADS * B_VAL_DIM
HGRN_CHUNK = 64
D_FF = 2816
FFN_CONV = 3
EPS = 1e-6
IN_WIDTHS = (A_WIDTH, 2 * A_KV_WIDTH, 2 * A_KV_WIDTH, 2 * A_KV_WIDTH, 3 * A_HEADS,
             B_KWIDTH, B_KWIDTH, B_WIDTH, B_WIDTH, 2 * D_MODEL)
D_IN = sum(IN_WIDTHS)

kernel_name = 'nsa_hgrn2_gated_merge_convffn_step'


def rms_norm(x, g):
    xf = x.astype(jnp.float32)
    y = xf * lax.rsqrt(jnp.mean(xf * xf, axis=-1, keepdims=True) + EPS)
    return (y * g.astype(jnp.float32)).astype(x.dtype)


def rope(x, pos):
    half = x.shape[-1] // 2
    inv = ROPE_THETA ** (-jnp.arange(half, dtype=jnp.float32) / half)
    ang = pos.astype(jnp.float32)[:, None] * inv[None, :]
    cos = jnp.cos(ang)[None, :, None, :]
    sin = jnp.sin(ang)[None, :, None, :]
    xf = x.astype(jnp.float32)
    x1, x2 = xf[..., :half], xf[..., half:]
    return jnp.concatenate([x1 * cos - x2 * sin, x2 * cos + x1 * sin], axis=-1).astype(x.dtype)


def masked_softmax(s, mask):
    s = jnp.where(mask, s.astype(jnp.float32), NEG_INF)
    p = jnp.exp(s - jnp.max(s, axis=-1, keepdims=True)) * mask
    return p / jnp.maximum(jnp.sum(p, axis=-1, keepdims=True), 1e-30)


def pad_time(rows):
    T = rows.shape[1]
    tp = -(-T // SLC_BLOCK) * SLC_BLOCK
    return jnp.pad(rows, ((0, 0), (0, tp - T), (0, 0), (0, 0), (0, 0)))


def mixer_features(h, pos, lb, w_in, q_norm_g, k_norm_g):
    B, T, _ = h.shape
    z = h @ w_in
    offs = np.cumsum(IN_WIDTHS)[:-1].tolist()
    q_a, kv_c, kv_s, kv_w, gate_a, q_b, f_b, i_b, g_b, mg = jnp.split(z, offs, axis=-1)
    q = rope(rms_norm(q_a.reshape(B, T, A_HEADS, A_HEAD_DIM), q_norm_g), pos)

    def kv_rows(kv, i):
        kv = kv.reshape(B, T, 2, A_KV_GROUPS, A_HEAD_DIM)
        k = rope(rms_norm(kv[:, :, 0], k_norm_g[i]), pos)
        return jnp.stack([k, kv[:, :, 1]], axis=2)

    gates = jax.nn.sigmoid(gate_a).reshape(B, T, A_HEADS, 3)
    fz = f_b.astype(jnp.float32)
    lb = lb.astype(jnp.float32)
    logf = jnp.log(lb + (1.0 - lb) * jax.nn.sigmoid(fz)).reshape(B, T, B_HEADS, B_KEY_DIM)
    kb = ((1.0 - lb) * jax.nn.sigmoid(-fz)).reshape(B, T, B_HEADS, B_KEY_DIM)
    qb = jax.nn.silu(q_b.astype(jnp.float32)).reshape(B, T, B_HEADS, B_KEY_DIM)
    vb = i_b.astype(jnp.float32).reshape(B, T, B_HEADS, B_VAL_DIM)
    gb = jax.nn.silu(g_b)
    ma, mb = jnp.split(jax.nn.sigmoid(mg), 2, axis=-1)
    return (q, kv_rows(kv_c, 0), kv_rows(kv_s, 1), kv_rows(kv_w, 2), gates, qb, kb, vb, logf, gb, ma, mb)


def compress(rows, pos_emb, w1, w2):
    B, T = rows.shape[:2]
    R = CMP_BLOCK // CMP_STRIDE
    n_chunk = T // CMP_STRIDE
    n_cmp = n_chunk - R + 1
    ch = rows.reshape(B, n_chunk, CMP_STRIDE, 2, A_KV_GROUPS, A_HEAD_DIM)
    ch = ch.transpose(0, 1, 3, 4, 2, 5).reshape(B, n_chunk, 2, A_KV_GROUPS, CMP_STRIDE * A_HEAD_DIM)
    w1r = w1.reshape(2, R, CMP_STRIDE * A_HEAD_DIM, A_HEAD_DIM)
    pre = jnp.einsum('jf,jfe->je', pos_emb.reshape(2, CMP_BLOCK * A_HEAD_DIM), w1)[None, None, :, None, :]
    for r in range(R):
        pre = pre + jnp.einsum('bcjgf,jfe->bcjge', ch, w1r[:, r])[:, r:r + n_cmp]
    out = jnp.einsum('bcjge,jed->bcjgd', jax.nn.silu(pre), w2)
    c_end = jnp.arange(n_cmp, dtype=jnp.int32) * CMP_STRIDE + (CMP_BLOCK - 1)
    return out[:, :, 0], out[:, :, 1], c_end


def select_blocks(rows):
    B, T = rows.shape[:2]
    blk = rows.reshape(B, T // SLC_BLOCK, SLC_BLOCK, 2, A_KV_GROUPS, A_HEAD_DIM)
    return blk.transpose(0, 4, 1, 2, 3, 5)


def nsa_core(q, q_pos, gates, kc, vc, c_end, kv_blk, kv_win, w_pos):
    B, Tq = q.shape[:2]
    NC = kc.shape[1]
    NS = kv_blk.shape[2]
    qg = q.reshape(B, Tq, A_KV_GROUPS, A_HPG, A_HEAD_DIM) * (A_HEAD_DIM ** -0.5)
    m_c = (c_end[None, :] <= q_pos[:, None])[None, :, None, None, :]
    p_c = masked_softmax(jnp.einsum('btghd,bcgd->btghc', qg, kc), m_c)
    o_c = jnp.einsum('btghc,bcgd->btghd', p_c.astype(vc.dtype), vc)
    c_start = jnp.arange(NC, dtype=jnp.int32) * CMP_STRIDE
    s_start = jnp.arange(NS, dtype=jnp.int32) * SLC_BLOCK
    cover = ((c_start[:, None] < s_start[None, :] + SLC_BLOCK)
             & (c_start[:, None] + CMP_BLOCK > s_start[None, :])).astype(jnp.float32)
    imp = jnp.einsum('btghc,cs->btgs', p_c, cover)
    blk = jnp.arange(NS, dtype=jnp.int32)[None, :]
    cur = (q_pos // SLC_BLOCK)[:, None]
    forced = (blk == 0) | (blk == cur) | (blk == cur - 1)
    valid = s_start[None, :] <= q_pos[:, None]
    score = jnp.where(forced[None, :, None, :], FORCE_SCORE,
                      jnp.where(valid[None, :, None, :], imp, -1.0))
    _, idx = lax.top_k(score, min(N_SELECT, NS))
    n = idx.shape[-1]
    bi = jnp.arange(B)[:, None, None, None]
    gi = jnp.arange(A_KV_GROUPS)[None, None, :, None]
    sel = kv_blk[bi, gi, idx]
    sel = sel.reshape(B, Tq, A_KV_GROUPS, n * SLC_BLOCK, 2, A_HEAD_DIM)
    sel_pos = (idx[..., None] * SLC_BLOCK + jnp.arange(SLC_BLOCK, dtype=jnp.int32)).reshape(B, Tq, A_KV_GROUPS, n * SLC_BLOCK)
    m_s = (sel_pos <= q_pos[None, :, None, None])[:, :, :, None, :]
    p_s = masked_softmax(jnp.einsum('btghd,btgkd->btghk', qg, sel[..., 0, :]), m_s)
    o_s = jnp.einsum('btghk,btgkd->btghd', p_s.astype(sel.dtype), sel[..., 1, :])
    dpos = q_pos[:, None] - w_pos[None, :]
    m_w = ((dpos >= 0) & (dpos < WINDOW) & (w_pos[None, :] >= 0))[None, :, None, None, :]
    p_w = masked_softmax(jnp.einsum('btghd,bkgd->btghk', qg, kv_win[:, :, 0]), m_w)
    o_w = jnp.einsum('btghk,bkgd->btghd', p_w.astype(kv_win.dtype), kv_win[:, :, 1])
    g = gates.reshape(B, Tq, A_KV_GROUPS, A_HPG, 3)
    o = g[..., 0:1] * o_c + g[..., 1:2] * o_s + g[..., 2:3] * o_w
    return o.reshape(B, Tq, A_WIDTH)


def hgrn2_recurrence(q, k, v, logf, s0):
    B, T, H, DK = q.shape
    C = math.gcd(T, HGRN_CHUNK)
    n = T // C

    def to_chunks(a):
        return jnp.swapaxes(a.reshape(B, n, C, *a.shape[2:]), 0, 1)

    tri = jnp.tril(jnp.ones((C, C), dtype=bool))[None, :, :, None, None]

    def step(S, inp):
        qc, kc, vc, lf = inp
        b = jnp.cumsum(lf, axis=1)
        o_inter = jnp.einsum('bthk,bhkv->bthv', qc * jnp.exp(b), S)
        d = b[:, :, None] - b[:, None, :]
        decay = jnp.where(tri, jnp.exp(jnp.where(tri, d, 0.0)), 0.0)
        A = jnp.einsum('bthk,btshk,bshk->btsh', qc, decay, kc)
        o_intra = jnp.einsum('btsh,bshv->bthv', A, vc)
        b_last = b[:, -1]
        S_new = jnp.exp(b_last)[..., None] * S + jnp.einsum('bshk,bshv->bhkv', kc * jnp.exp(b_last[:, None] - b), vc)
        return S_new, o_inter + o_intra

    S, o = lax.scan(step, s0, (to_chunks(q), to_chunks(k), to_chunks(v), to_chunks(logf)))
    return jnp.swapaxes(o, 0, 1).reshape(B, T, H, v.shape[-1]), S


def hgrn_readout(o, gb, g_norm):
    B, T = o.shape[:2]
    return rms_norm(o, g_norm).reshape(B, T, B_WIDTH).astype(gb.dtype) * gb


def merge_and_ffn(x, o_a, o_b, ma, mb, w_branch, w_out, ffn_norm_g, ffn_w_in, ffn_conv_w, ffn_conv_b, ffn_w_out, conv_buf):
    m = ma * (o_a.astype(x.dtype) @ w_branch[:A_WIDTH]) + mb * (o_b @ w_branch[A_WIDTH:])
    x = x + m @ w_out
    h = rms_norm(x, ffn_norm_g)
    a, b = jnp.split(h @ ffn_w_in, 2, axis=-1)
    T = a.shape[1]
    a_ext = jnp.concatenate([conv_buf.astype(a.dtype), a], axis=1)
    a_conv = ffn_conv_b + sum(a_ext[:, j:j + T] * ffn_conv_w[j] for j in range(FFN_CONV))
    y = x + (jax.nn.silu(a_conv) * b) @ ffn_w_out
    return y, a_ext[:, -(FFN_CONV - 1):]


def trunk_layer(xp, xs, cmp_pool, slc_pool, page_table, win_buf, hgrn_state, conv_buf, lb,
                attn_norm_g, w_in, q_norm_g, k_norm_g, cmp_pos_emb, cmp_w1, cmp_w2, hgrn_norm_g,
                w_branch, w_out, ffn_norm_g, ffn_w_in, ffn_conv_w, ffn_conv_b, ffn_w_out):
    B, T = xp.shape[:2]
    DB, TS = xs.shape[:2]
    pos_p = jnp.arange(T, dtype=jnp.int32)
    pos_s = PAST_LEN + jnp.arange(TS, dtype=jnp.int32)
    (q_p, kvc_p, kvs_p, kvw_p, gt_p, qb_p, kb_p, vb_p, lf_p, gb_p, ma_p, mb_p) = mixer_features(
        rms_norm(xp, attn_norm_g), pos_p, lb, w_in, q_norm_g, k_norm_g)
    (q_s, kvc_s, kvs_s, kvw_s, gt_s, qb_s, kb_s, vb_s, lf_s, gb_s, ma_s, mb_s) = mixer_features(
        rms_norm(xs, attn_norm_g), pos_s, lb, w_in, q_norm_g, k_norm_g)

    kc_p, vc_p, cend_p = compress(pad_time(kvc_p), cmp_pos_emb, cmp_w1, cmp_w2)
    blk_p = select_blocks(pad_time(kvs_p))
    kvw_pad = jnp.pad(kvw_p, ((0, 0), (WINDOW, 0), (0, 0), (0, 0), (0, 0)))

    def prompt_block(i):
        s0 = i * Q_BLOCK
        return nsa_core(
            lax.dynamic_slice_in_dim(q_p, s0, Q_BLOCK, axis=1),
            s0 + jnp.arange(Q_BLOCK, dtype=jnp.int32),
            lax.dynamic_slice_in_dim(gt_p, s0, Q_BLOCK, axis=1),
            kc_p, vc_p, cend_p, blk_p,
            lax.dynamic_slice_in_dim(kvw_pad, s0, WINDOW + Q_BLOCK, axis=1),
            s0 - WINDOW + jnp.arange(WINDOW + Q_BLOCK, dtype=jnp.int32))

    oa_p = lax.map(prompt_block, jnp.arange(T // Q_BLOCK, dtype=jnp.int32))
    oa_p = jnp.swapaxes(oa_p, 0, 1).reshape(B, T, A_WIDTH)

    past_c = cmp_pool[page_table].reshape(DB, -1, 2, A_KV_GROUPS, A_HEAD_DIM)
    past_s = slc_pool[page_table].reshape(DB, -1, 2, A_KV_GROUPS, A_HEAD_DIM)
    kc_s, vc_s, cend_s = compress(pad_time(jnp.concatenate([past_c, kvc_s.astype(past_c.dtype)], axis=1)),
                                  cmp_pos_emb, cmp_w1, cmp_w2)
    blk_s = select_blocks(pad_time(jnp.concatenate([past_s, kvs_s.astype(past_s.dtype)], axis=1)))
    wb = win_buf.shape[1]
    win_s = jnp.concatenate([win_buf, kvw_s.astype(win_buf.dtype)], axis=1)
    oa_s = nsa_core(q_s, pos_s, gt_s, kc_s, vc_s, cend_s, blk_s, win_s,
                    PAST_LEN - wb + jnp.arange(wb + TS, dtype=jnp.int32))

    o_p, S_p = hgrn2_recurrence(qb_p, kb_p, vb_p, lf_p, jnp.zeros((B, B_HEADS, B_KEY_DIM, B_VAL_DIM), jnp.float32))
    o_s, S_s = hgrn2_recurrence(qb_s, kb_s, vb_s, lf_s, hgrn_state.astype(jnp.float32))
    ob_p = hgrn_readout(o_p, gb_p, hgrn_norm_g)
    ob_s = hgrn_readout(o_s, gb_s, hgrn_norm_g)

    yp, conv_p = merge_and_ffn(xp, oa_p, ob_p, ma_p, mb_p, w_branch, w_out, ffn_norm_g, ffn_w_in,
                               ffn_conv_w, ffn_conv_b, ffn_w_out, jnp.zeros((B, FFN_CONV - 1, D_FF), xp.dtype))
    ys, conv_s = merge_and_ffn(xs, oa_s, ob_s, ma_s, mb_s, w_branch, w_out, ffn_norm_g, ffn_w_in,
                               ffn_conv_w, ffn_conv_b, ffn_w_out, conv_buf)
    return (yp, ys, kvc_p, kvc_s, kvs_p, kvs_s, kvw_p[:, -min(WINDOW, T):], win_s[:, -wb:],
            S_p, S_s, conv_p, conv_s)


def setup_inputs(seed: int = 0) -> dict:
    key = jax.random.key(seed)
    ks = jax.random.split(key, 24)
    f32 = jnp.float32
    n_pages = PAST_LEN // PAGE_SIZE
    n_pool = (DEC_BATCH * n_pages * 5) // 4
    win_buf = min(WINDOW, PAST_LEN)

    def nrm(k, shape, s):
        return s * jax.random.normal(k, shape, f32)

    page_table = jax.random.permutation(ks[4], n_pool)[:DEC_BATCH * n_pages].reshape(DEC_BATCH, n_pages).astype(jnp.int32)
    return {
        'x_prompt': nrm(ks[0], (BATCH, SEQ, D_MODEL), 1.0),
        'x_sample': nrm(ks[1], (DEC_BATCH, DEC_SEQ, D_MODEL), 1.0),
        'cache_cmp_kv': nrm(ks[2], (DEPTH, n_pool, PAGE_SIZE, 2, A_KV_GROUPS, A_HEAD_DIM), 1.0),
        'cache_slc_kv': nrm(ks[3], (DEPTH, n_pool, PAGE_SIZE, 2, A_KV_GROUPS, A_HEAD_DIM), 1.0),
        'page_table': page_table,
        'state_win_kv': nrm(ks[5], (DEPTH, DEC_BATCH, win_buf, 2, A_KV_GROUPS, A_HEAD_DIM), 1.0),
        'state_hgrn': nrm(ks[6], (DEPTH, DEC_BATCH, B_HEADS, B_KEY_DIM, B_VAL_DIM), 0.5),
        'state_ffn_conv': nrm(ks[7], (DEPTH, DEC_BATCH, FFN_CONV - 1, D_FF), 1.0),
        'attn_norm_g': 1.0 + nrm(ks[8], (DEPTH, D_MODEL), 0.02),
        'w_in': nrm(ks[9], (DEPTH, D_MODEL, D_IN), D_MODEL ** -0.5),
        'q_norm_g': 1.0 + nrm(ks[10], (DEPTH, A_HEAD_DIM), 0.02),
        'k_norm_g': 1.0 + nrm(ks[11], (DEPTH, 3, A_HEAD_DIM), 0.02),
        'cmp_pos_emb': nrm(ks[12], (DEPTH, 2, CMP_BLOCK, A_HEAD_DIM), 0.1),
        'cmp_w1': nrm(ks[13], (DEPTH, 2, CMP_BLOCK * A_HEAD_DIM, A_HEAD_DIM), (CMP_BLOCK * A_HEAD_DIM) ** -0.5),
        'cmp_w2': nrm(ks[14], (DEPTH, 2, A_HEAD_DIM, A_HEAD_DIM), A_HEAD_DIM ** -0.5),
        'hgrn_lb_logits': nrm(ks[15], (DEPTH + 1, B_KWIDTH), 0.5),
        'hgrn_norm_g': 1.0 + nrm(ks[16], (DEPTH, B_VAL_DIM), 0.02),
        'w_branch': nrm(ks[17], (DEPTH, A_WIDTH + B_WIDTH, D_MODEL), A_WIDTH ** -0.5),
        'w_out': nrm(ks[18], (DEPTH, D_MODEL, D_MODEL), D_MODEL ** -0.5),
        'ffn_norm_g': 1.0 + nrm(ks[19], (DEPTH, D_MODEL), 0.02),
        'ffn_w_in': nrm(ks[20], (DEPTH, D_MODEL, 2 * D_FF), D_MODEL ** -0.5),
        'ffn_conv_w': nrm(ks[21], (DEPTH, FFN_CONV, D_FF), FFN_CONV ** -0.5),
        'ffn_conv_b': nrm(ks[22], (DEPTH, D_FF), 0.01),
        'ffn_w_out': nrm(ks[23], (DEPTH, D_FF, D_MODEL), D_FF ** -0.5),
    }


def reference(x_prompt, x_sample, cache_cmp_kv, cache_slc_kv, page_table, state_win_kv, state_hgrn,
              state_ffn_conv, attn_norm_g, w_in, q_norm_g, k_norm_g, cmp_pos_emb, cmp_w1, cmp_w2,
              hgrn_lb_logits, hgrn_norm_g, w_branch, w_out, ffn_norm_g, ffn_w_in, ffn_conv_w,
              ffn_conv_b, ffn_w_out):
    lbs = jnp.cumsum(jax.nn.softmax(hgrn_lb_logits.astype(jnp.float32), axis=0), axis=0)
    yp, ys = x_prompt, x_sample
    cols = [[] for _ in range(10)]
    for l in range(DEPTH):
        yp, ys, *st = trunk_layer(
            yp, ys, cache_cmp_kv[l], cache_slc_kv[l], page_table, state_win_kv[l], state_hgrn[l],
            state_ffn_conv[l], lbs[l], attn_norm_g[l], w_in[l], q_norm_g[l], k_norm_g[l],
            cmp_pos_emb[l], cmp_w1[l], cmp_w2[l], hgrn_norm_g[l], w_branch[l], w_out[l],
            ffn_norm_g[l], ffn_w_in[l], ffn_conv_w[l], ffn_conv_b[l], ffn_w_out[l])
        for c, s in zip(cols, st):
            c.append(s)
    (cmp_kv_prompt, cmp_kv_sample, slc_kv_prompt, slc_kv_sample, win_kv_prompt, win_kv_sample,
     hgrn_prompt, hgrn_sample, ffn_conv_prompt, ffn_conv_sample) = [jnp.stack(c) for c in cols]
    return (yp, ys, cmp_kv_prompt, cmp_kv_sample, slc_kv_prompt, slc_kv_sample, win_kv_prompt,
            win_kv_sample, hgrn_prompt, hgrn_sample, ffn_conv_prompt, ffn_conv_sample)
```

```python
import functools

import numpy as np
import jax
import jax.numpy as jnp
from jax import lax
from jax.experimental import pallas as pl
from jax.experimental.pallas import tpu as pltpu

F32 = jnp.float32
BF16 = jnp.bfloat16

D_MODEL = 1024
PAGE_SIZE = 128
A_HEADS = 8
A_KV_GROUPS = 2
A_HPG = A_HEADS // A_KV_GROUPS
A_HEAD_DIM = 64
CMP_BLOCK = 32
CMP_STRIDE = 16
SLC_BLOCK = 64
N_SELECT = 16
WINDOW = 512
Q_BLOCK = 128
ROPE_THETA = 10000.0
FORCE_SCORE = 1e4
NEG_INF = -1e30
B_HEADS = 4
B_KEY_DIM = 128
D_FF = 2816
FFN_CONV = 3
EPS = 1e-6

LANES = 128
SUBLANES = 8
KV_ROW = 2 * A_KV_GROUPS * A_HEAD_DIM
HGRN_CHUNK = 128
HGRN_LEVELS = 7
VMEM_LIMIT = 56 * 1024 * 1024

OFF_Q, OFF_KVC, OFF_KVS, OFF_KVW = 0, 512, 768, 1024
OFF_QB, OFF_FB, OFF_IB, OFF_GB, OFF_MG, OFF_GATE = 1280, 1792, 2304, 2816, 3328, 5376
W_PACK = 5504


def _dot(a, b):
    return jnp.dot(a, b, preferred_element_type=F32)


def _dot_nt(a, b):
    return lax.dot_general(a, b, (((1,), (1,)), ((), ())), preferred_element_type=F32)


def _dot_tn(a, b):
    return lax.dot_general(a, b, (((0,), (0,)), ((), ())), preferred_element_type=F32)


def _split_bf16(x):
    hi = x.astype(BF16)
    lo = (x - hi.astype(F32)).astype(BF16)
    return hi, lo


def _round_up(n, m):
    return -(-n // m) * m


def _const_spec(shape):
    nd = len(shape)
    return pl.BlockSpec(shape, lambda *_: (0,) * nd, pipeline_mode=pl.Buffered(1))


def _params(semantics):
    return pltpu.CompilerParams(dimension_semantics=semantics, vmem_limit_bytes=VMEM_LIMIT)


def _inproj_kernel(x_ref, g_ref, w_ref, cos_ref, sin_ref, qg_ref, kg_ref, lbl_ref, mseg_ref,
                   q_ref, kvc_ref, kvs_ref, kvw_ref, kvsb_ref, kvwb_ref, gate_ref,
                   qb_ref, kb_ref, vb_ref, lf_ref, gb_ref, mg_ref):
    x = x_ref[...]
    ms = jnp.mean(x * x, axis=-1, keepdims=True)
    h = (x * lax.rsqrt(ms + EPS) * g_ref[...]).astype(BF16)
    cos = cos_ref[...]
    sin = sin_ref[...]
    tm = x.shape[0]
    lane = lax.broadcasted_iota(jnp.int32, (tm, LANES), 1)
    first_half = (lane & (A_HEAD_DIM // 2)) == 0
    low64 = lane < A_HEAD_DIM
    mseg = mseg_ref[...]

    def proj(lo, width):
        return _dot(h, w_ref[:, lo:lo + width])

    def head_norm_rope(chunk, gain):
        s_hi, s_lo = _split_bf16(chunk * chunk)
        mean = _dot(jnp.concatenate([s_hi, s_lo], axis=-1), mseg)
        y = chunk * lax.rsqrt(mean + EPS) * gain
        rot = jnp.where(first_half, pltpu.roll(y, LANES - A_HEAD_DIM // 2, 1),
                        pltpu.roll(y, A_HEAD_DIM // 2, 1))
        return y * cos + rot * sin

    zq = proj(OFF_Q, A_HEADS * A_HEAD_DIM)
    qg = qg_ref[...]
    for k in range(A_HEADS // 2):
        c = head_norm_rope(zq[:, k * LANES:(k + 1) * LANES], qg) * (A_HEAD_DIM ** -0.5)
        r = pltpu.roll(c, A_HEAD_DIM, 1)
        if k < A_HPG // 2:
            h0, h1 = jnp.where(low64, c, 0.0), jnp.where(low64, r, 0.0)
        else:
            h0, h1 = jnp.where(low64, 0.0, r), jnp.where(low64, 0.0, c)
        q_ref[:, (2 * k) * LANES:(2 * k + 1) * LANES] = h0.astype(BF16)
        q_ref[:, (2 * k + 1) * LANES:(2 * k + 2) * LANES] = h1.astype(BF16)

    for i, (off, f_ref, b_ref) in enumerate(((OFF_KVC, kvc_ref, None), (OFF_KVS, kvs_ref, kvsb_ref),
                                             (OFF_KVW, kvw_ref, kvwb_ref))):
        z = proj(off, KV_ROW)
        k_rot = head_norm_rope(z[:, :LANES], kg_ref[i:i + 1, :])
        v = z[:, LANES:]
        f_ref[:, :LANES] = k_rot
        f_ref[:, LANES:] = v
        if b_ref is not None:
            b_ref[:, :LANES] = k_rot.astype(BF16)
            b_ref[:, LANES:] = v.astype(BF16)

    gate_ref[...] = jax.nn.sigmoid(proj(OFF_GATE, LANES))

    lbl = lbl_ref[...]
    e = jnp.exp(lbl - jnp.max(lbl, axis=0, keepdims=True))
    lb = e[0:1, :] / jnp.sum(e, axis=0, keepdims=True)
    fz = proj(OFF_FB, B_HEADS * B_KEY_DIM)
    lf_ref[...] = jnp.log(lb + (1.0 - lb) * jax.nn.sigmoid(fz))
    kb_ref[...] = (1.0 - lb) * jax.nn.sigmoid(-fz)
    zqb = proj(OFF_QB, B_HEADS * B_KEY_DIM)
    qb_ref[...] = zqb * jax.nn.sigmoid(zqb)
    vb_ref[...] = proj(OFF_IB, B_HEADS * B_KEY_DIM)
    zg = proj(OFF_GB, B_HEADS * B_KEY_DIM)
    gb_ref[...] = zg * jax.nn.sigmoid(zg)
    mg_ref[...] = jax.nn.sigmoid(proj(OFF_MG, 2 * D_MODEL))


def _inproj(x2d, cos_tab, sin_tab, consts, tm):
    n = x2d.shape[0]
    tab_blocks = cos_tab.shape[0] // tm
    row = lambda w: pl.BlockSpec((tm, w), lambda i: (i, 0))
    tab = pl.BlockSpec((tm, LANES), lambda i: (i % tab_blocks, 0))
    widths = (A_HEADS * LANES, KV_ROW, KV_ROW, KV_ROW, KV_ROW, KV_ROW, LANES,
              512, 512, 512, 512, 512, 2 * D_MODEL)
    dtypes = (BF16, F32, F32, F32, BF16, BF16, F32, F32, F32, F32, F32, F32, F32)
    names = ('q', 'kvc', 'kvs', 'kvw', 'kvs_bf', 'kvw_bf', 'gate', 'qb', 'kb', 'vb', 'lf', 'gb', 'mg')
    outs = pl.pallas_call(
        _inproj_kernel,
        grid=(n // tm,),
        in_specs=[row(D_MODEL), _const_spec((1, D_MODEL)), _const_spec((D_MODEL, W_PACK)), tab, tab,
                  _const_spec((1, LANES)), _const_spec((3, LANES)),
                  _const_spec(consts['lbl'].shape), _const_spec((2 * LANES, LANES))],
        out_specs=[row(w) for w in widths],
        out_shape=[jax.ShapeDtypeStruct((n, w), d) for w, d in zip(widths, dtypes)],
        compiler_params=_params(("parallel",)),
        name="inproj",
    )(x2d, consts['attn_g'], consts['w_pack'], cos_tab, sin_tab, consts['q_g'], consts['k_g'],
      consts['lbl'], consts['mseg'])
    return dict(zip(names, outs))


def _token_halves(buf, start, size, stride=1):
    k = buf[pl.ds(2 * start, size, stride=2 * stride), :]
    v = buf[pl.ds(2 * start + 1, size, stride=2 * stride), :]
    return k, v


def _compress_rows(buf, pe_ref, w1_ref, w2_ref, kc_ref, vc_ref, m_rows, n_cmp):
    acc = jnp.zeros((m_rows, KV_ROW), F32)
    for p in range(CMP_BLOCK):
        xp = jnp.concatenate(_token_halves(buf, p, m_rows, CMP_STRIDE), axis=-1) + pe_ref[p:p + 1, :]
        acc = acc + _dot(xp.astype(BF16), w1_ref[p])
    hid = acc * jax.nn.sigmoid(acc)
    out = _dot(hid.astype(BF16), w2_ref[...])
    row = lax.broadcasted_iota(jnp.int32, out.shape, 0)
    out = jnp.where(row < n_cmp, out, 0.0)
    ncp = kc_ref.shape[1]
    kc_ref[0, 0:m_rows, :] = out[:, :LANES].astype(BF16)
    vc_ref[0, 0:m_rows, :] = out[:, LANES:].astype(BF16)
    if ncp > m_rows:
        kc_ref[0, m_rows:ncp, :] = jnp.zeros((ncp - m_rows, LANES), BF16)
        vc_ref[0, m_rows:ncp, :] = jnp.zeros((ncp - m_rows, LANES), BF16)


def _compress_dense_kernel(rows_ref, pe_ref, w1_ref, w2_ref, kc_ref, vc_ref, buf, *, m_rows, n_cmp):
    t2 = rows_ref.shape[1]
    buf[0:t2, :] = rows_ref[0]
    buf[t2:, :] = jnp.zeros((buf.shape[0] - t2, LANES), F32)
    _compress_rows(buf, pe_ref, w1_ref, w2_ref, kc_ref, vc_ref, m_rows, n_cmp)


def _page_copy(pool_ref, page, buf, slot, sem):
    return pltpu.make_async_copy(pool_ref.at[page], buf.at[pl.ds(slot * 2 * PAGE_SIZE, 2 * PAGE_SIZE)], sem)


def _append_new_rows(buf, new_ref, past):
    n = new_ref.shape[1]
    buf[2 * past:2 * past + n, :] = new_ref[0]
    buf[2 * past + n:, :] = jnp.zeros((buf.shape[0] - 2 * past - n, LANES), F32)


def _gather_pages_start(pt_ref, b, pool_ref, buf, sem, n_pages):
    def body(p, c):
        _page_copy(pool_ref, pt_ref[b, p], buf, p, sem).start()
        return c
    lax.fori_loop(0, n_pages, body, 0)


def _gather_pages_wait(pool_ref, buf, sem, n_pages):
    def body(p, c):
        _page_copy(pool_ref, 0, buf, p, sem).wait()
        return c
    lax.fori_loop(0, n_pages, body, 0)


def _compress_paged_kernel(pt_ref, pool_ref, new_ref, pe_ref, w1_ref, w2_ref, kc_ref, vc_ref, buf, sem,
                           *, n_pages, m_rows, n_cmp):
    b = pl.program_id(0)
    _gather_pages_start(pt_ref, b, pool_ref, buf, sem, n_pages)
    _append_new_rows(buf, new_ref, n_pages * PAGE_SIZE)
    _gather_pages_wait(pool_ref, buf, sem, n_pages)
    _compress_rows(buf, pe_ref, w1_ref, w2_ref, kc_ref, vc_ref, m_rows, n_cmp)


def _compress_geometry(t_real):
    t_pad = _round_up(t_real, SLC_BLOCK)
    n_cmp = t_pad // CMP_STRIDE - CMP_BLOCK // CMP_STRIDE + 1
    m_rows = _round_up(n_cmp, SUBLANES)
    ncp = _round_up(n_cmp, LANES)
    buf_rows = _round_up(CMP_STRIDE * (m_rows - 1) + CMP_BLOCK, SUBLANES)
    return n_cmp, m_rows, ncp, max(buf_rows, _round_up(t_real, SUBLANES))


def _compress_dense(rows, consts):
    b, t2, _ = rows.shape
    n_cmp, m_rows, ncp, buf_rows = _compress_geometry(t2 // 2)
    return pl.pallas_call(
        functools.partial(_compress_dense_kernel, m_rows=m_rows, n_cmp=n_cmp),
        grid=(b,),
        in_specs=[pl.BlockSpec((1, t2, LANES), lambda i: (i, 0, 0)),
                  _const_spec((CMP_BLOCK, KV_ROW)), _const_spec((CMP_BLOCK, KV_ROW, KV_ROW)),
                  _const_spec((KV_ROW, KV_ROW))],
        out_specs=[pl.BlockSpec((1, ncp, LANES), lambda i: (i, 0, 0))] * 2,
        out_shape=[jax.ShapeDtypeStruct((b, ncp, LANES), BF16)] * 2,
        scratch_shapes=[pltpu.VMEM((2 * buf_rows, LANES), F32)],
        compiler_params=_params(("parallel",)),
        name="compress_dense",
    )(rows, consts['cmp_pe'], consts['cmp_w1'], consts['cmp_w2'])


def _compress_paged(page_table, pool, new_rows, consts):
    db, n_pages = page_table.shape
    ts2 = new_rows.shape[1]
    n_cmp, m_rows, ncp, buf_rows = _compress_geometry(n_pages * PAGE_SIZE + ts2 // 2)
    grid_spec = pltpu.PrefetchScalarGridSpec(
        num_scalar_prefetch=1,
        grid=(db,),
        in_specs=[pl.BlockSpec(memory_space=pl.ANY),
                  pl.BlockSpec((1, ts2, LANES), lambda i, pt: (i, 0, 0)),
                  pl.BlockSpec((CMP_BLOCK, KV_ROW), lambda i, pt: (0, 0)),
                  pl.BlockSpec((CMP_BLOCK, KV_ROW, KV_ROW), lambda i, pt: (0, 0, 0)),
                  pl.BlockSpec((KV_ROW, KV_ROW), lambda i, pt: (0, 0))],
        out_specs=[pl.BlockSpec((1, ncp, LANES), lambda i, pt: (i, 0, 0))] * 2,
        scratch_shapes=[pltpu.VMEM((2 * buf_rows, LANES), F32), pltpu.SemaphoreType.DMA(())],
    )
    return pl.pallas_call(
        functools.partial(_compress_paged_kernel, n_pages=n_pages, m_rows=m_rows, n_cmp=n_cmp),
        grid_spec=grid_spec,
        out_shape=[jax.ShapeDtypeStruct((db, ncp, LANES), BF16)] * 2,
        compiler_params=_params(("arbitrary",)),
        name="compress_paged",
    )(page_table, pool, new_rows, consts['cmp_pe'], consts['cmp_w1'], consts['cmp_w2'])


def _softmax_rows(s, mask):
    s = jnp.where(mask, s, NEG_INF)
    p = jnp.where(mask, jnp.exp(s - jnp.max(s, axis=-1, keepdims=True)), 0.0)
    return p / jnp.maximum(jnp.sum(p, axis=-1, keepdims=True), 1e-30)


def _online_step(carry, s, v):
    m, l, acc = carry
    m_new = jnp.maximum(m, jnp.max(s, axis=-1, keepdims=True))
    alpha = jnp.exp(m - m_new)
    p = jnp.exp(s - m_new)
    l = alpha * l + jnp.sum(p, axis=-1, keepdims=True)
    acc = alpha * acc + _dot(p.astype(BF16), v)
    return m_new, l, acc


def _online_init(rows):
    return (jnp.full((rows, 1), NEG_INF, F32), jnp.zeros((rows, 1), F32), jnp.zeros((rows, LANES), F32))


def _online_finish(carry):
    _, l, acc = carry
    return acc / jnp.maximum(l, 1e-30)


def _block_scores(p_sum, cover_ref, qpos, n_blocks):
    hi, lo = _split_bf16(p_sum)
    imp = _dot(jnp.concatenate([hi, lo], axis=-1), cover_ref[...])
    blk = lax.broadcasted_iota(jnp.int32, imp.shape, 1)
    cur = qpos // SLC_BLOCK
    forced = (blk == 0) | (blk == cur) | (blk == cur - 1)
    score = jnp.where(forced, FORCE_SCORE, jnp.where(blk <= cur, imp, -1.0))
    return jnp.where(blk < n_blocks, score, -2.0)


def _topk_select(score, n_blocks, n_sel):
    lane = lax.broadcasted_iota(jnp.int32, score.shape, 1)
    cnt = jnp.zeros(score.shape, F32)
    for s in range(n_blocks):
        col = score[:, s:s + 1]
        beats = (col > score) | ((col == score) & (lane > s))
        cnt = cnt + jnp.where(beats, 1.0, 0.0)
    return jnp.where((cnt < n_sel) & (lane < n_blocks), 1.0, 0.0)


def _assemble_heads(heads, low64):
    chunks = []
    for k in range(A_HEADS // 2):
        a, b = heads[2 * k], heads[2 * k + 1]
        if k < A_HPG // 2:
            chunks.append(jnp.where(low64, a, pltpu.roll(b, A_HEAD_DIM, 1)))
        else:
            chunks.append(jnp.where(low64, pltpu.roll(a, A_HEAD_DIM, 1), b))
    return chunks


def _nsa_prompt_kernel(q_ref, gate_ref, kc_ref, vc_ref, ks_ref, vs_ref, kw_ref, vw_ref, cover_ref, exp_ref,
                       o_ref, *, n_blocks, n_sel, tk):
    i = pl.program_id(1)
    s0 = i * Q_BLOCK
    q = q_ref[0]
    gates = gate_ref[0]
    rows = A_HEADS * Q_BLOCK
    q_all = jnp.concatenate([q[:, h * LANES:(h + 1) * LANES] for h in range(A_HEADS)], axis=0)
    qpos1 = s0 + lax.broadcasted_iota(jnp.int32, (Q_BLOCK, 1), 0)
    qpos_all = jnp.concatenate([qpos1] * A_HEADS, axis=0)

    ncp = kc_ref.shape[1]
    c_end = lax.broadcasted_iota(jnp.int32, (1, ncp), 1) * CMP_STRIDE + (CMP_BLOCK - 1)
    p_c = _softmax_rows(_dot_nt(q_all, kc_ref[0]), c_end <= qpos_all)
    o_c = _dot(p_c.astype(BF16), vc_ref[0])

    bias_rows = []
    for g in range(A_KV_GROUPS):
        base = g * A_HPG * Q_BLOCK
        p_sum = p_c[base:base + Q_BLOCK]
        for hh in range(1, A_HPG):
            p_sum = p_sum + p_c[base + hh * Q_BLOCK:base + (hh + 1) * Q_BLOCK]
        score = _block_scores(p_sum, cover_ref, qpos1, n_blocks)
        bias_rows.append(_topk_select(score, n_blocks, n_sel).astype(BF16))

    def sel_body(kt, carry):
        off = pl.multiple_of(kt * tk, tk)
        s = _dot_nt(q_all, ks_ref[0, pl.ds(off, tk), :])
        kpos = off + lax.broadcasted_iota(jnp.int32, (1, tk), 1)
        biases = []
        for g in range(A_KV_GROUPS):
            em = _dot(bias_rows[g], exp_ref[kt])
            bias = jnp.where((em > 0.5) & (kpos <= qpos1), 0.0, NEG_INF)
            biases += [bias] * A_HPG
        return _online_step(carry, s + jnp.concatenate(biases, axis=0), vs_ref[0, pl.ds(off, tk), :])

    n_kt = (s0 + Q_BLOCK + tk - 1) // tk
    o_s = _online_finish(lax.fori_loop(0, n_kt, sel_body, _online_init(rows)))

    def win_body(kt, carry):
        off = pl.multiple_of(kt * Q_BLOCK, Q_BLOCK)
        s = _dot_nt(q_all, kw_ref[0, pl.ds(off, Q_BLOCK), :])
        d = qpos_all - (off + lax.broadcasted_iota(jnp.int32, (1, Q_BLOCK), 1))
        bias = jnp.where((d >= 0) & (d < WINDOW), 0.0, NEG_INF)
        return _online_step(carry, s + bias, vw_ref[0, pl.ds(off, Q_BLOCK), :])

    o_w = _online_finish(lax.fori_loop(jnp.maximum(i - WINDOW // Q_BLOCK, 0), i + 1, win_body,
                                       _online_init(rows)))

    heads = []
    for h in range(A_HEADS):
        r = slice(h * Q_BLOCK, (h + 1) * Q_BLOCK)
        heads.append(gates[:, 3 * h:3 * h + 1] * o_c[r] + gates[:, 3 * h + 1:3 * h + 2] * o_s[r]
                     + gates[:, 3 * h + 2:3 * h + 3] * o_w[r])
    low64 = lax.broadcasted_iota(jnp.int32, (Q_BLOCK, LANES), 1) < A_HEAD_DIM
    for k, chunk in enumerate(_assemble_heads(heads, low64)):
        o_ref[0, :, k * LANES:(k + 1) * LANES] = chunk.astype(BF16)


def _cover_matrix(n_cmp, ncp, n_blocks, nsp):
    c = np.arange(ncp)[:, None]
    s = np.arange(nsp)[None, :]
    cover = ((c * CMP_STRIDE < s * SLC_BLOCK + SLC_BLOCK) & (c * CMP_STRIDE + CMP_BLOCK > s * SLC_BLOCK)
             & (c < n_cmp) & (s < n_blocks))
    return jnp.asarray(np.concatenate([cover, cover], axis=0), BF16)


def _expand_matrix(n_keys, block0=0):
    e = (np.arange(n_keys)[None, :] // SLC_BLOCK) == (block0 + np.arange(LANES)[:, None])
    return e


def _expand_tiles(n_keys, tk):
    e = _expand_matrix(n_keys).reshape(LANES, n_keys // tk, tk)
    return jnp.asarray(np.transpose(e, (1, 0, 2)), BF16)


def _nsa_prompt(q, gates, kc, vc, kvs_bf, kvw_bf, t):
    b = q.shape[0]
    ncp = kc.shape[1]
    n_cmp = t // CMP_STRIDE - CMP_BLOCK // CMP_STRIDE + 1
    n_blocks = t // SLC_BLOCK
    assert n_blocks <= LANES and t % Q_BLOCK == 0
    tk = min(512, t)
    cover = _cover_matrix(n_cmp, ncp, n_blocks, LANES)
    expand = _expand_tiles(t, tk)
    per_b = lambda rows, w, j: pl.BlockSpec((1, rows, w), lambda bi, i: (bi, 0, j))
    return pl.pallas_call(
        functools.partial(_nsa_prompt_kernel, n_blocks=n_blocks, n_sel=min(N_SELECT, n_blocks), tk=tk),
        grid=(b, t // Q_BLOCK),
        in_specs=[pl.BlockSpec((1, Q_BLOCK, A_HEADS * LANES), lambda bi, i: (bi, i, 0)),
                  pl.BlockSpec((1, Q_BLOCK, LANES), lambda bi, i: (bi, i, 0)),
                  per_b(ncp, LANES, 0), per_b(ncp, LANES, 0),
                  per_b(t, LANES, 0), per_b(t, LANES, 1), per_b(t, LANES, 0), per_b(t, LANES, 1),
                  _const_spec(cover.shape), _const_spec(expand.shape)],
        out_specs=pl.BlockSpec((1, Q_BLOCK, A_HEADS * A_HEAD_DIM), lambda bi, i: (bi, i, 0)),
        out_shape=jax.ShapeDtypeStruct((b, t, A_HEADS * A_HEAD_DIM), BF16),
        compiler_params=_params(("parallel", "parallel")),
        name="nsa_prompt",
    )(q, gates, kc, vc, kvs_bf, kvs_bf, kvw_bf, kvw_bf, cover, expand)


def _nsa_sample_kernel(pt_ref, q_ref, gate_ref, kc_ref, vc_ref, pool_ref, new_ref, win_ref, cover_ref, exp_ref,
                       o_ref, buf, sem, *, n_pages, n_blocks, n_sel, win_len, key_chunk):
    b = pl.program_id(0)
    _gather_pages_start(pt_ref, b, pool_ref, buf, sem, n_pages)
    past = n_pages * PAGE_SIZE
    ts = q_ref.shape[1]
    _append_new_rows(buf, new_ref, past)

    q = q_ref[0].astype(F32)
    q_all = jnp.concatenate([q[:, h * LANES:(h + 1) * LANES] for h in range(A_HEADS)], axis=0).astype(BF16)
    rows = A_HEADS * ts
    qpos1 = past + lax.broadcasted_iota(jnp.int32, (ts, 1), 0)
    qpos_all = jnp.concatenate([qpos1] * A_HEADS, axis=0)

    ncp = kc_ref.shape[1]
    c_end = lax.broadcasted_iota(jnp.int32, (1, ncp), 1) * CMP_STRIDE + (CMP_BLOCK - 1)
    p_c = _softmax_rows(_dot_nt(q_all, kc_ref[0]), c_end <= qpos_all)
    o_c = _dot(p_c.astype(BF16), vc_ref[0])

    p_groups = []
    for g in range(A_KV_GROUPS):
        base = g * A_HPG * ts
        p_sum = p_c[base:base + ts]
        for hh in range(1, A_HPG):
            p_sum = p_sum + p_c[base + hh * ts:base + (hh + 1) * ts]
        p_groups.append(p_sum)
    qpos_g = jnp.concatenate([qpos1] * A_KV_GROUPS, axis=0)
    score = _block_scores(jnp.concatenate(p_groups, axis=0), cover_ref, qpos_g, n_blocks)
    sel = _topk_select(score, n_blocks, n_sel).astype(BF16)

    _gather_pages_wait(pool_ref, buf, sem, n_pages)

    n_keys = buf.shape[0] // 2
    carry = _online_init(rows)
    for ck in range(-(-n_keys // key_chunk)):
        k0 = ck * key_chunk
        kn = min(key_chunk, n_keys - k0)
        k_rows, v_rows = _token_halves(buf, k0, kn)
        s = _dot_nt(q_all, k_rows.astype(BF16))
        blk0 = ck * (key_chunk // SLC_BLOCK)
        em = _dot(sel[:, blk0:blk0 + LANES], exp_ref[:, 0:kn])
        kpos = k0 + lax.broadcasted_iota(jnp.int32, (1, kn), 1)
        bias = jnp.where((em > 0.5) & (kpos <= qpos_g), 0.0, NEG_INF)
        biases = []
        for g in range(A_KV_GROUPS):
            biases += [bias[g * ts:(g + 1) * ts]] * A_HPG
        carry = _online_step(carry, s + jnp.concatenate(biases, axis=0), v_rows.astype(BF16))
    o_s = _online_finish(carry)

    win = win_ref[0]
    wlen = win.shape[0]
    w_pos = past - (win_len - ts) + lax.broadcasted_iota(jnp.int32, (1, wlen), 1)
    d = qpos_all - w_pos
    m_w = (d >= 0) & (d < WINDOW) & (w_pos >= 0)
    p_w = _softmax_rows(_dot_nt(q_all, win[:, :LANES].astype(BF16)), m_w)
    o_w = _dot(p_w.astype(BF16), win[:, LANES:].astype(BF16))

    gates = gate_ref[0]
    heads = []
    for h in range(A_HEADS):
        r = slice(h * ts, (h + 1) * ts)
        heads.append(gates[:, 3 * h:3 * h + 1] * o_c[r] + gates[:, 3 * h + 1:3 * h + 2] * o_s[r]
                     + gates[:, 3 * h + 2:3 * h + 3] * o_w[r])
    low64 = lax.broadcasted_iota(jnp.int32, (ts, LANES), 1) < A_HEAD_DIM
    for k, chunk in enumerate(_assemble_heads(heads, low64)):
        o_ref[0, :, k * LANES:(k + 1) * LANES] = chunk.astype(BF16)


def _nsa_sample(page_table, q, gates, kc, vc, pool, new_rows, win_rows, win_len):
    db, n_pages = page_table.shape
    ts = q.shape[1]
    past = n_pages * PAGE_SIZE
    ncp = kc.shape[1]
    t_pad = _round_up(past + ts, SLC_BLOCK)
    n_cmp = t_pad // CMP_STRIDE - CMP_BLOCK // CMP_STRIDE + 1
    n_blocks = t_pad // SLC_BLOCK
    key_chunk = LANES * SLC_BLOCK
    n_keys = past + LANES
    nsp = LANES * (-(-n_keys // key_chunk))
    assert nsp >= n_blocks and ts % SUBLANES == 0
    cover = _cover_matrix(n_cmp, ncp, n_blocks, nsp)
    expand = jnp.asarray(_expand_matrix(min(key_chunk, n_keys)), BF16)
    wlen = win_rows.shape[1]
    cm = lambda nd: (lambda i, pt: (0,) * nd)
    per_b = lambda r, w: pl.BlockSpec((1, r, w), lambda i, pt: (i, 0, 0))
    grid_spec = pltpu.PrefetchScalarGridSpec(
        num_scalar_prefetch=1,
        grid=(db,),
        in_specs=[per_b(ts, A_HEADS * LANES), per_b(ts, LANES), per_b(ncp, LANES), per_b(ncp, LANES),
                  pl.BlockSpec(memory_space=pl.ANY), per_b(2 * ts, LANES), per_b(wlen, KV_ROW),
                  pl.BlockSpec(cover.shape, cm(2)), pl.BlockSpec(expand.shape, cm(2))],
        out_specs=per_b(ts, A_HEADS * A_HEAD_DIM),
        scratch_shapes=[pltpu.VMEM((2 * n_keys, LANES), F32), pltpu.SemaphoreType.DMA(())],
    )
    return pl.pallas_call(
        functools.partial(_nsa_sample_kernel, n_pages=n_pages, n_blocks=n_blocks,
                          n_sel=min(N_SELECT, n_blocks), win_len=win_len, key_chunk=key_chunk),
        grid_spec=grid_spec,
        out_shape=jax.ShapeDtypeStruct((db, ts, A_HEADS * A_HEAD_DIM), BF16),
        compiler_params=_params(("arbitrary",)),
        name="nsa_sample",
    )(page_table, q, gates, kc, vc, pool, new_rows, win_rows, cover, expand)


def _hgrn_matrices():
    c = HGRN_CHUNK
    t = np.arange(c)[:, None]
    u = np.arange(c)[None, :]
    mats = [u <= t]
    masks = [t == u]
    for lvl in range(HGRN_LEVELS):
        m = 1 << lvl
        mid = (t // (2 * m)) * (2 * m) + m - 1
        mats.append((u > mid) & (u <= t))
        mats.append((u > t) & (u <= mid))
        masks.append((t // (2 * m) == u // (2 * m)) & (t % (2 * m) >= m) & (u % (2 * m) < m))
    mats.append(u > t)
    pm = np.concatenate(mats, axis=0)
    return (jnp.asarray(np.concatenate([pm, pm], axis=1), BF16),
            jnp.asarray(np.stack(masks).astype(np.float32)))


def _hgrn_kernel(qb_ref, kb_ref, vb_ref, lf_ref, gb_ref, s0_ref, gn_ref, pm_ref, lm_ref,
                 ob_ref, sout_ref, st_scr, pad_scr):
    j = pl.program_id(1)
    c = HGRN_CHUNK
    t_blk = qb_ref.shape[1]

    @pl.when(j == 0)
    def _():
        for h in range(B_HEADS):
            st_scr[h] = s0_ref[0, h].T

    def load(ref, slot):
        if t_blk == c:
            return ref[0]
        pad_scr[slot] = jnp.zeros((c, B_HEADS * B_KEY_DIM), F32)
        pad_scr[slot, 0:t_blk, :] = ref[0]
        return pad_scr[slot]

    qb, kb, vb, lf = load(qb_ref, 0), load(kb_ref, 1), load(vb_ref, 2), load(lf_ref, 3)
    lf_hi, lf_lo = _split_bf16(lf)
    ex = _dot(pm_ref[...], jnp.concatenate([lf_hi, lf_lo], axis=0))
    gn = gn_ref[...]
    for h in range(B_HEADS):
        sl = slice(h * B_KEY_DIM, (h + 1) * B_KEY_DIM)
        q, k = qb[:, sl], kb[:, sl]
        v = vb[:, sl].astype(BF16)
        b_cum = ex[0:c, sl]
        a = lm_ref[0] * _dot_nt(q.astype(BF16), k.astype(BF16))
        for lvl in range(HGRN_LEVELS):
            eq = ex[(2 * lvl + 1) * c:(2 * lvl + 2) * c, sl]
            ek = ex[(2 * lvl + 2) * c:(2 * lvl + 3) * c, sl]
            a = a + lm_ref[lvl + 1] * _dot_nt((q * jnp.exp(eq)).astype(BF16), (k * jnp.exp(ek)).astype(BF16))
        st = st_scr[h]
        o = _dot_nt((q * jnp.exp(b_cum)).astype(BF16), st.astype(BF16)) + _dot(a.astype(BF16), v)
        e_end = ex[(2 * HGRN_LEVELS + 1) * c:(2 * HGRN_LEVELS + 2) * c, sl]
        st_scr[h] = st * jnp.exp(b_cum[c - 1:c, :]) + _dot_tn(v, (k * jnp.exp(e_end)).astype(BF16))
        y = o * lax.rsqrt(jnp.mean(o * o, axis=-1, keepdims=True) + EPS) * gn
        ob_ref[0, :, sl] = y[0:t_blk] * gb_ref[0, :, sl]

    @pl.when(j == pl.num_programs(1) - 1)
    def _():
        for h in range(B_HEADS):
            sout_ref[0, h] = st_scr[h].T


def _hgrn(qb, kb, vb, lf, gb, s0, consts):
    b, t, w = qb.shape
    t_blk = min(t, HGRN_CHUNK)
    assert t % t_blk == 0 and (t_blk == HGRN_CHUNK or t == t_blk)
    tok = pl.BlockSpec((1, t_blk, w), lambda bi, j: (bi, j, 0))
    st = pl.BlockSpec((1, B_HEADS, B_KEY_DIM, B_KEY_DIM), lambda bi, j: (bi, 0, 0, 0))
    return pl.pallas_call(
        _hgrn_kernel,
        grid=(b, t // t_blk),
        in_specs=[tok, tok, tok, tok, tok, st, _const_spec((1, B_KEY_DIM)),
                  _const_spec(consts['hgrn_pm'].shape), _const_spec(consts['hgrn_lm'].shape)],
        out_specs=[tok, st],
        out_shape=[jax.ShapeDtypeStruct((b, t, w), F32),
                   jax.ShapeDtypeStruct((b, B_HEADS, B_KEY_DIM, B_KEY_DIM), F32)],
        scratch_shapes=[pltpu.VMEM((B_HEADS, B_KEY_DIM, B_KEY_DIM), F32),
                        pltpu.VMEM((4, HGRN_CHUNK, w), F32)],
        compiler_params=_params(("parallel", "arbitrary")),
        name="hgrn",
    )(qb, kb, vb, lf, gb, s0, consts['hgrn_g'], consts['hgrn_pm'], consts['hgrn_lm'])


def _merge_ffn_kernel(x_ref, oa_ref, ob_ref, mg_ref, p1_ref, p2_ref, wa_ref, wb_ref, wo_ref, fg_ref, win_ref,
                      cw_ref, cb_ref, wout_ref, y_ref, a_ref, carry_scr, *, seq_len):
    tm = x_ref.shape[0]
    mg = mg_ref[...]
    m = (mg[:, :D_MODEL] * _dot(oa_ref[...], wa_ref[...])
         + mg[:, D_MODEL:] * _dot(ob_ref[...].astype(BF16), wb_ref[...]))
    x2 = x_ref[...] + _dot(m.astype(BF16), wo_ref[...])
    h = (x2 * lax.rsqrt(jnp.mean(x2 * x2, axis=-1, keepdims=True) + EPS) * fg_ref[...]).astype(BF16)
    a = _dot(h, win_ref[:, :D_FF])
    gate = _dot(h, win_ref[:, D_FF:])
    row = lax.broadcasted_iota(jnp.int32, (tm, 1), 0)
    if seq_len >= tm:
        j = pl.program_id(0) % (seq_len // tm)

        @pl.when(j == 0)
        def _():
            carry_scr[0:2, :] = p1_ref[0]

        prev = carry_scr[...]
        a1 = jnp.where(row == 0, prev[1:2], pltpu.roll(a, 1, 0))
        a2 = jnp.where(row == 0, prev[0:1], jnp.where(row == 1, prev[1:2], pltpu.roll(a, 2, 0)))
        carry_scr[0:2, :] = a[tm - 2:tm]
        a_ref[0] = a[tm - 2:tm]
    else:
        t = row % seq_len
        a1 = jnp.where(t == 0, p1_ref[...], pltpu.roll(a, 1, 0))
        a2 = jnp.where(t < 2, p2_ref[...], pltpu.roll(a, 2, 0))
        a_ref[...] = a
    a_conv = cb_ref[...] + a2 * cw_ref[0:1, :] + a1 * cw_ref[1:2, :] + a * cw_ref[2:3, :]
    act = a_conv * jax.nn.sigmoid(a_conv) * gate
    y_ref[...] = x2 + _dot(act.astype(BF16), wout_ref[...])


def _merge_ffn(x2d, oa, ob, mg, conv_state, seq_len, consts, tm):
    n = x2d.shape[0]
    b = n // seq_len
    row = lambda w: pl.BlockSpec((tm, w), lambda i: (i, 0))
    if seq_len >= tm:
        assert seq_len % tm == 0
        per_seq = seq_len // tm
        p1, p2 = conv_state, conv_state
        p_spec = pl.BlockSpec((1, FFN_CONV - 1, D_FF), lambda i: (i // per_seq, 0, 0))
        a_shape = jax.ShapeDtypeStruct((b, FFN_CONV - 1, D_FF), F32)
        a_spec = pl.BlockSpec((1, FFN_CONV - 1, D_FF), lambda i: (i // per_seq, 0, 0))
    else:
        assert tm % seq_len == 0 and seq_len >= FFN_CONV - 1
        zeros = jnp.zeros((b, seq_len - 1, D_FF), F32)
        p1 = jnp.concatenate([conv_state[:, 1:2], zeros], axis=1).reshape(n, D_FF)
        p2 = jnp.concatenate([conv_state, zeros[:, 1:]], axis=1).reshape(n, D_FF)
        p_spec = row(D_FF)
        a_shape = jax.ShapeDtypeStruct((n, D_FF), F32)
        a_spec = row(D_FF)
    y, a_out = pl.pallas_call(
        functools.partial(_merge_ffn_kernel, seq_len=seq_len),
        grid=(n // tm,),
        in_specs=[row(D_MODEL), row(A_HEADS * A_HEAD_DIM), row(B_HEADS * B_KEY_DIM), row(2 * D_MODEL),
                  p_spec, p_spec,
                  _const_spec((A_HEADS * A_HEAD_DIM, D_MODEL)), _const_spec((B_HEADS * B_KEY_DIM, D_MODEL)),
                  _const_spec((D_MODEL, D_MODEL)), _const_spec((1, D_MODEL)),
                  _const_spec((D_MODEL, 2 * D_FF)), _const_spec((FFN_CONV, D_FF)), _const_spec((1, D_FF)),
                  _const_spec((D_FF, D_MODEL))],
        out_specs=[row(D_MODEL), a_spec],
        out_shape=[jax.ShapeDtypeStruct((n, D_MODEL), F32), a_shape],
        scratch_shapes=[pltpu.VMEM((SUBLANES, D_FF), F32)],
        compiler_params=_params(("arbitrary",)),
        name="merge_ffn",
    )(x2d, oa, ob, mg, p1, p2, consts['w_a'], consts['w_b'], consts['w_out'], consts['ffn_g'],
      consts['ffn_w_in'], consts['conv_w'], consts['conv_b'], consts['ffn_w_out'])
    if seq_len >= tm:
        return y, a_out
    return y, a_out.reshape(b, seq_len, D_FF)[:, seq_len - (FFN_CONV - 1):]


def _prepare_consts(attn_norm_g, w_in, q_norm_g, k_norm_g, cmp_pos_emb, cmp_w1, cmp_w2, hgrn_lb_logits,
                    hgrn_norm_g, w_branch, w_out, ffn_norm_g, ffn_w_in, ffn_conv_w, ffn_conv_b, ffn_w_out):
    n_q = A_HEADS * A_HEAD_DIM
    gate_lo = n_q + 3 * KV_ROW
    gate_hi = gate_lo + 3 * A_HEADS
    w_pack = jnp.concatenate([w_in[:, :gate_lo], w_in[:, gate_hi:], w_in[:, gate_lo:gate_hi],
                              jnp.zeros((D_MODEL, LANES - 3 * A_HEADS), w_in.dtype)], axis=1).astype(BF16)
    seg = np.arange(LANES) // A_HEAD_DIM
    mseg = (seg[:, None] == seg[None, :]).astype(np.float32) / A_HEAD_DIM
    eye = jnp.eye(2 * A_KV_GROUPS, dtype=F32)
    jsel = np.repeat(np.arange(2), A_KV_GROUPS)
    w1 = cmp_w1.reshape(2, CMP_BLOCK, A_HEAD_DIM, A_HEAD_DIM)[jsel]
    w1_bd = jnp.einsum('ab,apde->padbe', eye, w1).reshape(CMP_BLOCK, KV_ROW, KV_ROW).astype(BF16)
    w2_bd = jnp.einsum('ab,ade->adbe', eye, cmp_w2[jsel]).reshape(KV_ROW, KV_ROW).astype(BF16)
    pe = jnp.transpose(cmp_pos_emb[jsel], (1, 0, 2)).reshape(CMP_BLOCK, KV_ROW)
    pm, lm = _hgrn_matrices()
    return {
        'attn_g': attn_norm_g.reshape(1, D_MODEL), 'w_pack': w_pack,
        'q_g': jnp.tile(q_norm_g, 2).reshape(1, LANES), 'k_g': jnp.tile(k_norm_g, (1, 2)),
        'lbl': hgrn_lb_logits.astype(F32), 'mseg': jnp.asarray(np.concatenate([mseg, mseg], axis=0), BF16),
        'cmp_pe': pe, 'cmp_w1': w1_bd, 'cmp_w2': w2_bd,
        'hgrn_g': hgrn_norm_g.reshape(1, B_KEY_DIM), 'hgrn_pm': pm, 'hgrn_lm': lm,
        'w_a': w_branch[:n_q].astype(BF16), 'w_b': w_branch[n_q:].astype(BF16), 'w_out': w_out.astype(BF16),
        'ffn_g': ffn_norm_g.reshape(1, D_MODEL), 'ffn_w_in': ffn_w_in.astype(BF16),
        'conv_w': ffn_conv_w, 'conv_b': ffn_conv_b.reshape(1, D_FF), 'ffn_w_out': ffn_w_out.astype(BF16),
    }


def _rope_tables(pos, reps):
    half = A_HEAD_DIM // 2
    inv = ROPE_THETA ** (-jnp.arange(half, dtype=F32) / half)
    ang = pos.astype(F32)[:, None] * inv[None, :]
    cos, sin = jnp.cos(ang), jnp.sin(ang)
    cos_t = jnp.tile(cos, (reps, LANES // half))
    sin_t = jnp.tile(jnp.concatenate([-sin, sin], axis=-1), (reps, LANES // A_HEAD_DIM))
    return cos_t, sin_t


def kernel(x_prompt, x_sample, cache_cmp_kv, cache_slc_kv, page_table, state_win_kv, state_hgrn, state_ffn_conv, attn_norm_g, w_in, q_norm_g, k_norm_g, cmp_pos_emb, cmp_w1, cmp_w2, hgrn_lb_logits, hgrn_norm_g, w_branch, w_out, ffn_norm_g, ffn_w_in, ffn_conv_w, ffn_conv_b, ffn_w_out):
    assert w_in.shape[0] == 1, "single-layer step"
    b, t, _ = x_prompt.shape
    db, ts, _ = x_sample.shape
    n_pool = cache_cmp_kv.shape[1]
    past = page_table.shape[1] * PAGE_SIZE
    wb = state_win_kv.shape[2]
    assert t % SLC_BLOCK == 0 and t % Q_BLOCK == 0
    consts = _prepare_consts(attn_norm_g[0], w_in[0], q_norm_g[0], k_norm_g[0], cmp_pos_emb[0], cmp_w1[0],
                             cmp_w2[0], hgrn_lb_logits, hgrn_norm_g[0], w_branch[0], w_out[0], ffn_norm_g[0],
                             ffn_w_in[0], ffn_conv_w[0], ffn_conv_b[0], ffn_w_out[0])
    kv_shape = (2, A_KV_GROUPS, A_HEAD_DIM)
    tm_p = min(256, t)
    n_s = db * ts

    fp = _inproj(x_prompt.reshape(b * t, D_MODEL), *_rope_tables(jnp.arange(t, dtype=jnp.int32), 1), consts, tm_p)
    seq = lambda a: a.reshape(b, t, a.shape[-1])
    kc_p, vc_p = _compress_dense(fp['kvc'].reshape(b, 2 * t, LANES), consts)
    oa_p = _nsa_prompt(seq(fp['q']), seq(fp['gate']), kc_p, vc_p, seq(fp['kvs_bf']), seq(fp['kvw_bf']), t)
    ob_p, s_p = _hgrn(seq(fp['qb']), seq(fp['kb']), seq(fp['vb']), seq(fp['lf']), seq(fp['gb']),
                      jnp.zeros((b, B_HEADS, B_KEY_DIM, B_KEY_DIM), F32), consts)
    y_p, conv_p = _merge_ffn(x_prompt.reshape(b * t, D_MODEL), oa_p.reshape(b * t, -1), ob_p.reshape(b * t, -1),
                             fp['mg'], jnp.zeros((b, FFN_CONV - 1, D_FF), F32), t, consts, min(512, t))

    fs = _inproj(x_sample.reshape(n_s, D_MODEL), *_rope_tables(past + jnp.arange(ts, dtype=jnp.int32), db),
                 consts, n_s)
    sseq = lambda a: a.reshape(db, ts, a.shape[-1])
    half = lambda a: a.reshape(db, 2 * ts, LANES)
    pool_view = lambda c: c[0].reshape(n_pool, 2 * PAGE_SIZE, LANES)
    kc_s, vc_s = _compress_paged(page_table, pool_view(cache_cmp_kv), half(fs['kvc']), consts)
    win_cat = jnp.concatenate([state_win_kv[0].reshape(db, wb, KV_ROW), sseq(fs['kvw'])], axis=1)
    win_pad = jnp.pad(win_cat, ((0, 0), (0, _round_up(wb + ts, LANES) - (wb + ts)), (0, 0)))
    oa_s = _nsa_sample(page_table, sseq(fs['q']), sseq(fs['gate']), kc_s, vc_s,
                       pool_view(cache_slc_kv), half(fs['kvs']), win_pad, wb + ts)
    ob_s, s_s = _hgrn(sseq(fs['qb']), sseq(fs['kb']), sseq(fs['vb']), sseq(fs['lf']), sseq(fs['gb']),
                      state_hgrn[0].astype(F32), consts)
    y_s, conv_s = _merge_ffn(x_sample.reshape(n_s, D_MODEL), oa_s.reshape(n_s, -1), ob_s.reshape(n_s, -1),
                             fs['mg'], state_ffn_conv[0], ts, consts, n_s)

    wkeep = min(WINDOW, t)
    return (y_p.reshape(b, t, D_MODEL), y_s.reshape(db, ts, D_MODEL),
            fp['kvc'].reshape(1, b, t, *kv_shape), fs['kvc'].reshape(1, db, ts, *kv_shape),
            fp['kvs'].reshape(1, b, t, *kv_shape), fs['kvs'].reshape(1, db, ts, *kv_shape),
            seq(fp['kvw'])[:, t - wkeep:].reshape(1, b, wkeep, *kv_shape),
            win_cat[:, ts:].reshape(1, db, wb, *kv_shape),
            s_p[None], s_s[None], conv_p[None], conv_s[None])
```

```python
import functools

import numpy as np
import jax
import jax.numpy as jnp
from jax import lax
from jax.experimental import pallas as pl
from jax.experimental.pallas import tpu as pltpu

F32 = jnp.float32
BF16 = jnp.bfloat16

D_MODEL = 1024
PAGE_SIZE = 128
A_HEADS = 8
A_KV_GROUPS = 2
A_HPG = A_HEADS // A_KV_GROUPS
A_HEAD_DIM = 64
CMP_BLOCK = 32
CMP_STRIDE = 16
SLC_BLOCK = 64
N_SELECT = 16
WINDOW = 512
Q_BLOCK = 128
ROPE_THETA = 10000.0
FORCE_SCORE = 1e4
NEG_INF = -1e30
B_HEADS = 4
B_KEY_DIM = 128
D_FF = 2816
FFN_CONV = 3
EPS = 1e-6

LANES = 128
SUBLANES = 8
KV_ROW = 2 * A_KV_GROUPS * A_HEAD_DIM
S_TILE = 256
W_TILE = 128
SEL_TK = 512
MASK_BIG = 1e30
CHUNK_PITCH = 24
HGRN_CHUNK = 128
HGRN_LEVELS = 7
VMEM_LIMIT = 56 * 1024 * 1024

OFF_Q, OFF_KVC, OFF_KVS, OFF_KVW = 0, 512, 768, 1024
OFF_QB, OFF_FB, OFF_IB, OFF_GB, OFF_MG, OFF_GATE = 1280, 1792, 2304, 2816, 3328, 5376
W_PACK = 5504


def _dot(a, b):
    return jnp.dot(a, b, preferred_element_type=F32)


def _dot_nt(a, b):
    return lax.dot_general(a, b, (((1,), (1,)), ((), ())), preferred_element_type=F32)


def _dot_tn(a, b):
    return lax.dot_general(a, b, (((0,), (0,)), ((), ())), preferred_element_type=F32)


def _split_bf16(x):
    hi = x.astype(BF16)
    lo = (x - hi.astype(F32)).astype(BF16)
    return hi, lo


def _round_up(n, m):
    return -(-n // m) * m


def _const_spec(shape):
    nd = len(shape)
    return pl.BlockSpec(shape, lambda *_: (0,) * nd, pipeline_mode=pl.Buffered(1))


def _params(semantics):
    return pltpu.CompilerParams(dimension_semantics=semantics, vmem_limit_bytes=VMEM_LIMIT)


INPROJ_COMMON = ('q', 'gate', 'qb', 'kb', 'vb', 'lf', 'gb', 'mg')
INPROJ_PROMPT = INPROJ_COMMON + ('kvc', 'kvc_t', 'kvs_t', 'kvw_t', 'ks_bf', 'vs_bf', 'kw_bf', 'vw_bf')
INPROJ_SAMPLE = INPROJ_COMMON + ('kvc', 'kvs', 'kvw')


def _inproj_kernel(x_ref, g_ref, w_ref, cos_ref, sin_ref, qg_ref, kg_ref, lbl_ref, mseg_ref, *out_refs, names):
    o = dict(zip(names, out_refs))
    q_ref, gate_ref, mg_ref = o['q'], o['gate'], o['mg']
    qb_ref, kb_ref, vb_ref, lf_ref, gb_ref = o['qb'], o['kb'], o['vb'], o['lf'], o['gb']
    x = x_ref[...]
    ms = jnp.mean(x * x, axis=-1, keepdims=True)
    h = (x * lax.rsqrt(ms + EPS) * g_ref[...]).astype(BF16)
    cos = cos_ref[...]
    sin = sin_ref[...]
    tm = x.shape[0]
    lane = lax.broadcasted_iota(jnp.int32, (tm, LANES), 1)
    first_half = (lane & (A_HEAD_DIM // 2)) == 0
    low64 = lane < A_HEAD_DIM
    mseg = mseg_ref[...]

    def proj(lo, width):
        return _dot(h, w_ref[:, lo:lo + width])

    def head_norm_rope(chunk, gain):
        s_hi, s_lo = _split_bf16(chunk * chunk)
        mean = _dot(jnp.concatenate([s_hi, s_lo], axis=-1), mseg)
        y = chunk * lax.rsqrt(mean + EPS) * gain
        rot = jnp.where(first_half, pltpu.roll(y, LANES - A_HEAD_DIM // 2, 1),
                        pltpu.roll(y, A_HEAD_DIM // 2, 1))
        return y * cos + rot * sin

    zq = proj(OFF_Q, A_HEADS * A_HEAD_DIM)
    qg = qg_ref[...]
    for k in range(A_HEADS // 2):
        c = head_norm_rope(zq[:, k * LANES:(k + 1) * LANES], qg) * (A_HEAD_DIM ** -0.5)
        r = pltpu.roll(c, A_HEAD_DIM, 1)
        if k < A_HPG // 2:
            h0, h1 = jnp.where(low64, c, 0.0), jnp.where(low64, r, 0.0)
        else:
            h0, h1 = jnp.where(low64, 0.0, r), jnp.where(low64, 0.0, c)
        q_ref[:, (2 * k) * LANES:(2 * k + 1) * LANES] = h0.astype(BF16)
        q_ref[:, (2 * k + 1) * LANES:(2 * k + 2) * LANES] = h1.astype(BF16)

    for i, (off, name) in enumerate(((OFF_KVC, 'kvc'), (OFF_KVS, 'kvs'), (OFF_KVW, 'kvw'))):
        z = proj(off, KV_ROW)
        rows = jnp.concatenate([head_norm_rope(z[:, :LANES], kg_ref[i:i + 1, :]), z[:, LANES:]], axis=-1)
        if name in o:
            o[name][...] = rows
        if name + '_t' in o:
            rows_t = rows.T
            o[name + '_t'][0] = rows_t
            for half, tag in ((rows_t[:LANES], 'k'), (rows_t[LANES:], 'v')):
                key = tag + name[-1] + '_bf'
                if key in o:
                    n_tiles, _, width = o[key].shape[1:]
                    for j in range(n_tiles):
                        o[key][0, j] = half[:, j * width:(j + 1) * width].astype(BF16)

    gate_ref[...] = jax.nn.sigmoid(proj(OFF_GATE, LANES))

    lbl = lbl_ref[...]
    e = jnp.exp(lbl - jnp.max(lbl, axis=0, keepdims=True))
    lb = e[0:1, :] / jnp.sum(e, axis=0, keepdims=True)
    fz = proj(OFF_FB, B_HEADS * B_KEY_DIM)
    lf_ref[...] = jnp.log(lb + (1.0 - lb) * jax.nn.sigmoid(fz))
    kb_ref[...] = (1.0 - lb) * jax.nn.sigmoid(-fz)
    zqb = proj(OFF_QB, B_HEADS * B_KEY_DIM)
    qb_ref[...] = zqb * jax.nn.sigmoid(zqb)
    vb_ref[...] = proj(OFF_IB, B_HEADS * B_KEY_DIM)
    zg = proj(OFF_GB, B_HEADS * B_KEY_DIM)
    gb_ref[...] = zg * jax.nn.sigmoid(zg)
    mg_ref[...] = jax.nn.sigmoid(proj(OFF_MG, 2 * D_MODEL))


def _inproj(x2d, cos_tab, sin_tab, consts, tm, seq_len=None):
    n = x2d.shape[0]
    tab_blocks = cos_tab.shape[0] // tm
    row = lambda w: pl.BlockSpec((tm, w), lambda i: (i, 0))
    tab = pl.BlockSpec((tm, LANES), lambda i: (i % tab_blocks, 0))
    token_major = {'q': (A_HEADS * LANES, BF16), 'gate': (LANES, F32), 'qb': (512, F32), 'kb': (512, F32),
                   'vb': (512, F32), 'lf': (512, F32), 'gb': (512, F32), 'mg': (2 * D_MODEL, F32),
                   'kvc': (KV_ROW, F32), 'kvs': (KV_ROW, F32), 'kvw': (KV_ROW, F32)}
    names = INPROJ_SAMPLE if seq_len is None else INPROJ_PROMPT
    specs, shapes = [], []
    for name in names:
        if name in token_major:
            w, dt = token_major[name]
            specs.append(row(w))
            shapes.append(jax.ShapeDtypeStruct((n, w), dt))
        elif name.endswith('_t'):
            per_seq = seq_len // tm
            specs.append(pl.BlockSpec((1, KV_ROW, tm), lambda i: (i // per_seq, 0, i % per_seq)))
            shapes.append(jax.ShapeDtypeStruct((n // seq_len, KV_ROW, seq_len), F32))
        else:
            width = S_TILE if name[1] == 's' else W_TILE
            per_seq = seq_len // tm
            specs.append(pl.BlockSpec((1, tm // width, LANES, width), lambda i: (i // per_seq, i % per_seq, 0, 0)))
            shapes.append(jax.ShapeDtypeStruct((n // seq_len, seq_len // width, LANES, width), BF16))
    outs = pl.pallas_call(
        functools.partial(_inproj_kernel, names=names),
        grid=(n // tm,),
        in_specs=[row(D_MODEL), _const_spec((1, D_MODEL)), _const_spec((D_MODEL, W_PACK)), tab, tab,
                  _const_spec((1, LANES)), _const_spec((3, LANES)),
                  _const_spec(consts['lbl'].shape), _const_spec((2 * LANES, LANES))],
        out_specs=specs,
        out_shape=shapes,
        compiler_params=_params(("parallel",)),
        name="inproj",
    )(x2d, consts['attn_g'], consts['w_pack'], cos_tab, sin_tab, consts['q_g'], consts['k_g'],
      consts['lbl'], consts['mseg'])
    return dict(zip(names, outs))


def _chunk_row(chunk):
    return chunk * CHUNK_PITCH


def _compress_rows(tok_k, tok_v, pe_ref, w1_ref, w2_ref, kc_ref, vc_ref, m_rows, n_cmp):
    acc = jnp.zeros((m_rows, KV_ROW), F32)
    for p in range(CMP_BLOCK):
        start = _chunk_row(p // CMP_STRIDE) + p % CMP_STRIDE
        xp = jnp.concatenate([tok_k[pl.ds(start, m_rows, stride=CHUNK_PITCH), :],
                              tok_v[pl.ds(start, m_rows, stride=CHUNK_PITCH), :]], axis=-1) + pe_ref[p:p + 1, :]
        acc = acc + _dot(xp.astype(BF16), w1_ref[p])
    hid = acc * jax.nn.sigmoid(acc)
    out = _dot(hid.astype(BF16), w2_ref[...])
    row = lax.broadcasted_iota(jnp.int32, out.shape, 0)
    out = jnp.where(row < n_cmp, out, 0.0)
    ncp = kc_ref.shape[1]
    kc_ref[0, 0:m_rows, :] = out[:, :LANES].astype(BF16)
    vc_ref[0, 0:m_rows, :] = out[:, LANES:].astype(BF16)
    if ncp > m_rows:
        kc_ref[0, m_rows:ncp, :] = jnp.zeros((ncp - m_rows, LANES), BF16)
        vc_ref[0, m_rows:ncp, :] = jnp.zeros((ncp - m_rows, LANES), BF16)


def _zero_chunks(tok_k, tok_v, first_chunk):
    r0 = _chunk_row(first_chunk)
    tok_k[r0:, :] = jnp.zeros((tok_k.shape[0] - r0, LANES), F32)
    tok_v[r0:, :] = jnp.zeros((tok_v.shape[0] - r0, LANES), F32)


def _compress_dense_kernel(rows_ref, pe_ref, w1_ref, w2_ref, kc_ref, vc_ref, tok_k, tok_v, *, m_rows, n_cmp):
    n_chunks = rows_ref.shape[1] // CMP_STRIDE

    def body(c, carry):
        src = pl.multiple_of(c * CMP_STRIDE, CMP_STRIDE)
        dst = pl.multiple_of(_chunk_row(c), SUBLANES)
        tok_k[pl.ds(dst, CMP_STRIDE), :] = rows_ref[0, pl.ds(src, CMP_STRIDE), 0:LANES]
        tok_v[pl.ds(dst, CMP_STRIDE), :] = rows_ref[0, pl.ds(src, CMP_STRIDE), LANES:KV_ROW]
        return carry

    lax.fori_loop(0, n_chunks, body, 0)
    _zero_chunks(tok_k, tok_v, n_chunks)
    _compress_rows(tok_k, tok_v, pe_ref, w1_ref, w2_ref, kc_ref, vc_ref, m_rows, n_cmp)


def _page_copy(pool_ref, page, stage, slot, sems):
    return pltpu.make_async_copy(pool_ref.at[page], stage.at[slot], sems.at[slot])


def _compress_paged_kernel(pt_ref, pool_ref, new_ref, pe_ref, w1_ref, w2_ref, kc_ref, vc_ref,
                           stage, tok_k, tok_v, sems, *, n_pages, m_rows, n_cmp):
    b = pl.program_id(0)

    def start(p, carry):
        _page_copy(pool_ref, pt_ref[b, p], stage, p, sems).start()
        return carry

    lax.fori_loop(0, n_pages, start, 0)

    chunks_per_page = PAGE_SIZE // CMP_STRIDE
    first_new = n_pages * chunks_per_page
    _zero_chunks(tok_k, tok_v, first_new)
    ts = new_ref.shape[1]
    assert ts <= CMP_STRIDE
    tok_k[_chunk_row(first_new):_chunk_row(first_new) + ts, :] = new_ref[0, :, 0:LANES]
    tok_v[_chunk_row(first_new):_chunk_row(first_new) + ts, :] = new_ref[0, :, LANES:KV_ROW]

    def land(p, carry):
        _page_copy(pool_ref, 0, stage, p, sems).wait()
        page_t = stage[p].T
        base = pl.multiple_of(_chunk_row(p * chunks_per_page), SUBLANES)
        for c in range(chunks_per_page):
            rows = slice(c * CMP_STRIDE, (c + 1) * CMP_STRIDE)
            tok_k[pl.ds(base + _chunk_row(c), CMP_STRIDE), :] = page_t[rows, 0:LANES]
            tok_v[pl.ds(base + _chunk_row(c), CMP_STRIDE), :] = page_t[rows, LANES:KV_ROW]
        return carry

    lax.fori_loop(0, n_pages, land, 0)
    _compress_rows(tok_k, tok_v, pe_ref, w1_ref, w2_ref, kc_ref, vc_ref, m_rows, n_cmp)


def _compress_geometry(t_real):
    t_pad = _round_up(t_real, SLC_BLOCK)
    n_cmp = t_pad // CMP_STRIDE - CMP_BLOCK // CMP_STRIDE + 1
    m_rows = _round_up(n_cmp, SUBLANES)
    ncp = _round_up(n_cmp, LANES)
    n_chunks = max(m_rows + CMP_BLOCK // CMP_STRIDE - 1, -(-t_real // CMP_STRIDE))
    return n_cmp, m_rows, ncp, _chunk_row(n_chunks)


def _compress_dense(rows, consts):
    b, t, _ = rows.shape
    n_cmp, m_rows, ncp, tok_rows = _compress_geometry(t)
    return pl.pallas_call(
        functools.partial(_compress_dense_kernel, m_rows=m_rows, n_cmp=n_cmp),
        grid=(b,),
        in_specs=[pl.BlockSpec((1, t, KV_ROW), lambda i: (i, 0, 0)),
                  _const_spec((CMP_BLOCK, KV_ROW)), _const_spec((CMP_BLOCK, KV_ROW, KV_ROW)),
                  _const_spec((KV_ROW, KV_ROW))],
        out_specs=[pl.BlockSpec((1, ncp, LANES), lambda i: (i, 0, 0))] * 2,
        out_shape=[jax.ShapeDtypeStruct((b, ncp, LANES), BF16)] * 2,
        scratch_shapes=[pltpu.VMEM((tok_rows, LANES), F32)] * 2,
        compiler_params=_params(("parallel",)),
        name="compress_dense",
    )(rows, consts['cmp_pe'], consts['cmp_w1'], consts['cmp_w2'])


def _compress_paged(page_table, pool, new_rows, consts):
    db, n_pages = page_table.shape
    ts = new_rows.shape[1]
    n_cmp, m_rows, ncp, tok_rows = _compress_geometry(n_pages * PAGE_SIZE + ts)
    cm = lambda nd: (lambda i, pt: (0,) * nd)
    grid_spec = pltpu.PrefetchScalarGridSpec(
        num_scalar_prefetch=1,
        grid=(db,),
        in_specs=[pl.BlockSpec(memory_space=pl.ANY),
                  pl.BlockSpec((1, ts, KV_ROW), lambda i, pt: (i, 0, 0)),
                  pl.BlockSpec((CMP_BLOCK, KV_ROW), cm(2), pipeline_mode=pl.Buffered(1)),
                  pl.BlockSpec((CMP_BLOCK, KV_ROW, KV_ROW), cm(3), pipeline_mode=pl.Buffered(1)),
                  pl.BlockSpec((KV_ROW, KV_ROW), cm(2), pipeline_mode=pl.Buffered(1))],
        out_specs=[pl.BlockSpec((1, ncp, LANES), lambda i, pt: (i, 0, 0))] * 2,
        scratch_shapes=[pltpu.VMEM((n_pages, KV_ROW, PAGE_SIZE), F32), pltpu.VMEM((tok_rows, LANES), F32),
                        pltpu.VMEM((tok_rows, LANES), F32), pltpu.SemaphoreType.DMA((n_pages,))],
    )
    return pl.pallas_call(
        functools.partial(_compress_paged_kernel, n_pages=n_pages, m_rows=m_rows, n_cmp=n_cmp),
        grid_spec=grid_spec,
        out_shape=[jax.ShapeDtypeStruct((db, ncp, LANES), BF16)] * 2,
        compiler_params=_params(("arbitrary",)),
        name="compress_paged",
    )(page_table, pool, new_rows, consts['cmp_pe'], consts['cmp_w1'], consts['cmp_w2'])


def _softmax_rows(s, mask):
    s = jnp.where(mask, s, NEG_INF)
    p = jnp.where(mask, jnp.exp(s - jnp.max(s, axis=-1, keepdims=True)), 0.0)
    return p / jnp.maximum(jnp.sum(p, axis=-1, keepdims=True), 1e-30)


def _online_step(carry, s, v):
    m, l, acc = carry
    m_new = jnp.maximum(m, jnp.max(s, axis=-1, keepdims=True))
    alpha = jnp.exp(m - m_new)
    p = jnp.exp(s - m_new)
    l = alpha * l + jnp.sum(p, axis=-1, keepdims=True)
    acc = alpha * acc + _dot(p.astype(BF16), v)
    return m_new, l, acc


def _online_init(rows):
    return (jnp.full((rows, 1), NEG_INF, F32), jnp.zeros((rows, 1), F32), jnp.zeros((rows, LANES), F32))


def _online_finish(carry):
    _, l, acc = carry
    return acc / jnp.maximum(l, 1e-30)


def _block_scores(p_sum, cover_ref, qpos, n_blocks):
    hi, lo = _split_bf16(p_sum)
    imp = _dot(jnp.concatenate([hi, lo], axis=-1), cover_ref[...])
    blk = lax.broadcasted_iota(jnp.int32, imp.shape, 1)
    cur = qpos // SLC_BLOCK
    forced = (blk == 0) | (blk == cur) | (blk == cur - 1)
    score = jnp.where(forced, FORCE_SCORE, jnp.where(blk <= cur, imp, -1.0))
    return jnp.where(blk < n_blocks, score, -2.0)


def _topk_select(score, n_blocks, n_sel):
    lane = lax.broadcasted_iota(jnp.int32, score.shape, 1)
    cnt = jnp.zeros(score.shape, F32)
    for s in range(n_blocks):
        col = score[:, s:s + 1]
        beats = (col > score) | ((col == score) & (lane > s))
        cnt = cnt + jnp.where(beats, 1.0, 0.0)
    return jnp.where((cnt < n_sel) & (lane < n_blocks), 1.0, 0.0)


def _assemble_heads(heads, low64):
    chunks = []
    for k in range(A_HEADS // 2):
        a, b = heads[2 * k], heads[2 * k + 1]
        if k < A_HPG // 2:
            chunks.append(jnp.where(low64, a, pltpu.roll(b, A_HEAD_DIM, 1)))
        else:
            chunks.append(jnp.where(low64, pltpu.roll(a, A_HEAD_DIM, 1), b))
    return chunks


def _topk_rows(score_t, n_blocks, n_sel):
    n_tiles = score_t.shape[0] // SUBLANES
    tiles = [score_t[t * SUBLANES:(t + 1) * SUBLANES] for t in range(n_tiles)]
    sub = lax.broadcasted_iota(jnp.int32, tiles[0].shape, 0)
    cnt = [jnp.zeros(tiles[0].shape, F32) for _ in range(n_tiles)]
    for s in range(n_blocks):
        row = score_t[s:s + 1, :]
        for t in range(n_tiles):
            if t * SUBLANES > s:
                beats = row >= tiles[t]
            elif (t + 1) * SUBLANES - 1 < s:
                beats = row > tiles[t]
            else:
                later = sub + t * SUBLANES > s
                beats = (row > tiles[t]) | ((row == tiles[t]) & later)
            cnt[t] = cnt[t] + jnp.where(beats, 1.0, 0.0)
    blk = lax.broadcasted_iota(jnp.int32, score_t.shape, 0)
    return jnp.where((jnp.concatenate(cnt, axis=0) < n_sel) & (blk < n_blocks), 1.0, 0.0)


def _online_step_t(carry, s, v_t):
    m, l, acc = carry
    m_new = jnp.maximum(m, jnp.max(s, axis=-1, keepdims=True))
    alpha = jnp.exp(m - m_new)
    p = jnp.exp(s - m_new)
    l = alpha * l + jnp.sum(p, axis=-1, keepdims=True)
    acc = alpha * acc + _dot_nt(p.astype(BF16), v_t)
    return m_new, l, acc


def _nsa_prompt_kernel(q_ref, gate_ref, kc_ref, vc_ref, ks_ref, vs_ref, kw_ref, vw_ref, cover_ref, exp_ref,
                       o_ref, *, n_blocks, n_sel):
    i = pl.program_id(1)
    s0 = i * Q_BLOCK
    q = q_ref[0]
    gates = gate_ref[0]
    rows = A_HEADS * Q_BLOCK
    q_all = jnp.concatenate([q[:, h * LANES:(h + 1) * LANES] for h in range(A_HEADS)], axis=0)
    qpos1 = s0 + lax.broadcasted_iota(jnp.int32, (Q_BLOCK, 1), 0)
    qpos_all = jnp.concatenate([qpos1] * A_HEADS, axis=0)

    ncp = kc_ref.shape[1]
    c_end = lax.broadcasted_iota(jnp.int32, (1, ncp), 1) * CMP_STRIDE + (CMP_BLOCK - 1)
    p_c = _softmax_rows(_dot_nt(q_all, kc_ref[0]), c_end <= qpos_all)
    o_c = _dot(p_c.astype(BF16), vc_ref[0])

    cur = (s0 + lax.broadcasted_iota(jnp.int32, (1, Q_BLOCK), 1)) // SLC_BLOCK
    blk = lax.broadcasted_iota(jnp.int32, (LANES, Q_BLOCK), 0)
    forced = (blk == 0) | (blk == cur) | (blk == cur - 1)
    scores = []
    for g in range(A_KV_GROUPS):
        base = g * A_HPG * Q_BLOCK
        p_sum = p_c[base:base + Q_BLOCK]
        for hh in range(1, A_HPG):
            p_sum = p_sum + p_c[base + hh * Q_BLOCK:base + (hh + 1) * Q_BLOCK]
        hi, lo = _split_bf16(p_sum)
        imp_t = _dot(jnp.concatenate([hi, lo], axis=-1), cover_ref[...]).T
        score = jnp.where(forced, FORCE_SCORE, jnp.where(blk <= cur, imp_t, -1.0))
        scores.append(jnp.where(blk < n_blocks, score, -2.0))
    nb8 = _round_up(n_blocks, SUBLANES)
    sel_t = _topk_rows(jnp.concatenate(scores, axis=1)[0:nb8], n_blocks, n_sel)
    if nb8 < LANES:
        sel_t = jnp.concatenate([sel_t, jnp.zeros((LANES - nb8, sel_t.shape[1]), F32)], axis=0)

    aug = []
    for g in range(A_KV_GROUPS):
        sel = sel_t[:, g * Q_BLOCK:(g + 1) * Q_BLOCK].T
        aug += [((sel - 1.0) * MASK_BIG).astype(BF16)] * A_HPG
    q_aug = jnp.concatenate([q_all, jnp.concatenate(aug, axis=0)], axis=1)
    per_tile = SEL_TK // S_TILE

    def sel_tile(kt):
        cat = lambda ref_tile: jnp.concatenate([ref_tile(kt * per_tile + j) for j in range(per_tile)], axis=1)
        k_aug = jnp.concatenate([cat(lambda n: ks_ref[0, n]), cat(lambda n: exp_ref[n])], axis=0)
        return _dot(q_aug, k_aug), cat(lambda n: vs_ref[0, n])

    def sel_body(kt, carry):
        return _online_step_t(carry, *sel_tile(kt))

    n_full = s0 // SEL_TK
    carry = lax.fori_loop(0, n_full, sel_body, _online_init(rows))
    s_last, v_last = sel_tile(n_full)
    kpos = n_full * SEL_TK + lax.broadcasted_iota(jnp.int32, (1, SEL_TK), 1)
    o_s = _online_finish(_online_step_t(carry, s_last + jnp.where(kpos <= qpos_all, 0.0, NEG_INF), v_last))

    st = jnp.maximum(i - WINDOW // Q_BLOCK, 0) * (Q_BLOCK // W_TILE)
    n_wt = (WINDOW + Q_BLOCK) // W_TILE
    kw_t = jnp.concatenate([kw_ref[0, st + j] for j in range(n_wt)], axis=1)
    vw_t = jnp.concatenate([vw_ref[0, st + j] for j in range(n_wt)], axis=1)
    d = qpos_all - (st * W_TILE + lax.broadcasted_iota(jnp.int32, (1, n_wt * W_TILE), 1))
    p_w = _softmax_rows(_dot(q_all, kw_t), (d >= 0) & (d < WINDOW))
    o_w = _dot_nt(p_w.astype(BF16), vw_t)

    heads = []
    for h in range(A_HEADS):
        r = slice(h * Q_BLOCK, (h + 1) * Q_BLOCK)
        heads.append(gates[:, 3 * h:3 * h + 1] * o_c[r] + gates[:, 3 * h + 1:3 * h + 2] * o_s[r]
                     + gates[:, 3 * h + 2:3 * h + 3] * o_w[r])
    low64 = lax.broadcasted_iota(jnp.int32, (Q_BLOCK, LANES), 1) < A_HEAD_DIM
    for k, chunk in enumerate(_assemble_heads(heads, low64)):
        o_ref[0, :, k * LANES:(k + 1) * LANES] = chunk.astype(BF16)


def _cover_matrix(n_cmp, ncp, n_blocks, nsp):
    c = np.arange(ncp)[:, None]
    s = np.arange(nsp)[None, :]
    cover = ((c * CMP_STRIDE < s * SLC_BLOCK + SLC_BLOCK) & (c * CMP_STRIDE + CMP_BLOCK > s * SLC_BLOCK)
             & (c < n_cmp) & (s < n_blocks))
    return jnp.asarray(np.concatenate([cover, cover], axis=0), BF16)


def _expand_matrix(n_keys, block0=0):
    e = (np.arange(n_keys)[None, :] // SLC_BLOCK) == (block0 + np.arange(LANES)[:, None])
    return e


def _expand_tiles(n_keys, tk):
    e = _expand_matrix(n_keys).reshape(LANES, n_keys // tk, tk)
    return jnp.asarray(np.transpose(e, (1, 0, 2)), BF16)


def _nsa_prompt(q, gates, kc, vc, ks_bf, vs_bf, kw_bf, vw_bf, t):
    b = q.shape[0]
    ncp = kc.shape[1]
    n_cmp = t // CMP_STRIDE - CMP_BLOCK // CMP_STRIDE + 1
    n_blocks = t // SLC_BLOCK
    assert n_blocks <= LANES and t % SEL_TK == 0 and t >= WINDOW + Q_BLOCK
    cover = _cover_matrix(n_cmp, ncp, n_blocks, LANES)
    expand = _expand_tiles(t, S_TILE)
    per_b = lambda rows, w: pl.BlockSpec((1, rows, w), lambda bi, i: (bi, 0, 0))
    tiles = lambda a: pl.BlockSpec((1,) + a.shape[1:], lambda bi, i: (bi, 0, 0, 0))
    return pl.pallas_call(
        functools.partial(_nsa_prompt_kernel, n_blocks=n_blocks, n_sel=min(N_SELECT, n_blocks)),
        grid=(b, t // Q_BLOCK),
        in_specs=[pl.BlockSpec((1, Q_BLOCK, A_HEADS * LANES), lambda bi, i: (bi, i, 0)),
                  pl.BlockSpec((1, Q_BLOCK, LANES), lambda bi, i: (bi, i, 0)),
                  per_b(ncp, LANES), per_b(ncp, LANES),
                  tiles(ks_bf), tiles(vs_bf), tiles(kw_bf), tiles(vw_bf),
                  _const_spec(cover.shape), _const_spec(expand.shape)],
        out_specs=pl.BlockSpec((1, Q_BLOCK, A_HEADS * A_HEAD_DIM), lambda bi, i: (bi, i, 0)),
        out_shape=jax.ShapeDtypeStruct((b, t, A_HEADS * A_HEAD_DIM), BF16),
        compiler_params=_params(("parallel", "parallel")),
        name="nsa_prompt",
    )(q, gates, kc, vc, ks_bf, vs_bf, kw_bf, vw_bf, cover, expand)


def _key_page_copy(pool_ref, page, buf, slot, sem):
    return pltpu.make_async_copy(pool_ref.at[page], buf.at[:, pl.ds(pl.multiple_of(slot * PAGE_SIZE, PAGE_SIZE),
                                                                   PAGE_SIZE)], sem)


def _nsa_sample_kernel(pt_ref, q_ref, gate_ref, kc_ref, vc_ref, pool_ref, new_ref, win_ref, wnew_ref,
                       cover_ref, exp_ref, o_ref, buf, sem, *, n_pages, n_blocks, n_sel, key_chunk):
    b = pl.program_id(0)

    def start(p, carry):
        _key_page_copy(pool_ref, pt_ref[b, p], buf, p, sem).start()
        return carry

    lax.fori_loop(0, n_pages, start, 0)
    past = n_pages * PAGE_SIZE
    ts = q_ref.shape[1]
    buf[:, past:past + LANES] = new_ref[0]

    q = q_ref[0].astype(F32)
    q_all = jnp.concatenate([q[:, h * LANES:(h + 1) * LANES] for h in range(A_HEADS)], axis=0).astype(BF16)
    rows = A_HEADS * ts
    qpos1 = past + lax.broadcasted_iota(jnp.int32, (ts, 1), 0)
    qpos_all = jnp.concatenate([qpos1] * A_HEADS, axis=0)

    ncp = kc_ref.shape[1]
    c_end = lax.broadcasted_iota(jnp.int32, (1, ncp), 1) * CMP_STRIDE + (CMP_BLOCK - 1)
    p_c = _softmax_rows(_dot_nt(q_all, kc_ref[0]), c_end <= qpos_all)
    o_c = _dot(p_c.astype(BF16), vc_ref[0])

    p_groups = []
    for g in range(A_KV_GROUPS):
        base = g * A_HPG * ts
        p_sum = p_c[base:base + ts]
        for hh in range(1, A_HPG):
            p_sum = p_sum + p_c[base + hh * ts:base + (hh + 1) * ts]
        p_groups.append(p_sum)
    qpos_g = jnp.concatenate([qpos1] * A_KV_GROUPS, axis=0)
    score = _block_scores(jnp.concatenate(p_groups, axis=0), cover_ref, qpos_g, n_blocks)
    sel = _topk_select(score, n_blocks, n_sel).astype(BF16)

    def wait(p, carry):
        _key_page_copy(pool_ref, 0, buf, p, sem).wait()
        return carry

    lax.fori_loop(0, n_pages, wait, 0)

    n_keys = buf.shape[1]
    carry = _online_init(rows)
    for ck in range(-(-n_keys // key_chunk)):
        k0 = ck * key_chunk
        kn = min(key_chunk, n_keys - k0)
        s = _dot(q_all, buf[0:LANES, k0:k0 + kn].astype(BF16))
        blk0 = ck * (key_chunk // SLC_BLOCK)
        em = _dot(sel[:, blk0:blk0 + LANES], exp_ref[:, 0:kn])
        kpos = k0 + lax.broadcasted_iota(jnp.int32, (1, kn), 1)
        bias = jnp.where((em > 0.5) & (kpos <= qpos_g), 0.0, NEG_INF)
        biases = []
        for g in range(A_KV_GROUPS):
            biases += [bias[g * ts:(g + 1) * ts]] * A_HPG
        carry = _online_step_t(carry, s + jnp.concatenate(biases, axis=0),
                               buf[LANES:KV_ROW, k0:k0 + kn].astype(BF16))
    o_s = _online_finish(carry)

    wb = win_ref.shape[2]
    kv_w = jnp.concatenate([win_ref[0], wnew_ref[0]], axis=1).astype(BF16)
    idx = lax.broadcasted_iota(jnp.int32, (1, wb + LANES), 1)
    w_pos = jnp.where(idx < wb, past - wb + idx, past + idx - wb)
    d = qpos_all - w_pos
    m_w = (d >= 0) & (d < WINDOW) & (w_pos >= 0)
    p_w = _softmax_rows(_dot(q_all, kv_w[0:LANES]), m_w)
    o_w = _dot_nt(p_w.astype(BF16), kv_w[LANES:KV_ROW])

    gates = gate_ref[0]
    heads = []
    for h in range(A_HEADS):
        r = slice(h * ts, (h + 1) * ts)
        heads.append(gates[:, 3 * h:3 * h + 1] * o_c[r] + gates[:, 3 * h + 1:3 * h + 2] * o_s[r]
                     + gates[:, 3 * h + 2:3 * h + 3] * o_w[r])
    low64 = lax.broadcasted_iota(jnp.int32, (ts, LANES), 1) < A_HEAD_DIM
    for k, chunk in enumerate(_assemble_heads(heads, low64)):
        o_ref[0, :, k * LANES:(k + 1) * LANES] = chunk.astype(BF16)


def _nsa_sample(page_table, q, gates, kc, vc, pool, new_tile, win_t, wnew_tile):
    db, n_pages = page_table.shape
    wb = win_t.shape[2]
    assert wb % LANES == 0
    ts = q.shape[1]
    past = n_pages * PAGE_SIZE
    ncp = kc.shape[1]
    t_pad = _round_up(past + ts, SLC_BLOCK)
    n_cmp = t_pad // CMP_STRIDE - CMP_BLOCK // CMP_STRIDE + 1
    n_blocks = t_pad // SLC_BLOCK
    key_chunk = LANES * SLC_BLOCK
    n_keys = past + LANES
    nsp = LANES * (-(-n_keys // key_chunk))
    assert nsp >= n_blocks and ts % SUBLANES == 0
    cover = _cover_matrix(n_cmp, ncp, n_blocks, nsp)
    expand = jnp.asarray(_expand_matrix(min(key_chunk, n_keys)), BF16)
    cm = lambda nd: (lambda i, pt: (0,) * nd)
    per_b = lambda r, w: pl.BlockSpec((1, r, w), lambda i, pt: (i, 0, 0))
    grid_spec = pltpu.PrefetchScalarGridSpec(
        num_scalar_prefetch=1,
        grid=(db,),
        in_specs=[per_b(ts, A_HEADS * LANES), per_b(ts, LANES), per_b(ncp, LANES), per_b(ncp, LANES),
                  pl.BlockSpec(memory_space=pl.ANY), per_b(KV_ROW, LANES), per_b(KV_ROW, wb), per_b(KV_ROW, LANES),
                  pl.BlockSpec(cover.shape, cm(2), pipeline_mode=pl.Buffered(1)),
                  pl.BlockSpec(expand.shape, cm(2), pipeline_mode=pl.Buffered(1))],
        out_specs=per_b(ts, A_HEADS * A_HEAD_DIM),
        scratch_shapes=[pltpu.VMEM((KV_ROW, n_keys), F32), pltpu.SemaphoreType.DMA(())],
    )
    return pl.pallas_call(
        functools.partial(_nsa_sample_kernel, n_pages=n_pages, n_blocks=n_blocks,
                          n_sel=min(N_SELECT, n_blocks), key_chunk=key_chunk),
        grid_spec=grid_spec,
        out_shape=jax.ShapeDtypeStruct((db, ts, A_HEADS * A_HEAD_DIM), BF16),
        compiler_params=_params(("arbitrary",)),
        name="nsa_sample",
    )(page_table, q, gates, kc, vc, pool, new_tile, win_t, wnew_tile, cover, expand)


def _hgrn_matrices():
    c = HGRN_CHUNK
    t = np.arange(c)[:, None]
    u = np.arange(c)[None, :]
    mats = [u <= t]
    masks = [t == u]
    for lvl in range(HGRN_LEVELS):
        m = 1 << lvl
        mid = (t // (2 * m)) * (2 * m) + m - 1
        mats.append((u > mid) & (u <= t))
        mats.append((u > t) & (u <= mid))
        masks.append((t // (2 * m) == u // (2 * m)) & (t % (2 * m) >= m) & (u % (2 * m) < m))
    mats.append(u > t)
    pm = np.concatenate(mats, axis=0)
    return (jnp.asarray(np.concatenate([pm, pm], axis=1), BF16),
            jnp.asarray(np.stack(masks).astype(np.float32)))


def _hgrn_kernel(qb_ref, kb_ref, vb_ref, lf_ref, gb_ref, s0_ref, gn_ref, pm_ref, lm_ref,
                 ob_ref, sout_ref, st_scr, pad_scr):
    j = pl.program_id(1)
    c = HGRN_CHUNK
    t_blk = qb_ref.shape[1]

    @pl.when(j == 0)
    def _():
        for h in range(B_HEADS):
            st_scr[h] = s0_ref[0, h].T

    def load(ref, slot):
        if t_blk == c:
            return ref[0]
        pad_scr[slot] = jnp.zeros((c, B_HEADS * B_KEY_DIM), F32)
        pad_scr[slot, 0:t_blk, :] = ref[0]
        return pad_scr[slot]

    qb, kb, vb, lf = load(qb_ref, 0), load(kb_ref, 1), load(vb_ref, 2), load(lf_ref, 3)
    lf_hi, lf_lo = _split_bf16(lf)
    ex = _dot(pm_ref[...], jnp.concatenate([lf_hi, lf_lo], axis=0))
    gn = gn_ref[...]
    for h in range(B_HEADS):
        sl = slice(h * B_KEY_DIM, (h + 1) * B_KEY_DIM)
        q, k = qb[:, sl], kb[:, sl]
        v = vb[:, sl].astype(BF16)
        b_cum = ex[0:c, sl]
        a = lm_ref[0] * _dot_nt(q.astype(BF16), k.astype(BF16))
        for lvl in range(HGRN_LEVELS):
            eq = ex[(2 * lvl + 1) * c:(2 * lvl + 2) * c, sl]
            ek = ex[(2 * lvl + 2) * c:(2 * lvl + 3) * c, sl]
            a = a + lm_ref[lvl + 1] * _dot_nt((q * jnp.exp(eq)).astype(BF16), (k * jnp.exp(ek)).astype(BF16))
        st = st_scr[h]
        o = _dot_nt((q * jnp.exp(b_cum)).astype(BF16), st.astype(BF16)) + _dot(a.astype(BF16), v)
        e_end = ex[(2 * HGRN_LEVELS + 1) * c:(2 * HGRN_LEVELS + 2) * c, sl]
        st_scr[h] = st * jnp.exp(b_cum[c - 1:c, :]) + _dot_tn(v, (k * jnp.exp(e_end)).astype(BF16))
        y = o * lax.rsqrt(jnp.mean(o * o, axis=-1, keepdims=True) + EPS) * gn
        ob_ref[0, :, sl] = y[0:t_blk] * gb_ref[0, :, sl]

    @pl.when(j == pl.num_programs(1) - 1)
    def _():
        for h in range(B_HEADS):
            sout_ref[0, h] = st_scr[h].T


def _hgrn(qb, kb, vb, lf, gb, s0, consts):
    b, t, w = qb.shape
    t_blk = min(t, HGRN_CHUNK)
    assert t % t_blk == 0 and (t_blk == HGRN_CHUNK or t == t_blk)
    tok = pl.BlockSpec((1, t_blk, w), lambda bi, j: (bi, j, 0))
    st = pl.BlockSpec((1, B_HEADS, B_KEY_DIM, B_KEY_DIM), lambda bi, j: (bi, 0, 0, 0))
    return pl.pallas_call(
        _hgrn_kernel,
        grid=(b, t // t_blk),
        in_specs=[tok, tok, tok, tok, tok, st, _const_spec((1, B_KEY_DIM)),
                  _const_spec(consts['hgrn_pm'].shape), _const_spec(consts['hgrn_lm'].shape)],
        out_specs=[tok, st],
        out_shape=[jax.ShapeDtypeStruct((b, t, w), F32),
                   jax.ShapeDtypeStruct((b, B_HEADS, B_KEY_DIM, B_KEY_DIM), F32)],
        scratch_shapes=[pltpu.VMEM((B_HEADS, B_KEY_DIM, B_KEY_DIM), F32),
                        pltpu.VMEM((4, HGRN_CHUNK, w), F32)],
        compiler_params=_params(("parallel", "arbitrary")),
        name="hgrn",
    )(qb, kb, vb, lf, gb, s0, consts['hgrn_g'], consts['hgrn_pm'], consts['hgrn_lm'])


def _merge_ffn_kernel(x_ref, oa_ref, ob_ref, mg_ref, p1_ref, p2_ref, wa_ref, wb_ref, wo_ref, fg_ref, win_ref,
                      cw_ref, cb_ref, wout_ref, y_ref, a_ref, carry_scr, *, seq_len):
    tm = x_ref.shape[0]
    mg = mg_ref[...]
    m = (mg[:, :D_MODEL] * _dot(oa_ref[...], wa_ref[...])
         + mg[:, D_MODEL:] * _dot(ob_ref[...].astype(BF16), wb_ref[...]))
    x2 = x_ref[...] + _dot(m.astype(BF16), wo_ref[...])
    h = (x2 * lax.rsqrt(jnp.mean(x2 * x2, axis=-1, keepdims=True) + EPS) * fg_ref[...]).astype(BF16)
    a = _dot(h, win_ref[:, :D_FF])
    gate = _dot(h, win_ref[:, D_FF:])
    row = lax.broadcasted_iota(jnp.int32, (tm, 1), 0)
    if seq_len >= tm:
        j = pl.program_id(0) % (seq_len // tm)

        @pl.when(j == 0)
        def _():
            carry_scr[0:2, :] = p1_ref[0]

        prev = carry_scr[...]
        a1 = jnp.where(row == 0, prev[1:2], pltpu.roll(a, 1, 0))
        a2 = jnp.where(row == 0, prev[0:1], jnp.where(row == 1, prev[1:2], pltpu.roll(a, 2, 0)))
        carry_scr[0:2, :] = a[tm - 2:tm]
        a_ref[0] = a[tm - 2:tm]
    else:
        t = row % seq_len
        a1 = jnp.where(t == 0, p1_ref[...], pltpu.roll(a, 1, 0))
        a2 = jnp.where(t < 2, p2_ref[...], pltpu.roll(a, 2, 0))
        a_ref[...] = a
    a_conv = cb_ref[...] + a2 * cw_ref[0:1, :] + a1 * cw_ref[1:2, :] + a * cw_ref[2:3, :]
    act = a_conv * jax.nn.sigmoid(a_conv) * gate
    y_ref[...] = x2 + _dot(act.astype(BF16), wout_ref[...])


def _merge_ffn(x2d, oa, ob, mg, conv_state, seq_len, consts, tm):
    n = x2d.shape[0]
    b = n // seq_len
    row = lambda w: pl.BlockSpec((tm, w), lambda i: (i, 0))
    if seq_len >= tm:
        assert seq_len % tm == 0
        per_seq = seq_len // tm
        p1, p2 = conv_state, conv_state
        p_spec = pl.BlockSpec((1, FFN_CONV - 1, D_FF), lambda i: (i // per_seq, 0, 0))
        a_shape = jax.ShapeDtypeStruct((b, FFN_CONV - 1, D_FF), F32)
        a_spec = pl.BlockSpec((1, FFN_CONV - 1, D_FF), lambda i: (i // per_seq, 0, 0))
    else:
        assert tm % seq_len == 0 and seq_len >= FFN_CONV - 1
        zeros = jnp.zeros((b, seq_len - 1, D_FF), F32)
        p1 = jnp.concatenate([conv_state[:, 1:2], zeros], axis=1).reshape(n, D_FF)
        p2 = jnp.concatenate([conv_state, zeros[:, 1:]], axis=1).reshape(n, D_FF)
        p_spec = row(D_FF)
        a_shape = jax.ShapeDtypeStruct((n, D_FF), F32)
        a_spec = row(D_FF)
    y, a_out = pl.pallas_call(
        functools.partial(_merge_ffn_kernel, seq_len=seq_len),
        grid=(n // tm,),
        in_specs=[row(D_MODEL), row(A_HEADS * A_HEAD_DIM), row(B_HEADS * B_KEY_DIM), row(2 * D_MODEL),
                  p_spec, p_spec,
                  _const_spec((A_HEADS * A_HEAD_DIM, D_MODEL)), _const_spec((B_HEADS * B_KEY_DIM, D_MODEL)),
                  _const_spec((D_MODEL, D_MODEL)), _const_spec((1, D_MODEL)),
                  _const_spec((D_MODEL, 2 * D_FF)), _const_spec((FFN_CONV, D_FF)), _const_spec((1, D_FF)),
                  _const_spec((D_FF, D_MODEL))],
        out_specs=[row(D_MODEL), a_spec],
        out_shape=[jax.ShapeDtypeStruct((n, D_MODEL), F32), a_shape],
        scratch_shapes=[pltpu.VMEM((SUBLANES, D_FF), F32)],
        compiler_params=_params(("arbitrary",)),
        name="merge_ffn",
    )(x2d, oa, ob, mg, p1, p2, consts['w_a'], consts['w_b'], consts['w_out'], consts['ffn_g'],
      consts['ffn_w_in'], consts['conv_w'], consts['conv_b'], consts['ffn_w_out'])
    if seq_len >= tm:
        return y, a_out
    return y, a_out.reshape(b, seq_len, D_FF)[:, seq_len - (FFN_CONV - 1):]


def _prepare_consts(attn_norm_g, w_in, q_norm_g, k_norm_g, cmp_pos_emb, cmp_w1, cmp_w2, hgrn_lb_logits,
                    hgrn_norm_g, w_branch, w_out, ffn_norm_g, ffn_w_in, ffn_conv_w, ffn_conv_b, ffn_w_out):
    n_q = A_HEADS * A_HEAD_DIM
    gate_lo = n_q + 3 * KV_ROW
    gate_hi = gate_lo + 3 * A_HEADS
    w_pack = jnp.concatenate([w_in[:, :gate_lo], w_in[:, gate_hi:], w_in[:, gate_lo:gate_hi],
                              jnp.zeros((D_MODEL, LANES - 3 * A_HEADS), w_in.dtype)], axis=1).astype(BF16)
    seg = np.arange(LANES) // A_HEAD_DIM
    mseg = (seg[:, None] == seg[None, :]).astype(np.float32) / A_HEAD_DIM
    eye = jnp.eye(2 * A_KV_GROUPS, dtype=F32)
    jsel = np.repeat(np.arange(2), A_KV_GROUPS)
    w1 = cmp_w1.reshape(2, CMP_BLOCK, A_HEAD_DIM, A_HEAD_DIM)[jsel]
    w1_bd = jnp.einsum('ab,apde->padbe', eye, w1).reshape(CMP_BLOCK, KV_ROW, KV_ROW).astype(BF16)
    w2_bd = jnp.einsum('ab,ade->adbe', eye, cmp_w2[jsel]).reshape(KV_ROW, KV_ROW).astype(BF16)
    pe = jnp.transpose(cmp_pos_emb[jsel], (1, 0, 2)).reshape(CMP_BLOCK, KV_ROW)
    pm, lm = _hgrn_matrices()
    return {
        'attn_g': attn_norm_g.reshape(1, D_MODEL), 'w_pack': w_pack,
        'q_g': jnp.tile(q_norm_g, 2).reshape(1, LANES), 'k_g': jnp.tile(k_norm_g, (1, 2)),
        'lbl': hgrn_lb_logits.astype(F32), 'mseg': jnp.asarray(np.concatenate([mseg, mseg], axis=0), BF16),
        'cmp_pe': pe, 'cmp_w1': w1_bd, 'cmp_w2': w2_bd,
        'hgrn_g': hgrn_norm_g.reshape(1, B_KEY_DIM), 'hgrn_pm': pm, 'hgrn_lm': lm,
        'w_a': w_branch[:n_q].astype(BF16), 'w_b': w_branch[n_q:].astype(BF16), 'w_out': w_out.astype(BF16),
        'ffn_g': ffn_norm_g.reshape(1, D_MODEL), 'ffn_w_in': ffn_w_in.astype(BF16),
        'conv_w': ffn_conv_w, 'conv_b': ffn_conv_b.reshape(1, D_FF), 'ffn_w_out': ffn_w_out.astype(BF16),
    }


def _rope_tables(pos, reps):
    half = A_HEAD_DIM // 2
    inv = ROPE_THETA ** (-jnp.arange(half, dtype=F32) / half)
    ang = pos.astype(F32)[:, None] * inv[None, :]
    cos, sin = jnp.cos(ang), jnp.sin(ang)
    cos_t = jnp.tile(cos, (reps, LANES // half))
    sin_t = jnp.tile(jnp.concatenate([-sin, sin], axis=-1), (reps, LANES // A_HEAD_DIM))
    return cos_t, sin_t


def kernel(x_prompt, x_sample, cache_cmp_kv, cache_slc_kv, page_table, state_win_kv, state_hgrn, state_ffn_conv, attn_norm_g, w_in, q_norm_g, k_norm_g, cmp_pos_emb, cmp_w1, cmp_w2, hgrn_lb_logits, hgrn_norm_g, w_branch, w_out, ffn_norm_g, ffn_w_in, ffn_conv_w, ffn_conv_b, ffn_w_out):
    assert w_in.shape[0] == 1, "single-layer step"
    b, t, _ = x_prompt.shape
    db, ts, _ = x_sample.shape
    n_pool = cache_cmp_kv.shape[1]
    past = page_table.shape[1] * PAGE_SIZE
    wb = state_win_kv.shape[2]
    assert t % SLC_BLOCK == 0 and t % Q_BLOCK == 0
    consts = _prepare_consts(attn_norm_g[0], w_in[0], q_norm_g[0], k_norm_g[0], cmp_pos_emb[0], cmp_w1[0],
                             cmp_w2[0], hgrn_lb_logits, hgrn_norm_g[0], w_branch[0], w_out[0], ffn_norm_g[0],
                             ffn_w_in[0], ffn_conv_w[0], ffn_conv_b[0], ffn_w_out[0])
    kv_shape = (2, A_KV_GROUPS, A_HEAD_DIM)
    tm_p = min(256, t)
    n_s = db * ts

    fp = _inproj(x_prompt.reshape(b * t, D_MODEL), *_rope_tables(jnp.arange(t, dtype=jnp.int32), 1), consts, tm_p,
                 seq_len=t)
    seq = lambda a: a.reshape(b, t, a.shape[-1])
    kc_p, vc_p = _compress_dense(seq(fp['kvc']), consts)
    oa_p = _nsa_prompt(seq(fp['q']), seq(fp['gate']), kc_p, vc_p, fp['ks_bf'], fp['vs_bf'], fp['kw_bf'],
                       fp['vw_bf'], t)
    ob_p, s_p = _hgrn(seq(fp['qb']), seq(fp['kb']), seq(fp['vb']), seq(fp['lf']), seq(fp['gb']),
                      jnp.zeros((b, B_HEADS, B_KEY_DIM, B_KEY_DIM), F32), consts)
    y_p, conv_p = _merge_ffn(x_prompt.reshape(b * t, D_MODEL), oa_p.reshape(b * t, -1), ob_p.reshape(b * t, -1),
                             fp['mg'], jnp.zeros((b, FFN_CONV - 1, D_FF), F32), t, consts, min(512, t))

    fs = _inproj(x_sample.reshape(n_s, D_MODEL), *_rope_tables(past + jnp.arange(ts, dtype=jnp.int32), db),
                 consts, n_s)
    sseq = lambda a: a.reshape(db, ts, a.shape[-1])
    feat = lambda c: jnp.transpose(c, (0, 2, 3, 4, 1)).reshape(c.shape[0], KV_ROW, c.shape[1])
    new_tile = lambda rows: jnp.pad(jnp.transpose(sseq(rows), (0, 2, 1)), ((0, 0), (0, 0), (0, LANES - ts)))
    kc_s, vc_s = _compress_paged(page_table, feat(cache_cmp_kv[0]), sseq(fs['kvc']), consts)
    oa_s = _nsa_sample(page_table, sseq(fs['q']), sseq(fs['gate']), kc_s, vc_s, feat(cache_slc_kv[0]),
                       new_tile(fs['kvs']), feat(state_win_kv[0]), new_tile(fs['kvw']))
    win_cat = jnp.concatenate([state_win_kv[0].reshape(db, wb, KV_ROW), sseq(fs['kvw'])], axis=1)
    ob_s, s_s = _hgrn(sseq(fs['qb']), sseq(fs['kb']), sseq(fs['vb']), sseq(fs['lf']), sseq(fs['gb']),
                      state_hgrn[0].astype(F32), consts)
    y_s, conv_s = _merge_ffn(x_sample.reshape(n_s, D_MODEL), oa_s.reshape(n_s, -1), ob_s.reshape(n_s, -1),
                             fs['mg'], state_ffn_conv[0], ts, consts, n_s)

    wkeep = min(WINDOW, t)
    unfeat = lambda a: jnp.transpose(a.reshape(b, *kv_shape, a.shape[-1]), (0, 4, 1, 2, 3))[None]
    return (y_p.reshape(b, t, D_MODEL), y_s.reshape(db, ts, D_MODEL),
            unfeat(fp['kvc_t']), fs['kvc'].reshape(1, db, ts, *kv_shape),
            unfeat(fp['kvs_t']), fs['kvs'].reshape(1, db, ts, *kv_shape),
            unfeat(fp['kvw_t'][:, :, t - wkeep:]),
            win_cat[:, ts:].reshape(1, db, wb, *kv_shape),
            s_p[None], s_s[None], conv_p[None], conv_s[None])
```

```python
import functools

import numpy as np
import jax
import jax.numpy as jnp
from jax import lax
from jax.experimental import pallas as pl
from jax.experimental.pallas import tpu as pltpu

F32 = jnp.float32
BF16 = jnp.bfloat16

D_MODEL = 1024
PAGE_SIZE = 128
A_HEADS = 8
A_KV_GROUPS = 2
A_HPG = A_HEADS // A_KV_GROUPS
A_HEAD_DIM = 64
CMP_BLOCK = 32
CMP_STRIDE = 16
SLC_BLOCK = 64
N_SELECT = 16
WINDOW = 512
Q_BLOCK = 128
ROPE_THETA = 10000.0
FORCE_SCORE = 1e4
NEG_INF = -1e30
B_HEADS = 4
B_KEY_DIM = 128
D_FF = 2816
FFN_CONV = 3
EPS = 1e-6

LANES = 128
SUBLANES = 8
KV_ROW = 2 * A_KV_GROUPS * A_HEAD_DIM
S_TILE = 256
W_TILE = 128
SEL_TK = 512
MASK_BIG = 1e30
CHUNK_PITCH = 24
LAND_GROUP = 8
HGRN_CHUNK = 128
HGRN_LEVELS = 7
HGRN_STEP_CHUNKS = 2
VMEM_LIMIT = 56 * 1024 * 1024

OFF_Q, OFF_KVC, OFF_KVS, OFF_KVW = 0, 512, 768, 1024
OFF_QB, OFF_FB, OFF_IB, OFF_GB, OFF_MG, OFF_GATE = 1280, 1792, 2304, 2816, 3328, 5376
W_PACK = 5504


def _dot(a, b):
    return jnp.dot(a, b, preferred_element_type=F32)


def _dot_nt(a, b):
    return lax.dot_general(a, b, (((1,), (1,)), ((), ())), preferred_element_type=F32)


def _dot_tn(a, b):
    return lax.dot_general(a, b, (((0,), (0,)), ((), ())), preferred_element_type=F32)


def _split_bf16(x):
    hi = x.astype(BF16)
    lo = (x - hi.astype(F32)).astype(BF16)
    return hi, lo


def _round_up(n, m):
    return -(-n // m) * m


def _const_spec(shape):
    nd = len(shape)
    return pl.BlockSpec(shape, lambda *_: (0,) * nd, pipeline_mode=pl.Buffered(1))


def _params(semantics):
    return pltpu.CompilerParams(dimension_semantics=semantics, vmem_limit_bytes=VMEM_LIMIT)


INPROJ_COMMON = ('q', 'gate', 'qb', 'kb', 'vb', 'lf', 'gb', 'mg')
INPROJ_PROMPT = INPROJ_COMMON + ('kvc', 'kvc_t', 'kvs_t', 'kvw_t', 'ks_bf', 'vs_bf', 'kw_bf', 'vw_bf')
INPROJ_SAMPLE = INPROJ_COMMON + ('kvc', 'kvs', 'kvw')


def _inproj_kernel(x_ref, g_ref, w_ref, cos_ref, sin_ref, qg_ref, kg_ref, lbl_ref, mseg_ref, *out_refs, names):
    o = dict(zip(names, out_refs))
    q_ref, gate_ref, mg_ref = o['q'], o['gate'], o['mg']
    qb_ref, kb_ref, vb_ref, lf_ref, gb_ref = o['qb'], o['kb'], o['vb'], o['lf'], o['gb']
    x = x_ref[...]
    ms = jnp.mean(x * x, axis=-1, keepdims=True)
    h = (x * lax.rsqrt(ms + EPS) * g_ref[...]).astype(BF16)
    cos = cos_ref[...]
    sin = sin_ref[...]
    tm = x.shape[0]
    lane = lax.broadcasted_iota(jnp.int32, (tm, LANES), 1)
    first_half = (lane & (A_HEAD_DIM // 2)) == 0
    low64 = lane < A_HEAD_DIM
    mseg = mseg_ref[...]

    def proj(lo, width):
        return _dot(h, w_ref[:, lo:lo + width])

    def head_norm_rope(chunk, gain):
        s_hi, s_lo = _split_bf16(chunk * chunk)
        mean = _dot(jnp.concatenate([s_hi, s_lo], axis=-1), mseg)
        y = chunk * lax.rsqrt(mean + EPS) * gain
        rot = jnp.where(first_half, pltpu.roll(y, LANES - A_HEAD_DIM // 2, 1),
                        pltpu.roll(y, A_HEAD_DIM // 2, 1))
        return y * cos + rot * sin

    zq = proj(OFF_Q, A_HEADS * A_HEAD_DIM)
    qg = qg_ref[...]
    for k in range(A_HEADS // 2):
        c = head_norm_rope(zq[:, k * LANES:(k + 1) * LANES], qg) * (A_HEAD_DIM ** -0.5)
        r = pltpu.roll(c, A_HEAD_DIM, 1)
        if k < A_HPG // 2:
            h0, h1 = jnp.where(low64, c, 0.0), jnp.where(low64, r, 0.0)
        else:
            h0, h1 = jnp.where(low64, 0.0, r), jnp.where(low64, 0.0, c)
        q_ref[:, (2 * k) * LANES:(2 * k + 1) * LANES] = h0.astype(BF16)
        q_ref[:, (2 * k + 1) * LANES:(2 * k + 2) * LANES] = h1.astype(BF16)

    for i, (off, name) in enumerate(((OFF_KVC, 'kvc'), (OFF_KVS, 'kvs'), (OFF_KVW, 'kvw'))):
        z = proj(off, KV_ROW)
        rows = jnp.concatenate([head_norm_rope(z[:, :LANES], kg_ref[i:i + 1, :]), z[:, LANES:]], axis=-1)
        if name in o:
            o[name][...] = rows
        if name + '_t' in o:
            rows_t = rows.T
            o[name + '_t'][0] = rows_t
            for half, tag in ((rows_t[:LANES], 'k'), (rows_t[LANES:], 'v')):
                key = tag + name[-1] + '_bf'
                if key in o:
                    n_tiles, _, width = o[key].shape[1:]
                    for j in range(n_tiles):
                        o[key][0, j] = half[:, j * width:(j + 1) * width].astype(BF16)

    gate_ref[...] = jax.nn.sigmoid(proj(OFF_GATE, LANES))

    lbl = lbl_ref[...]
    e = jnp.exp(lbl - jnp.max(lbl, axis=0, keepdims=True))
    lb = e[0:1, :] / jnp.sum(e, axis=0, keepdims=True)
    fz = proj(OFF_FB, B_HEADS * B_KEY_DIM)
    lf_ref[...] = jnp.log(lb + (1.0 - lb) * jax.nn.sigmoid(fz))
    kb_ref[...] = (1.0 - lb) * jax.nn.sigmoid(-fz)
    zqb = proj(OFF_QB, B_HEADS * B_KEY_DIM)
    qb_ref[...] = zqb * jax.nn.sigmoid(zqb)
    vb_ref[...] = proj(OFF_IB, B_HEADS * B_KEY_DIM)
    zg = proj(OFF_GB, B_HEADS * B_KEY_DIM)
    gb_ref[...] = zg * jax.nn.sigmoid(zg)
    mg_ref[...] = jax.nn.sigmoid(proj(OFF_MG, 2 * D_MODEL))


def _inproj(x2d, cos_tab, sin_tab, consts, tm, seq_len=None):
    n = x2d.shape[0]
    tab_blocks = cos_tab.shape[0] // tm
    row = lambda w: pl.BlockSpec((tm, w), lambda i: (i, 0))
    tab = pl.BlockSpec((tm, LANES), lambda i: (i % tab_blocks, 0))
    token_major = {'q': (A_HEADS * LANES, BF16), 'gate': (LANES, F32), 'qb': (512, F32), 'kb': (512, F32),
                   'vb': (512, F32), 'lf': (512, F32), 'gb': (512, F32), 'mg': (2 * D_MODEL, F32),
                   'kvc': (KV_ROW, F32), 'kvs': (KV_ROW, F32), 'kvw': (KV_ROW, F32)}
    names = INPROJ_SAMPLE if seq_len is None else INPROJ_PROMPT
    specs, shapes = [], []
    for name in names:
        if name in token_major:
            w, dt = token_major[name]
            specs.append(row(w))
            shapes.append(jax.ShapeDtypeStruct((n, w), dt))
        elif name.endswith('_t'):
            per_seq = seq_len // tm
            specs.append(pl.BlockSpec((1, KV_ROW, tm), lambda i: (i // per_seq, 0, i % per_seq)))
            shapes.append(jax.ShapeDtypeStruct((n // seq_len, KV_ROW, seq_len), F32))
        else:
            width = S_TILE if name[1] == 's' else W_TILE
            per_seq = seq_len // tm
            specs.append(pl.BlockSpec((1, tm // width, LANES, width), lambda i: (i // per_seq, i % per_seq, 0, 0)))
            shapes.append(jax.ShapeDtypeStruct((n // seq_len, seq_len // width, LANES, width), BF16))
    outs = pl.pallas_call(
        functools.partial(_inproj_kernel, names=names),
        grid=(n // tm,),
        in_specs=[row(D_MODEL), _const_spec((1, D_MODEL)), _const_spec((D_MODEL, W_PACK)), tab, tab,
                  _const_spec((1, LANES)), _const_spec((3, LANES)),
                  _const_spec(consts['lbl'].shape), _const_spec((2 * LANES, LANES))],
        out_specs=specs,
        out_shape=shapes,
        compiler_params=_params(("parallel",)),
        name="inproj",
    )(x2d, consts['attn_g'], consts['w_pack'], cos_tab, sin_tab, consts['q_g'], consts['k_g'],
      consts['lbl'], consts['mseg'])
    return dict(zip(names, outs))


def _chunk_row(chunk):
    return chunk * CHUNK_PITCH


def _compress_rows(tok_k, tok_v, pe_ref, w1_ref, w2_ref, kc_ref, vc_ref, m_rows, n_cmp):
    acc = jnp.zeros((m_rows, KV_ROW), F32)
    for p in range(CMP_BLOCK):
        start = _chunk_row(p // CMP_STRIDE) + p % CMP_STRIDE
        xp = jnp.concatenate([tok_k[pl.ds(start, m_rows, stride=CHUNK_PITCH), :],
                              tok_v[pl.ds(start, m_rows, stride=CHUNK_PITCH), :]], axis=-1) + pe_ref[p:p + 1, :]
        acc = acc + _dot(xp.astype(BF16), w1_ref[p])
    hid = acc * jax.nn.sigmoid(acc)
    out = _dot(hid.astype(BF16), w2_ref[...])
    row = lax.broadcasted_iota(jnp.int32, out.shape, 0)
    out = jnp.where(row < n_cmp, out, 0.0)
    ncp = kc_ref.shape[1]
    kc_ref[0, 0:m_rows, :] = out[:, :LANES].astype(BF16)
    vc_ref[0, 0:m_rows, :] = out[:, LANES:].astype(BF16)
    if ncp > m_rows:
        kc_ref[0, m_rows:ncp, :] = jnp.zeros((ncp - m_rows, LANES), BF16)
        vc_ref[0, m_rows:ncp, :] = jnp.zeros((ncp - m_rows, LANES), BF16)


def _zero_chunks(tok_k, tok_v, first_chunk):
    r0 = _chunk_row(first_chunk)
    tok_k[r0:, :] = jnp.zeros((tok_k.shape[0] - r0, LANES), F32)
    tok_v[r0:, :] = jnp.zeros((tok_v.shape[0] - r0, LANES), F32)


def _compress_dense_kernel(rows_ref, pe_ref, w1_ref, w2_ref, kc_ref, vc_ref, tok_k, tok_v, *, m_rows, n_cmp):
    n_chunks = rows_ref.shape[1] // CMP_STRIDE

    def body(c, carry):
        src = pl.multiple_of(c * CMP_STRIDE, CMP_STRIDE)
        dst = pl.multiple_of(_chunk_row(c), SUBLANES)
        tok_k[pl.ds(dst, CMP_STRIDE), :] = rows_ref[0, pl.ds(src, CMP_STRIDE), 0:LANES]
        tok_v[pl.ds(dst, CMP_STRIDE), :] = rows_ref[0, pl.ds(src, CMP_STRIDE), LANES:KV_ROW]
        return carry

    lax.fori_loop(0, n_chunks, body, 0)
    _zero_chunks(tok_k, tok_v, n_chunks)
    _compress_rows(tok_k, tok_v, pe_ref, w1_ref, w2_ref, kc_ref, vc_ref, m_rows, n_cmp)


def _page_copy(pool_ref, page, stage, slot, sems):
    return pltpu.make_async_copy(pool_ref.at[page], stage.at[slot], sems.at[slot])


def _compress_paged_kernel(pt_ref, pool_ref, new_ref, pe_ref, w1_ref, w2_ref, kc_ref, vc_ref,
                           stage, tok_k, tok_v, sems, *, n_pages, m_rows, n_cmp):
    b = pl.program_id(0)

    def start_row(row):
        def start(p, carry):
            _page_copy(pool_ref, pt_ref[row, p], stage, p, sems).start()
            return carry
        lax.fori_loop(0, n_pages, start, 0)

    @pl.when(b == 0)
    def _():
        start_row(0)

    chunks_per_page = PAGE_SIZE // CMP_STRIDE
    first_new = n_pages * chunks_per_page
    _zero_chunks(tok_k, tok_v, first_new)
    ts = new_ref.shape[1]
    assert ts <= CMP_STRIDE
    tok_k[_chunk_row(first_new):_chunk_row(first_new) + ts, :] = new_ref[0, :, 0:LANES]
    tok_v[_chunk_row(first_new):_chunk_row(first_new) + ts, :] = new_ref[0, :, LANES:KV_ROW]

    group = int(np.gcd(n_pages, LAND_GROUP))

    def land(i, carry):
        for j in range(group):
            _page_copy(pool_ref, 0, stage, i * group + j, sems).wait()
        for j in range(group):
            p = i * group + j
            page_t = stage[p].T
            base = pl.multiple_of(_chunk_row(p * chunks_per_page), SUBLANES)
            for c in range(chunks_per_page):
                rows = slice(c * CMP_STRIDE, (c + 1) * CMP_STRIDE)
                tok_k[pl.ds(base + _chunk_row(c), CMP_STRIDE), :] = page_t[rows, 0:LANES]
                tok_v[pl.ds(base + _chunk_row(c), CMP_STRIDE), :] = page_t[rows, LANES:KV_ROW]
        return carry

    lax.fori_loop(0, n_pages // group, land, 0)

    @pl.when(b + 1 < pl.num_programs(0))
    def _():
        start_row(b + 1)

    _compress_rows(tok_k, tok_v, pe_ref, w1_ref, w2_ref, kc_ref, vc_ref, m_rows, n_cmp)


def _compress_geometry(t_real):
    t_pad = _round_up(t_real, SLC_BLOCK)
    n_cmp = t_pad // CMP_STRIDE - CMP_BLOCK // CMP_STRIDE + 1
    m_rows = _round_up(n_cmp, SUBLANES)
    ncp = _round_up(n_cmp, LANES)
    n_chunks = max(m_rows + CMP_BLOCK // CMP_STRIDE - 1, -(-t_real // CMP_STRIDE))
    return n_cmp, m_rows, ncp, _chunk_row(n_chunks)


def _compress_dense(rows, consts):
    b, t, _ = rows.shape
    n_cmp, m_rows, ncp, tok_rows = _compress_geometry(t)
    return pl.pallas_call(
        functools.partial(_compress_dense_kernel, m_rows=m_rows, n_cmp=n_cmp),
        grid=(b,),
        in_specs=[pl.BlockSpec((1, t, KV_ROW), lambda i: (i, 0, 0)),
                  _const_spec((CMP_BLOCK, KV_ROW)), _const_spec((CMP_BLOCK, KV_ROW, KV_ROW)),
                  _const_spec((KV_ROW, KV_ROW))],
        out_specs=[pl.BlockSpec((1, ncp, LANES), lambda i: (i, 0, 0))] * 2,
        out_shape=[jax.ShapeDtypeStruct((b, ncp, LANES), BF16)] * 2,
        scratch_shapes=[pltpu.VMEM((tok_rows, LANES), F32)] * 2,
        compiler_params=_params(("parallel",)),
        name="compress_dense",
    )(rows, consts['cmp_pe'], consts['cmp_w1'], consts['cmp_w2'])


def _compress_paged(page_table, pool, new_rows, consts):
    db, n_pages = page_table.shape
    ts = new_rows.shape[1]
    n_cmp, m_rows, ncp, tok_rows = _compress_geometry(n_pages * PAGE_SIZE + ts)
    cm = lambda nd: (lambda i, pt: (0,) * nd)
    grid_spec = pltpu.PrefetchScalarGridSpec(
        num_scalar_prefetch=1,
        grid=(db,),
        in_specs=[pl.BlockSpec(memory_space=pl.ANY),
                  pl.BlockSpec((1, ts, KV_ROW), lambda i, pt: (i, 0, 0)),
                  pl.BlockSpec((CMP_BLOCK, KV_ROW), cm(2), pipeline_mode=pl.Buffered(1)),
                  pl.BlockSpec((CMP_BLOCK, KV_ROW, KV_ROW), cm(3), pipeline_mode=pl.Buffered(1)),
                  pl.BlockSpec((KV_ROW, KV_ROW), cm(2), pipeline_mode=pl.Buffered(1))],
        out_specs=[pl.BlockSpec((1, ncp, LANES), lambda i, pt: (i, 0, 0))] * 2,
        scratch_shapes=[pltpu.VMEM((n_pages, KV_ROW, PAGE_SIZE), F32), pltpu.VMEM((tok_rows, LANES), F32),
                        pltpu.VMEM((tok_rows, LANES), F32), pltpu.SemaphoreType.DMA((n_pages,))],
    )
    return pl.pallas_call(
        functools.partial(_compress_paged_kernel, n_pages=n_pages, m_rows=m_rows, n_cmp=n_cmp),
        grid_spec=grid_spec,
        out_shape=[jax.ShapeDtypeStruct((db, ncp, LANES), BF16)] * 2,
        compiler_params=_params(("arbitrary",)),
        name="compress_paged",
    )(page_table, pool, new_rows, consts['cmp_pe'], consts['cmp_w1'], consts['cmp_w2'])


def _softmax_rows(s, mask):
    s = jnp.where(mask, s, NEG_INF)
    p = jnp.where(mask, jnp.exp(s - jnp.max(s, axis=-1, keepdims=True)), 0.0)
    return p / jnp.maximum(jnp.sum(p, axis=-1, keepdims=True), 1e-30)


def _online_step(carry, s, v):
    m, l, acc = carry
    m_new = jnp.maximum(m, jnp.max(s, axis=-1, keepdims=True))
    alpha = jnp.exp(m - m_new)
    p = jnp.exp(s - m_new)
    l = alpha * l + jnp.sum(p, axis=-1, keepdims=True)
    acc = alpha * acc + _dot(p.astype(BF16), v)
    return m_new, l, acc


def _online_init(rows):
    return (jnp.full((rows, 1), NEG_INF, F32), jnp.zeros((rows, 1), F32), jnp.zeros((rows, LANES), F32))


def _online_finish(carry):
    _, l, acc = carry
    return acc / jnp.maximum(l, 1e-30)


def _block_scores(p_sum, cover_ref, qpos, n_blocks):
    hi, lo = _split_bf16(p_sum)
    imp = _dot(jnp.concatenate([hi, lo], axis=-1), cover_ref[...])
    blk = lax.broadcasted_iota(jnp.int32, imp.shape, 1)
    cur = qpos // SLC_BLOCK
    forced = (blk == 0) | (blk == cur) | (blk == cur - 1)
    score = jnp.where(forced, FORCE_SCORE, jnp.where(blk <= cur, imp, -1.0))
    return jnp.where(blk < n_blocks, score, -2.0)


def _topk_select(score, n_blocks, n_sel):
    lane = lax.broadcasted_iota(jnp.int32, score.shape, 1)
    cnt = jnp.zeros(score.shape, F32)
    for s in range(n_blocks):
        col = score[:, s:s + 1]
        beats = (col > score) | ((col == score) & (lane > s))
        cnt = cnt + jnp.where(beats, 1.0, 0.0)
    return jnp.where((cnt < n_sel) & (lane < n_blocks), 1.0, 0.0)


def _assemble_heads(heads, low64):
    chunks = []
    for k in range(A_HEADS // 2):
        a, b = heads[2 * k], heads[2 * k + 1]
        if k < A_HPG // 2:
            chunks.append(jnp.where(low64, a, pltpu.roll(b, A_HEAD_DIM, 1)))
        else:
            chunks.append(jnp.where(low64, pltpu.roll(a, A_HEAD_DIM, 1), b))
    return chunks


def _topk_rows(score_t, n_blocks, n_sel):
    n_tiles = score_t.shape[0] // SUBLANES
    tiles = [score_t[t * SUBLANES:(t + 1) * SUBLANES] for t in range(n_tiles)]
    sub = lax.broadcasted_iota(jnp.int32, tiles[0].shape, 0)
    cnt = [jnp.zeros(tiles[0].shape, F32) for _ in range(n_tiles)]
    for s in range(n_blocks):
        row = score_t[s:s + 1, :]
        for t in range(n_tiles):
            if t * SUBLANES > s:
                beats = row >= tiles[t]
            elif (t + 1) * SUBLANES - 1 < s:
                beats = row > tiles[t]
            else:
                later = sub + t * SUBLANES > s
                beats = (row > tiles[t]) | ((row == tiles[t]) & later)
            cnt[t] = cnt[t] + jnp.where(beats, 1.0, 0.0)
    blk = lax.broadcasted_iota(jnp.int32, score_t.shape, 0)
    return jnp.where((jnp.concatenate(cnt, axis=0) < n_sel) & (blk < n_blocks), 1.0, 0.0)


def _online_step_t(carry, s, v_t):
    m, l, acc = carry
    m_new = jnp.maximum(m, jnp.max(s, axis=-1, keepdims=True))
    alpha = jnp.exp(m - m_new)
    p = jnp.exp(s - m_new)
    l = alpha * l + jnp.sum(p, axis=-1, keepdims=True)
    acc = alpha * acc + _dot_nt(p.astype(BF16), v_t)
    return m_new, l, acc


def _nsa_prompt_kernel(q_ref, gate_ref, kc_ref, vc_ref, ks_ref, vs_ref, kw_ref, vw_ref, cover_ref, exp_ref,
                       o_ref, *, n_blocks, n_sel):
    i = pl.program_id(1)
    s0 = i * Q_BLOCK
    q = q_ref[0]
    gates = gate_ref[0]
    rows = A_HEADS * Q_BLOCK
    q_all = jnp.concatenate([q[:, h * LANES:(h + 1) * LANES] for h in range(A_HEADS)], axis=0)
    qpos1 = s0 + lax.broadcasted_iota(jnp.int32, (Q_BLOCK, 1), 0)
    qpos_all = jnp.concatenate([qpos1] * A_HEADS, axis=0)

    ncp = kc_ref.shape[1]
    c_end = lax.broadcasted_iota(jnp.int32, (1, ncp), 1) * CMP_STRIDE + (CMP_BLOCK - 1)
    p_c = _softmax_rows(_dot_nt(q_all, kc_ref[0]), c_end <= qpos_all)
    o_c = _dot(p_c.astype(BF16), vc_ref[0])

    cur = (s0 + lax.broadcasted_iota(jnp.int32, (1, Q_BLOCK), 1)) // SLC_BLOCK
    blk = lax.broadcasted_iota(jnp.int32, (LANES, Q_BLOCK), 0)
    forced = (blk == 0) | (blk == cur) | (blk == cur - 1)
    scores = []
    for g in range(A_KV_GROUPS):
        base = g * A_HPG * Q_BLOCK
        p_sum = p_c[base:base + Q_BLOCK]
        for hh in range(1, A_HPG):
            p_sum = p_sum + p_c[base + hh * Q_BLOCK:base + (hh + 1) * Q_BLOCK]
        hi, lo = _split_bf16(p_sum)
        imp_t = _dot(jnp.concatenate([hi, lo], axis=-1), cover_ref[...]).T
        score = jnp.where(forced, FORCE_SCORE, jnp.where(blk <= cur, imp_t, -1.0))
        scores.append(jnp.where(blk < n_blocks, score, -2.0))
    nb8 = _round_up(n_blocks, SUBLANES)
    sel_t = _topk_rows(jnp.concatenate(scores, axis=1)[0:nb8], n_blocks, n_sel)
    if nb8 < LANES:
        sel_t = jnp.concatenate([sel_t, jnp.zeros((LANES - nb8, sel_t.shape[1]), F32)], axis=0)

    aug = []
    for g in range(A_KV_GROUPS):
        sel = sel_t[:, g * Q_BLOCK:(g + 1) * Q_BLOCK].T
        aug += [((sel - 1.0) * MASK_BIG).astype(BF16)] * A_HPG
    q_aug = jnp.concatenate([q_all, jnp.concatenate(aug, axis=0)], axis=1)
    per_tile = SEL_TK // S_TILE
    cat = lambda kt, ref_tile: jnp.concatenate([ref_tile(kt * per_tile + j) for j in range(per_tile)], axis=1)

    def sel_step(kt, state, bias):
        k_aug = jnp.concatenate([cat(kt, lambda n: ks_ref[0, n]), cat(kt, lambda n: exp_ref[n])], axis=0)
        s = _dot(q_aug, k_aug)
        return _online_step_t(state, s if bias is None else s + bias, cat(kt, lambda n: vs_ref[0, n]))

    n_full = s0 // SEL_TK
    state = lax.fori_loop(0, n_full, lambda kt, st: sel_step(kt, st, None), _online_init(rows))
    kpos = n_full * SEL_TK + lax.broadcasted_iota(jnp.int32, (1, SEL_TK), 1)
    causal = jnp.where(kpos <= qpos1, 0.0, NEG_INF)
    o_s = _online_finish(sel_step(n_full, state, jnp.concatenate([causal] * A_HEADS, axis=0)))

    st = jnp.maximum(i - WINDOW // Q_BLOCK, 0) * (Q_BLOCK // W_TILE)
    n_wt = (WINDOW + Q_BLOCK) // W_TILE
    kw_t = jnp.concatenate([kw_ref[0, st + j] for j in range(n_wt)], axis=1)
    vw_t = jnp.concatenate([vw_ref[0, st + j] for j in range(n_wt)], axis=1)
    d = qpos1 - (st * W_TILE + lax.broadcasted_iota(jnp.int32, (1, n_wt * W_TILE), 1))
    band = jnp.where((d >= 0) & (d < WINDOW), 0.0, NEG_INF)
    s_w = _dot(q_all, kw_t) + jnp.concatenate([band] * A_HEADS, axis=0)
    p_w = jnp.exp(s_w - jnp.max(s_w, axis=-1, keepdims=True))
    o_w = _dot_nt(p_w.astype(BF16), vw_t) / jnp.maximum(jnp.sum(p_w, axis=-1, keepdims=True), 1e-30)

    heads = []
    for h in range(A_HEADS):
        r = slice(h * Q_BLOCK, (h + 1) * Q_BLOCK)
        heads.append(gates[:, 3 * h:3 * h + 1] * o_c[r] + gates[:, 3 * h + 1:3 * h + 2] * o_s[r]
                     + gates[:, 3 * h + 2:3 * h + 3] * o_w[r])
    low64 = lax.broadcasted_iota(jnp.int32, (Q_BLOCK, LANES), 1) < A_HEAD_DIM
    for k, chunk in enumerate(_assemble_heads(heads, low64)):
        o_ref[0, :, k * LANES:(k + 1) * LANES] = chunk.astype(BF16)


def _cover_matrix(n_cmp, ncp, n_blocks, nsp):
    c = np.arange(ncp)[:, None]
    s = np.arange(nsp)[None, :]
    cover = ((c * CMP_STRIDE < s * SLC_BLOCK + SLC_BLOCK) & (c * CMP_STRIDE + CMP_BLOCK > s * SLC_BLOCK)
             & (c < n_cmp) & (s < n_blocks))
    return jnp.asarray(np.concatenate([cover, cover], axis=0), BF16)


def _expand_matrix(n_keys, block0=0):
    e = (np.arange(n_keys)[None, :] // SLC_BLOCK) == (block0 + np.arange(LANES)[:, None])
    return e


def _expand_tiles(n_keys, tk):
    e = _expand_matrix(n_keys).reshape(LANES, n_keys // tk, tk)
    return jnp.asarray(np.transpose(e, (1, 0, 2)), BF16)


def _nsa_prompt(q, gates, kc, vc, ks_bf, vs_bf, kw_bf, vw_bf, t):
    b = q.shape[0]
    ncp = kc.shape[1]
    n_cmp = t // CMP_STRIDE - CMP_BLOCK // CMP_STRIDE + 1
    n_blocks = t // SLC_BLOCK
    assert n_blocks <= LANES and t % SEL_TK == 0 and t >= WINDOW + Q_BLOCK
    cover = _cover_matrix(n_cmp, ncp, n_blocks, LANES)
    expand = _expand_tiles(t, S_TILE)
    per_b = lambda rows, w: pl.BlockSpec((1, rows, w), lambda bi, i: (bi, 0, 0))
    tiles = lambda a: pl.BlockSpec((1,) + a.shape[1:], lambda bi, i: (bi, 0, 0, 0))
    return pl.pallas_call(
        functools.partial(_nsa_prompt_kernel, n_blocks=n_blocks, n_sel=min(N_SELECT, n_blocks)),
        grid=(b, t // Q_BLOCK),
        in_specs=[pl.BlockSpec((1, Q_BLOCK, A_HEADS * LANES), lambda bi, i: (bi, i, 0)),
                  pl.BlockSpec((1, Q_BLOCK, LANES), lambda bi, i: (bi, i, 0)),
                  per_b(ncp, LANES), per_b(ncp, LANES),
                  tiles(ks_bf), tiles(vs_bf), tiles(kw_bf), tiles(vw_bf),
                  _const_spec(cover.shape), _const_spec(expand.shape)],
        out_specs=pl.BlockSpec((1, Q_BLOCK, A_HEADS * A_HEAD_DIM), lambda bi, i: (bi, i, 0)),
        out_shape=jax.ShapeDtypeStruct((b, t, A_HEADS * A_HEAD_DIM), BF16),
        compiler_params=_params(("parallel", "parallel")),
        name="nsa_prompt",
    )(q, gates, kc, vc, ks_bf, vs_bf, kw_bf, vw_bf, cover, expand)


def _key_page_copy(pool_ref, page, bufs, half, slot, sems):
    dst = bufs.at[half, :, pl.ds(pl.multiple_of(slot * PAGE_SIZE, PAGE_SIZE), PAGE_SIZE)]
    return pltpu.make_async_copy(pool_ref.at[page], dst, sems.at[half])


def _nsa_sample_kernel(pt_ref, q_ref, gate_ref, kc_ref, vc_ref, pool_ref, new_ref, win_ref, wnew_ref,
                       cover_ref, exp_ref, o_ref, bufs, sems, *, n_pages, n_blocks, n_sel, key_chunk):
    b = pl.program_id(0)
    half = b % 2

    def start_row(row, into):
        def start(p, carry):
            _key_page_copy(pool_ref, pt_ref[row, p], bufs, into, p, sems).start()
            return carry
        lax.fori_loop(0, n_pages, start, 0)

    @pl.when(b == 0)
    def _():
        start_row(0, 0)

    @pl.when(b + 1 < pl.num_programs(0))
    def _():
        start_row(b + 1, 1 - half)

    buf = bufs.at[half]
    past = n_pages * PAGE_SIZE
    ts = q_ref.shape[1]
    buf[:, past:past + LANES] = new_ref[0]

    q = q_ref[0].astype(F32)
    q_all = jnp.concatenate([q[:, h * LANES:(h + 1) * LANES] for h in range(A_HEADS)], axis=0).astype(BF16)
    rows = A_HEADS * ts
    qpos1 = past + lax.broadcasted_iota(jnp.int32, (ts, 1), 0)
    qpos_all = jnp.concatenate([qpos1] * A_HEADS, axis=0)

    ncp = kc_ref.shape[1]
    c_end = lax.broadcasted_iota(jnp.int32, (1, ncp), 1) * CMP_STRIDE + (CMP_BLOCK - 1)
    p_c = _softmax_rows(_dot_nt(q_all, kc_ref[0]), c_end <= qpos_all)
    o_c = _dot(p_c.astype(BF16), vc_ref[0])

    p_groups = []
    for g in range(A_KV_GROUPS):
        base = g * A_HPG * ts
        p_sum = p_c[base:base + ts]
        for hh in range(1, A_HPG):
            p_sum = p_sum + p_c[base + hh * ts:base + (hh + 1) * ts]
        p_groups.append(p_sum)
    qpos_g = jnp.concatenate([qpos1] * A_KV_GROUPS, axis=0)
    score = _block_scores(jnp.concatenate(p_groups, axis=0), cover_ref, qpos_g, n_blocks)
    sel = _topk_select(score, n_blocks, n_sel).astype(BF16)

    def wait(p, carry):
        _key_page_copy(pool_ref, 0, bufs, half, p, sems).wait()
        return carry

    lax.fori_loop(0, n_pages, wait, 0)

    n_keys = buf.shape[1]
    carry = _online_init(rows)
    for ck in range(-(-n_keys // key_chunk)):
        k0 = ck * key_chunk
        kn = min(key_chunk, n_keys - k0)
        s = _dot(q_all, buf[0:LANES, k0:k0 + kn].astype(BF16))
        blk0 = ck * (key_chunk // SLC_BLOCK)
        em = _dot(sel[:, blk0:blk0 + LANES], exp_ref[:, 0:kn])
        kpos = k0 + lax.broadcasted_iota(jnp.int32, (1, kn), 1)
        bias = jnp.where((em > 0.5) & (kpos <= qpos_g), 0.0, NEG_INF)
        biases = []
        for g in range(A_KV_GROUPS):
            biases += [bias[g * ts:(g + 1) * ts]] * A_HPG
        carry = _online_step_t(carry, s + jnp.concatenate(biases, axis=0),
                               buf[LANES:KV_ROW, k0:k0 + kn].astype(BF16))
    o_s = _online_finish(carry)

    wb = win_ref.shape[2]
    kv_w = jnp.concatenate([win_ref[0], wnew_ref[0]], axis=1).astype(BF16)
    idx = lax.broadcasted_iota(jnp.int32, (1, wb + LANES), 1)
    w_pos = jnp.where(idx < wb, past - wb + idx, past + idx - wb)
    d = qpos_all - w_pos
    m_w = (d >= 0) & (d < WINDOW) & (w_pos >= 0)
    p_w = _softmax_rows(_dot(q_all, kv_w[0:LANES]), m_w)
    o_w = _dot_nt(p_w.astype(BF16), kv_w[LANES:KV_ROW])

    gates = gate_ref[0]
    heads = []
    for h in range(A_HEADS):
        r = slice(h * ts, (h + 1) * ts)
        heads.append(gates[:, 3 * h:3 * h + 1] * o_c[r] + gates[:, 3 * h + 1:3 * h + 2] * o_s[r]
                     + gates[:, 3 * h + 2:3 * h + 3] * o_w[r])
    low64 = lax.broadcasted_iota(jnp.int32, (ts, LANES), 1) < A_HEAD_DIM
    for k, chunk in enumerate(_assemble_heads(heads, low64)):
        o_ref[0, :, k * LANES:(k + 1) * LANES] = chunk.astype(BF16)


def _nsa_sample(page_table, q, gates, kc, vc, pool, new_tile, win_t, wnew_tile):
    db, n_pages = page_table.shape
    wb = win_t.shape[2]
    assert wb % LANES == 0
    ts = q.shape[1]
    past = n_pages * PAGE_SIZE
    ncp = kc.shape[1]
    t_pad = _round_up(past + ts, SLC_BLOCK)
    n_cmp = t_pad // CMP_STRIDE - CMP_BLOCK // CMP_STRIDE + 1
    n_blocks = t_pad // SLC_BLOCK
    key_chunk = LANES * SLC_BLOCK
    n_keys = past + LANES
    nsp = LANES * (-(-n_keys // key_chunk))
    assert nsp >= n_blocks and ts % SUBLANES == 0
    cover = _cover_matrix(n_cmp, ncp, n_blocks, nsp)
    expand = jnp.asarray(_expand_matrix(min(key_chunk, n_keys)), BF16)
    cm = lambda nd: (lambda i, pt: (0,) * nd)
    per_b = lambda r, w: pl.BlockSpec((1, r, w), lambda i, pt: (i, 0, 0))
    grid_spec = pltpu.PrefetchScalarGridSpec(
        num_scalar_prefetch=1,
        grid=(db,),
        in_specs=[per_b(ts, A_HEADS * LANES), per_b(ts, LANES), per_b(ncp, LANES), per_b(ncp, LANES),
                  pl.BlockSpec(memory_space=pl.ANY), per_b(KV_ROW, LANES), per_b(KV_ROW, wb), per_b(KV_ROW, LANES),
                  pl.BlockSpec(cover.shape, cm(2), pipeline_mode=pl.Buffered(1)),
                  pl.BlockSpec(expand.shape, cm(2), pipeline_mode=pl.Buffered(1))],
        out_specs=per_b(ts, A_HEADS * A_HEAD_DIM),
        scratch_shapes=[pltpu.VMEM((2, KV_ROW, n_keys), F32), pltpu.SemaphoreType.DMA((2,))],
    )
    return pl.pallas_call(
        functools.partial(_nsa_sample_kernel, n_pages=n_pages, n_blocks=n_blocks,
                          n_sel=min(N_SELECT, n_blocks), key_chunk=key_chunk),
        grid_spec=grid_spec,
        out_shape=jax.ShapeDtypeStruct((db, ts, A_HEADS * A_HEAD_DIM), BF16),
        compiler_params=_params(("arbitrary",)),
        name="nsa_sample",
    )(page_table, q, gates, kc, vc, pool, new_tile, win_t, wnew_tile, cover, expand)


def _hgrn_matrices():
    c = HGRN_CHUNK
    t = np.arange(c)[:, None]
    u = np.arange(c)[None, :]
    mats = [u <= t]
    masks = [t == u]
    for lvl in range(HGRN_LEVELS):
        m = 1 << lvl
        mid = (t // (2 * m)) * (2 * m) + m - 1
        mats.append((u > mid) & (u <= t))
        mats.append((u > t) & (u <= mid))
        masks.append((t // (2 * m) == u // (2 * m)) & (t % (2 * m) >= m) & (u % (2 * m) < m))
    mats.append(u > t)
    pm = np.concatenate(mats, axis=0)
    return (jnp.asarray(np.concatenate([pm, pm], axis=1), BF16),
            jnp.asarray(np.stack(masks).astype(np.float32)))


def _hgrn_kernel(qb_ref, kb_ref, vb_ref, lf_ref, gb_ref, s0_ref, gn_ref, pm_ref, lm_ref,
                 ob_ref, sout_ref, st_scr, pad_scr):
    j = pl.program_id(1)
    c = HGRN_CHUNK
    t_blk = qb_ref.shape[1]

    @pl.when(j == 0)
    def _():
        for h in range(B_HEADS):
            st_scr[h] = s0_ref[0, h].T

    def load(ref, slot, rows):
        if t_blk >= c:
            return ref[0, rows, :]
        pad_scr[slot] = jnp.zeros((c, B_HEADS * B_KEY_DIM), F32)
        pad_scr[slot, 0:t_blk, :] = ref[0]
        return pad_scr[slot]

    gn = gn_ref[...]
    states = [st_scr[h] for h in range(B_HEADS)]
    for sub in range(max(t_blk // c, 1)):
        rows = slice(sub * c, (sub + 1) * c)
        out_rows = rows if t_blk >= c else slice(0, t_blk)
        qb, kb, vb, lf = load(qb_ref, 0, rows), load(kb_ref, 1, rows), load(vb_ref, 2, rows), load(lf_ref, 3, rows)
        lf_hi, lf_lo = _split_bf16(lf)
        ex = _dot(pm_ref[...], jnp.concatenate([lf_hi, lf_lo], axis=0))
        for h in range(B_HEADS):
            sl = slice(h * B_KEY_DIM, (h + 1) * B_KEY_DIM)
            q, k = qb[:, sl], kb[:, sl]
            v = vb[:, sl].astype(BF16)
            b_cum = ex[0:c, sl]
            a = lm_ref[0] * _dot_nt(q.astype(BF16), k.astype(BF16))
            for lvl in range(HGRN_LEVELS):
                eq = ex[(2 * lvl + 1) * c:(2 * lvl + 2) * c, sl]
                ek = ex[(2 * lvl + 2) * c:(2 * lvl + 3) * c, sl]
                a = a + lm_ref[lvl + 1] * _dot_nt((q * jnp.exp(eq)).astype(BF16), (k * jnp.exp(ek)).astype(BF16))
            st = states[h]
            o = _dot_nt((q * jnp.exp(b_cum)).astype(BF16), st.astype(BF16)) + _dot(a.astype(BF16), v)
            e_end = ex[(2 * HGRN_LEVELS + 1) * c:(2 * HGRN_LEVELS + 2) * c, sl]
            states[h] = st * jnp.exp(b_cum[c - 1:c, :]) + _dot_tn(v, (k * jnp.exp(e_end)).astype(BF16))
            y = o * lax.rsqrt(jnp.mean(o * o, axis=-1, keepdims=True) + EPS) * gn
            ob_ref[0, out_rows, sl] = y[0:min(t_blk, c)] * gb_ref[0, out_rows, sl]
    for h in range(B_HEADS):
        st_scr[h] = states[h]

    @pl.when(j == pl.num_programs(1) - 1)
    def _():
        for h in range(B_HEADS):
            sout_ref[0, h] = st_scr[h].T


def _hgrn(qb, kb, vb, lf, gb, s0, consts):
    b, t, w = qb.shape
    t_blk = min(t, HGRN_STEP_CHUNKS * HGRN_CHUNK)
    assert t % t_blk == 0 and (t_blk % HGRN_CHUNK == 0 or t == t_blk < HGRN_CHUNK)
    tok = pl.BlockSpec((1, t_blk, w), lambda bi, j: (bi, j, 0))
    st = pl.BlockSpec((1, B_HEADS, B_KEY_DIM, B_KEY_DIM), lambda bi, j: (bi, 0, 0, 0))
    return pl.pallas_call(
        _hgrn_kernel,
        grid=(b, t // t_blk),
        in_specs=[tok, tok, tok, tok, tok, st, _const_spec((1, B_KEY_DIM)),
                  _const_spec(consts['hgrn_pm'].shape), _const_spec(consts['hgrn_lm'].shape)],
        out_specs=[tok, st],
        out_shape=[jax.ShapeDtypeStruct((b, t, w), F32),
                   jax.ShapeDtypeStruct((b, B_HEADS, B_KEY_DIM, B_KEY_DIM), F32)],
        scratch_shapes=[pltpu.VMEM((B_HEADS, B_KEY_DIM, B_KEY_DIM), F32),
                        pltpu.VMEM((4, HGRN_CHUNK, w), F32)],
        compiler_params=_params(("parallel", "arbitrary")),
        name="hgrn",
    )(qb, kb, vb, lf, gb, s0, consts['hgrn_g'], consts['hgrn_pm'], consts['hgrn_lm'])


def _merge_ffn_kernel(x_ref, oa_ref, ob_ref, mg_ref, p1_ref, p2_ref, wa_ref, wb_ref, wo_ref, fg_ref, win_ref,
                      cw_ref, cb_ref, wout_ref, y_ref, a_ref, carry_scr, *, seq_len):
    tm = x_ref.shape[0]
    mg = mg_ref[...]
    m = (mg[:, :D_MODEL] * _dot(oa_ref[...], wa_ref[...])
         + mg[:, D_MODEL:] * _dot(ob_ref[...].astype(BF16), wb_ref[...]))
    x2 = x_ref[...] + _dot(m.astype(BF16), wo_ref[...])
    h = (x2 * lax.rsqrt(jnp.mean(x2 * x2, axis=-1, keepdims=True) + EPS) * fg_ref[...]).astype(BF16)
    a = _dot(h, win_ref[:, :D_FF])
    gate = _dot(h, win_ref[:, D_FF:])
    row = lax.broadcasted_iota(jnp.int32, (tm, 1), 0)
    if seq_len >= tm:
        j = pl.program_id(0) % (seq_len // tm)

        @pl.when(j == 0)
        def _():
            carry_scr[0:2, :] = p1_ref[0]

        prev = carry_scr[...]
        a1 = jnp.where(row == 0, prev[1:2], pltpu.roll(a, 1, 0))
        a2 = jnp.where(row == 0, prev[0:1], jnp.where(row == 1, prev[1:2], pltpu.roll(a, 2, 0)))
        carry_scr[0:2, :] = a[tm - 2:tm]
        a_ref[0] = a[tm - 2:tm]
    else:
        t = row % seq_len
        a1 = jnp.where(t == 0, p1_ref[...], pltpu.roll(a, 1, 0))
        a2 = jnp.where(t < 2, p2_ref[...], pltpu.roll(a, 2, 0))
        a_ref[...] = a
    a_conv = cb_ref[...] + a2 * cw_ref[0:1, :] + a1 * cw_ref[1:2, :] + a * cw_ref[2:3, :]
    act = a_conv * jax.nn.sigmoid(a_conv) * gate
    y_ref[...] = x2 + _dot(act.astype(BF16), wout_ref[...])


def _merge_ffn(x2d, oa, ob, mg, conv_state, seq_len, consts, tm):
    n = x2d.shape[0]
    b = n // seq_len
    row = lambda w: pl.BlockSpec((tm, w), lambda i: (i, 0))
    if seq_len >= tm:
        assert seq_len % tm == 0
        per_seq = seq_len // tm
        p1, p2 = conv_state, conv_state
        p_spec = pl.BlockSpec((1, FFN_CONV - 1, D_FF), lambda i: (i // per_seq, 0, 0))
        a_shape = jax.ShapeDtypeStruct((b, FFN_CONV - 1, D_FF), F32)
        a_spec = pl.BlockSpec((1, FFN_CONV - 1, D_FF), lambda i: (i // per_seq, 0, 0))
    else:
        assert tm % seq_len == 0 and seq_len >= FFN_CONV - 1
        zeros = jnp.zeros((b, seq_len - 1, D_FF), F32)
        p1 = jnp.concatenate([conv_state[:, 1:2], zeros], axis=1).reshape(n, D_FF)
        p2 = jnp.concatenate([conv_state, zeros[:, 1:]], axis=1).reshape(n, D_FF)
        p_spec = row(D_FF)
        a_shape = jax.ShapeDtypeStruct((n, D_FF), F32)
        a_spec = row(D_FF)
    y, a_out = pl.pallas_call(
        functools.partial(_merge_ffn_kernel, seq_len=seq_len),
        grid=(n // tm,),
        in_specs=[row(D_MODEL), row(A_HEADS * A_HEAD_DIM), row(B_HEADS * B_KEY_DIM), row(2 * D_MODEL),
                  p_spec, p_spec,
                  _const_spec((A_HEADS * A_HEAD_DIM, D_MODEL)), _const_spec((B_HEADS * B_KEY_DIM, D_MODEL)),
                  _const_spec((D_MODEL, D_MODEL)), _const_spec((1, D_MODEL)),
                  _const_spec((D_MODEL, 2 * D_FF)), _const_spec((FFN_CONV, D_FF)), _const_spec((1, D_FF)),
                  _const_spec((D_FF, D_MODEL))],
        out_specs=[row(D_MODEL), a_spec],
        out_shape=[jax.ShapeDtypeStruct((n, D_MODEL), F32), a_shape],
        scratch_shapes=[pltpu.VMEM((SUBLANES, D_FF), F32)],
        compiler_params=_params(("arbitrary",)),
        name="merge_ffn",
    )(x2d, oa, ob, mg, p1, p2, consts['w_a'], consts['w_b'], consts['w_out'], consts['ffn_g'],
      consts['ffn_w_in'], consts['conv_w'], consts['conv_b'], consts['ffn_w_out'])
    if seq_len >= tm:
        return y, a_out
    return y, a_out.reshape(b, seq_len, D_FF)[:, seq_len - (FFN_CONV - 1):]


def _prepare_consts(attn_norm_g, w_in, q_norm_g, k_norm_g, cmp_pos_emb, cmp_w1, cmp_w2, hgrn_lb_logits,
                    hgrn_norm_g, w_branch, w_out, ffn_norm_g, ffn_w_in, ffn_conv_w, ffn_conv_b, ffn_w_out):
    n_q = A_HEADS * A_HEAD_DIM
    gate_lo = n_q + 3 * KV_ROW
    gate_hi = gate_lo + 3 * A_HEADS
    w_pack = jnp.concatenate([w_in[:, :gate_lo], w_in[:, gate_hi:], w_in[:, gate_lo:gate_hi],
                              jnp.zeros((D_MODEL, LANES - 3 * A_HEADS), w_in.dtype)], axis=1).astype(BF16)
    seg = np.arange(LANES) // A_HEAD_DIM
    mseg = (seg[:, None] == seg[None, :]).astype(np.float32) / A_HEAD_DIM
    eye = jnp.eye(2 * A_KV_GROUPS, dtype=F32)
    jsel = np.repeat(np.arange(2), A_KV_GROUPS)
    w1 = cmp_w1.reshape(2, CMP_BLOCK, A_HEAD_DIM, A_HEAD_DIM)[jsel]
    w1_bd = jnp.einsum('ab,apde->padbe', eye, w1).reshape(CMP_BLOCK, KV_ROW, KV_ROW).astype(BF16)
    w2_bd = jnp.einsum('ab,ade->adbe', eye, cmp_w2[jsel]).reshape(KV_ROW, KV_ROW).astype(BF16)
    pe = jnp.transpose(cmp_pos_emb[jsel], (1, 0, 2)).reshape(CMP_BLOCK, KV_ROW)
    pm, lm = _hgrn_matrices()
    return {
        'attn_g': attn_norm_g.reshape(1, D_MODEL), 'w_pack': w_pack,
        'q_g': jnp.tile(q_norm_g, 2).reshape(1, LANES), 'k_g': jnp.tile(k_norm_g, (1, 2)),
        'lbl': hgrn_lb_logits.astype(F32), 'mseg': jnp.asarray(np.concatenate([mseg, mseg], axis=0), BF16),
        'cmp_pe': pe, 'cmp_w1': w1_bd, 'cmp_w2': w2_bd,
        'hgrn_g': hgrn_norm_g.reshape(1, B_KEY_DIM), 'hgrn_pm': pm, 'hgrn_lm': lm,
        'w_a': w_branch[:n_q].astype(BF16), 'w_b': w_branch[n_q:].astype(BF16), 'w_out': w_out.astype(BF16),
        'ffn_g': ffn_norm_g.reshape(1, D_MODEL), 'ffn_w_in': ffn_w_in.astype(BF16),
        'conv_w': ffn_conv_w, 'conv_b': ffn_conv_b.reshape(1, D_FF), 'ffn_w_out': ffn_w_out.astype(BF16),
    }


def _rope_tables(pos, reps):
    half = A_HEAD_DIM // 2
    inv = ROPE_THETA ** (-jnp.arange(half, dtype=F32) / half)
    ang = pos.astype(F32)[:, None] * inv[None, :]
    cos, sin = jnp.cos(ang), jnp.sin(ang)
    cos_t = jnp.tile(cos, (reps, LANES // half))
    sin_t = jnp.tile(jnp.concatenate([-sin, sin], axis=-1), (reps, LANES // A_HEAD_DIM))
    return cos_t, sin_t


def kernel(x_prompt, x_sample, cache_cmp_kv, cache_slc_kv, page_table, state_win_kv, state_hgrn, state_ffn_conv, attn_norm_g, w_in, q_norm_g, k_norm_g, cmp_pos_emb, cmp_w1, cmp_w2, hgrn_lb_logits, hgrn_norm_g, w_branch, w_out, ffn_norm_g, ffn_w_in, ffn_conv_w, ffn_conv_b, ffn_w_out):
    assert w_in.shape[0] == 1, "single-layer step"
    b, t, _ = x_prompt.shape
    db, ts, _ = x_sample.shape
    n_pool = cache_cmp_kv.shape[1]
    past = page_table.shape[1] * PAGE_SIZE
    wb = state_win_kv.shape[2]
    assert t % SLC_BLOCK == 0 and t % Q_BLOCK == 0
    consts = _prepare_consts(attn_norm_g[0], w_in[0], q_norm_g[0], k_norm_g[0], cmp_pos_emb[0], cmp_w1[0],
                             cmp_w2[0], hgrn_lb_logits, hgrn_norm_g[0], w_branch[0], w_out[0], ffn_norm_g[0],
                             ffn_w_in[0], ffn_conv_w[0], ffn_conv_b[0], ffn_w_out[0])
    kv_shape = (2, A_KV_GROUPS, A_HEAD_DIM)
    tm_p = min(256, t)
    n_s = db * ts

    fp = _inproj(x_prompt.reshape(b * t, D_MODEL), *_rope_tables(jnp.arange(t, dtype=jnp.int32), 1), consts, tm_p,
                 seq_len=t)
    seq = lambda a: a.reshape(b, t, a.shape[-1])
    kc_p, vc_p = _compress_dense(seq(fp['kvc']), consts)
    oa_p = _nsa_prompt(seq(fp['q']), seq(fp['gate']), kc_p, vc_p, fp['ks_bf'], fp['vs_bf'], fp['kw_bf'],
                       fp['vw_bf'], t)
    ob_p, s_p = _hgrn(seq(fp['qb']), seq(fp['kb']), seq(fp['vb']), seq(fp['lf']), seq(fp['gb']),
                      jnp.zeros((b, B_HEADS, B_KEY_DIM, B_KEY_DIM), F32), consts)
    y_p, conv_p = _merge_ffn(x_prompt.reshape(b * t, D_MODEL), oa_p.reshape(b * t, -1), ob_p.reshape(b * t, -1),
                             fp['mg'], jnp.zeros((b, FFN_CONV - 1, D_FF), F32), t, consts, min(512, t))

    fs = _inproj(x_sample.reshape(n_s, D_MODEL), *_rope_tables(past + jnp.arange(ts, dtype=jnp.int32), db),
                 consts, n_s)
    sseq = lambda a: a.reshape(db, ts, a.shape[-1])
    feat = lambda c: jnp.transpose(c, (0, 2, 3, 4, 1)).reshape(c.shape[0], KV_ROW, c.shape[1])
    new_tile = lambda rows: jnp.pad(jnp.transpose(sseq(rows), (0, 2, 1)), ((0, 0), (0, 0), (0, LANES - ts)))
    kc_s, vc_s = _compress_paged(page_table, feat(cache_cmp_kv[0]), sseq(fs['kvc']), consts)
    oa_s = _nsa_sample(page_table, sseq(fs['q']), sseq(fs['gate']), kc_s, vc_s, feat(cache_slc_kv[0]),
                       new_tile(fs['kvs']), feat(state_win_kv[0]), new_tile(fs['kvw']))
    win_cat = jnp.concatenate([state_win_kv[0].reshape(db, wb, KV_ROW), sseq(fs['kvw'])], axis=1)
    ob_s, s_s = _hgrn(sseq(fs['qb']), sseq(fs['kb']), sseq(fs['vb']), sseq(fs['lf']), sseq(fs['gb']),
                      state_hgrn[0].astype(F32), consts)
    y_s, conv_s = _merge_ffn(x_sample.reshape(n_s, D_MODEL), oa_s.reshape(n_s, -1), ob_s.reshape(n_s, -1),
                             fs['mg'], state_ffn_conv[0], ts, consts, n_s)

    wkeep = min(WINDOW, t)
    unfeat = lambda a: jnp.transpose(a.reshape(b, *kv_shape, a.shape[-1]), (0, 4, 1, 2, 3))[None]
    return (y_p.reshape(b, t, D_MODEL), y_s.reshape(db, ts, D_MODEL),
            unfeat(fp['kvc_t']), fs['kvc'].reshape(1, db, ts, *kv_shape),
            unfeat(fp['kvs_t']), fs['kvs'].reshape(1, db, ts, *kv_shape),
            unfeat(fp['kvw_t'][:, :, t - wkeep:]),
            win_cat[:, ts:].reshape(1, db, wb, *kv_shape),
            s_p[None], s_s[None], conv_p[None], conv_s[None])
```

```python
import functools

import numpy as np
import jax
import jax.numpy as jnp
from jax import lax
from jax.experimental import pallas as pl
from jax.experimental.pallas import tpu as pltpu

F32 = jnp.float32
BF16 = jnp.bfloat16

D_MODEL = 1024
PAGE_SIZE = 128
A_HEADS = 8
A_KV_GROUPS = 2
A_HPG = A_HEADS // A_KV_GROUPS
A_HEAD_DIM = 64
CMP_BLOCK = 32
CMP_STRIDE = 16
SLC_BLOCK = 64
N_SELECT = 16
WINDOW = 512
Q_BLOCK = 128
ROPE_THETA = 10000.0
FORCE_SCORE = 1e4
NEG_INF = -1e30
B_HEADS = 4
B_KEY_DIM = 128
D_FF = 2816
FFN_CONV = 3
EPS = 1e-6

LANES = 128
SUBLANES = 8
KV_ROW = 2 * A_KV_GROUPS * A_HEAD_DIM
S_TILE = 256
W_TILE = 128
SEL_TK = 512
MASK_BIG = 1e30
SCORE_SAFE = 40.0
CHUNK_PITCH = 24
LAND_GROUP = 8
CMP_GROUP = 4
CMP_W1_SHAPE = (CMP_BLOCK // CMP_GROUP, CMP_GROUP * KV_ROW, KV_ROW)
HGRN_CHUNK = 128
HGRN_LEVELS = 7
HGRN_STEP_CHUNKS = 4
VMEM_LIMIT = 56 * 1024 * 1024

OFF_Q, OFF_KVC, OFF_KVS, OFF_KVW = 0, 512, 768, 1024
OFF_QB, OFF_FB, OFF_IB, OFF_GB, OFF_MG, OFF_GATE = 1280, 1792, 2304, 2816, 3328, 5376
W_PACK = 5504


def _dot(a, b):
    return jnp.dot(a, b, preferred_element_type=F32)


def _dot_nt(a, b):
    return lax.dot_general(a, b, (((1,), (1,)), ((), ())), preferred_element_type=F32)


def _dot_tn(a, b):
    return lax.dot_general(a, b, (((0,), (0,)), ((), ())), preferred_element_type=F32)


def _split_bf16(x):
    hi = x.astype(BF16)
    lo = (x - hi.astype(F32)).astype(BF16)
    return hi, lo


def _round_up(n, m):
    return -(-n // m) * m


def _const_spec(shape):
    nd = len(shape)
    return pl.BlockSpec(shape, lambda *_: (0,) * nd, pipeline_mode=pl.Buffered(1))


def _params(semantics):
    return pltpu.CompilerParams(dimension_semantics=semantics, vmem_limit_bytes=VMEM_LIMIT)


INPROJ_COMMON = ('q', 'gate', 'qb', 'kb', 'vb', 'lf', 'gb', 'mg')
INPROJ_PROMPT = INPROJ_COMMON + ('kvc', 'kvc_t', 'kvs_t', 'kvw_t', 'ks_bf', 'vs_bf', 'kw_bf', 'vw_bf')
INPROJ_SAMPLE = INPROJ_COMMON + ('kvc', 'kvs', 'kvw')


def _inproj_kernel(x_ref, g_ref, wa_ref, wb_ref, wg_ref, cos_ref, sin_ref, qg_ref, kg_ref, lbl_ref, mseg_ref,
                   *out_refs, names):
    o = dict(zip(names, out_refs))
    q_ref, gate_ref, mg_ref = o['q'], o['gate'], o['mg']
    qb_ref, kb_ref, vb_ref, lf_ref, gb_ref = o['qb'], o['kb'], o['vb'], o['lf'], o['gb']
    x = x_ref[...]
    ms = jnp.mean(x * x, axis=-1, keepdims=True)
    h = (x * lax.rsqrt(ms + EPS) * g_ref[...]).astype(BF16)
    cos = cos_ref[...]
    sin = sin_ref[...]
    tm = x.shape[0]
    lane = lax.broadcasted_iota(jnp.int32, (tm, LANES), 1)
    first_half = (lane & (A_HEAD_DIM // 2)) == 0
    low64 = lane < A_HEAD_DIM
    mseg = mseg_ref[...]

    def proj(lo, width):
        for start, ref in ((OFF_GATE, wg_ref), (OFF_QB, wb_ref), (OFF_Q, wa_ref)):
            if lo >= start:
                return _dot(h, ref[:, lo - start:lo - start + width])

    def head_norm_rope(chunk, gain):
        s_hi, s_lo = _split_bf16(chunk * chunk)
        mean = _dot(jnp.concatenate([s_hi, s_lo], axis=-1), mseg)
        y = chunk * lax.rsqrt(mean + EPS) * gain
        rot = jnp.where(first_half, pltpu.roll(y, LANES - A_HEAD_DIM // 2, 1),
                        pltpu.roll(y, A_HEAD_DIM // 2, 1))
        return y * cos + rot * sin

    zq = proj(OFF_Q, A_HEADS * A_HEAD_DIM)
    qg = qg_ref[...]
    for k in range(A_HEADS // 2):
        c = head_norm_rope(zq[:, k * LANES:(k + 1) * LANES], qg) * (A_HEAD_DIM ** -0.5)
        r = pltpu.roll(c, A_HEAD_DIM, 1)
        if k < A_HPG // 2:
            h0, h1 = jnp.where(low64, c, 0.0), jnp.where(low64, r, 0.0)
        else:
            h0, h1 = jnp.where(low64, 0.0, r), jnp.where(low64, 0.0, c)
        q_ref[:, (2 * k) * LANES:(2 * k + 1) * LANES] = h0.astype(BF16)
        q_ref[:, (2 * k + 1) * LANES:(2 * k + 2) * LANES] = h1.astype(BF16)

    for i, (off, name) in enumerate(((OFF_KVC, 'kvc'), (OFF_KVS, 'kvs'), (OFF_KVW, 'kvw'))):
        z = proj(off, KV_ROW)
        rows = jnp.concatenate([head_norm_rope(z[:, :LANES], kg_ref[i:i + 1, :]), z[:, LANES:]], axis=-1)
        if name in o:
            o[name][...] = rows
        if name + '_t' in o:
            rows_t = rows.T
            o[name + '_t'][0] = rows_t
            for half, tag in ((rows_t[:LANES], 'k'), (rows_t[LANES:], 'v')):
                key = tag + name[-1] + '_bf'
                if key in o:
                    n_tiles, _, width = o[key].shape[1:]
                    for j in range(n_tiles):
                        o[key][0, j] = half[:, j * width:(j + 1) * width].astype(BF16)

    gate_ref[...] = jax.nn.sigmoid(proj(OFF_GATE, LANES))

    lbl = lbl_ref[...]
    e = jnp.exp(lbl - jnp.max(lbl, axis=0, keepdims=True))
    lb = e[0:1, :] / jnp.sum(e, axis=0, keepdims=True)
    fz = proj(OFF_FB, B_HEADS * B_KEY_DIM)
    lf_ref[...] = jnp.log(lb + (1.0 - lb) * jax.nn.sigmoid(fz))
    kb_ref[...] = (1.0 - lb) * jax.nn.sigmoid(-fz)
    zqb = proj(OFF_QB, B_HEADS * B_KEY_DIM)
    qb_ref[...] = zqb * jax.nn.sigmoid(zqb)
    vb_ref[...] = proj(OFF_IB, B_HEADS * B_KEY_DIM)
    zg = proj(OFF_GB, B_HEADS * B_KEY_DIM)
    gb_ref[...] = zg * jax.nn.sigmoid(zg)
    mg_ref[...] = jax.nn.sigmoid(proj(OFF_MG, 2 * D_MODEL))


def _inproj(x2d, cos_tab, sin_tab, consts, tm, seq_len=None):
    n = x2d.shape[0]
    tab_blocks = cos_tab.shape[0] // tm
    row = lambda w: pl.BlockSpec((tm, w), lambda i: (i, 0))
    tab = pl.BlockSpec((tm, LANES), lambda i: (i % tab_blocks, 0))
    token_major = {'q': (A_HEADS * LANES, BF16), 'gate': (LANES, F32), 'qb': (512, F32), 'kb': (512, F32),
                   'vb': (512, F32), 'lf': (512, F32), 'gb': (512, F32), 'mg': (2 * D_MODEL, F32),
                   'kvc': (KV_ROW, F32), 'kvs': (KV_ROW, F32), 'kvw': (KV_ROW, F32)}
    names = INPROJ_SAMPLE if seq_len is None else INPROJ_PROMPT
    specs, shapes = [], []
    for name in names:
        if name in token_major:
            w, dt = token_major[name]
            specs.append(row(w))
            shapes.append(jax.ShapeDtypeStruct((n, w), dt))
        elif name.endswith('_t'):
            per_seq = seq_len // tm
            specs.append(pl.BlockSpec((1, KV_ROW, tm), lambda i: (i // per_seq, 0, i % per_seq)))
            shapes.append(jax.ShapeDtypeStruct((n // seq_len, KV_ROW, seq_len), F32))
        else:
            width = S_TILE if name[1] == 's' else W_TILE
            per_seq = seq_len // tm
            specs.append(pl.BlockSpec((1, tm // width, LANES, width), lambda i: (i // per_seq, i % per_seq, 0, 0)))
            shapes.append(jax.ShapeDtypeStruct((n // seq_len, seq_len // width, LANES, width), BF16))
    outs = pl.pallas_call(
        functools.partial(_inproj_kernel, names=names),
        grid=(n // tm,),
        in_specs=[row(D_MODEL), _const_spec((1, D_MODEL)), _const_spec((D_MODEL, OFF_QB)),
                  _const_spec((D_MODEL, OFF_GATE - OFF_QB)), _const_spec((D_MODEL, LANES)), tab, tab,
                  _const_spec((1, LANES)), _const_spec((3, LANES)),
                  _const_spec(consts['lbl'].shape), _const_spec((2 * LANES, LANES))],
        out_specs=specs,
        out_shape=shapes,
        compiler_params=_params(("parallel",)),
        name="inproj",
    )(x2d, consts['attn_g'], consts['w_attn'], consts['w_rest'], consts['w_gate'], cos_tab, sin_tab,
      consts['q_g'], consts['k_g'],
      consts['lbl'], consts['mseg'])
    return dict(zip(names, outs))


def _chunk_row(chunk):
    return chunk * CHUNK_PITCH


def _compress_rows(tok_k, tok_v, pe_ref, w1_ref, w2_ref, kc_ref, vc_ref, m_rows, n_cmp):
    acc = jnp.zeros((m_rows, KV_ROW), F32)
    for p0 in range(0, CMP_BLOCK, CMP_GROUP):
        xs = []
        for p in range(p0, p0 + CMP_GROUP):
            start = _chunk_row(p // CMP_STRIDE) + p % CMP_STRIDE
            xp = jnp.concatenate([tok_k[pl.ds(start, m_rows, stride=CHUNK_PITCH), :],
                                  tok_v[pl.ds(start, m_rows, stride=CHUNK_PITCH), :]], axis=-1) + pe_ref[p:p + 1, :]
            xs.append(xp.astype(BF16))
        acc = acc + _dot(jnp.concatenate(xs, axis=-1), w1_ref[p0 // CMP_GROUP])
    hid = acc * jax.nn.sigmoid(acc)
    out = _dot(hid.astype(BF16), w2_ref[...])
    row = lax.broadcasted_iota(jnp.int32, out.shape, 0)
    out = jnp.where(row < n_cmp, out, 0.0)
    ncp = kc_ref.shape[1]
    kc_ref[0, 0:m_rows, :] = out[:, :LANES].astype(BF16)
    vc_ref[0, 0:m_rows, :] = out[:, LANES:].astype(BF16)
    if ncp > m_rows:
        kc_ref[0, m_rows:ncp, :] = jnp.zeros((ncp - m_rows, LANES), BF16)
        vc_ref[0, m_rows:ncp, :] = jnp.zeros((ncp - m_rows, LANES), BF16)


def _zero_chunks(tok_k, tok_v, first_chunk):
    r0 = _chunk_row(first_chunk)
    tok_k[r0:, :] = jnp.zeros((tok_k.shape[0] - r0, LANES), F32)
    tok_v[r0:, :] = jnp.zeros((tok_v.shape[0] - r0, LANES), F32)


def _compress_dense_kernel(rows_ref, pe_ref, w1_ref, w2_ref, kc_ref, vc_ref, tok_k, tok_v, *, m_rows, n_cmp):
    n_chunks = rows_ref.shape[1] // CMP_STRIDE

    def body(c, carry):
        src = pl.multiple_of(c * CMP_STRIDE, CMP_STRIDE)
        dst = pl.multiple_of(_chunk_row(c), SUBLANES)
        tok_k[pl.ds(dst, CMP_STRIDE), :] = rows_ref[0, pl.ds(src, CMP_STRIDE), 0:LANES]
        tok_v[pl.ds(dst, CMP_STRIDE), :] = rows_ref[0, pl.ds(src, CMP_STRIDE), LANES:KV_ROW]
        return carry

    lax.fori_loop(0, n_chunks, body, 0)
    _zero_chunks(tok_k, tok_v, n_chunks)
    _compress_rows(tok_k, tok_v, pe_ref, w1_ref, w2_ref, kc_ref, vc_ref, m_rows, n_cmp)


def _page_copy(pool_ref, page, stage, slot, sems):
    return pltpu.make_async_copy(pool_ref.at[page], stage.at[slot], sems.at[slot])


def _compress_paged_kernel(pt_ref, pool_ref, new_ref, pe_ref, w1_ref, w2_ref, kc_ref, vc_ref,
                           stage, tok_k, tok_v, sems, *, n_pages, m_rows, n_cmp):
    b = pl.program_id(0)

    def start_row(row):
        def start(p, carry):
            _page_copy(pool_ref, pt_ref[row, p], stage, p, sems).start()
            return carry
        lax.fori_loop(0, n_pages, start, 0)

    @pl.when(b == 0)
    def _():
        start_row(0)

    chunks_per_page = PAGE_SIZE // CMP_STRIDE
    first_new = n_pages * chunks_per_page
    _zero_chunks(tok_k, tok_v, first_new)
    ts = new_ref.shape[1]
    assert ts <= CMP_STRIDE
    tok_k[_chunk_row(first_new):_chunk_row(first_new) + ts, :] = new_ref[0, :, 0:LANES]
    tok_v[_chunk_row(first_new):_chunk_row(first_new) + ts, :] = new_ref[0, :, LANES:KV_ROW]

    group = int(np.gcd(n_pages, LAND_GROUP))

    def land(i, carry):
        for j in range(group):
            _page_copy(pool_ref, 0, stage, i * group + j, sems).wait()
        for j in range(group):
            p = i * group + j
            page_t = stage[p].T
            base = pl.multiple_of(_chunk_row(p * chunks_per_page), SUBLANES)
            for c in range(chunks_per_page):
                rows = slice(c * CMP_STRIDE, (c + 1) * CMP_STRIDE)
                tok_k[pl.ds(base + _chunk_row(c), CMP_STRIDE), :] = page_t[rows, 0:LANES]
                tok_v[pl.ds(base + _chunk_row(c), CMP_STRIDE), :] = page_t[rows, LANES:KV_ROW]
        return carry

    lax.fori_loop(0, n_pages // group, land, 0)

    @pl.when(b + 1 < pl.num_programs(0))
    def _():
        start_row(b + 1)

    _compress_rows(tok_k, tok_v, pe_ref, w1_ref, w2_ref, kc_ref, vc_ref, m_rows, n_cmp)


def _compress_geometry(t_real):
    t_pad = _round_up(t_real, SLC_BLOCK)
    n_cmp = t_pad // CMP_STRIDE - CMP_BLOCK // CMP_STRIDE + 1
    m_rows = _round_up(n_cmp, SUBLANES)
    ncp = _round_up(n_cmp, LANES)
    n_chunks = max(m_rows + CMP_BLOCK // CMP_STRIDE - 1, -(-t_real // CMP_STRIDE))
    return n_cmp, m_rows, ncp, _chunk_row(n_chunks)


def _compress_dense(rows, consts):
    b, t, _ = rows.shape
    n_cmp, m_rows, ncp, tok_rows = _compress_geometry(t)
    return pl.pallas_call(
        functools.partial(_compress_dense_kernel, m_rows=m_rows, n_cmp=n_cmp),
        grid=(b,),
        in_specs=[pl.BlockSpec((1, t, KV_ROW), lambda i: (i, 0, 0)),
                  _const_spec((CMP_BLOCK, KV_ROW)), _const_spec(CMP_W1_SHAPE),
                  _const_spec((KV_ROW, KV_ROW))],
        out_specs=[pl.BlockSpec((1, ncp, LANES), lambda i: (i, 0, 0))] * 2,
        out_shape=[jax.ShapeDtypeStruct((b, ncp, LANES), BF16)] * 2,
        scratch_shapes=[pltpu.VMEM((tok_rows, LANES), F32)] * 2,
        compiler_params=_params(("parallel",)),
        name="compress_dense",
    )(rows, consts['cmp_pe'], consts['cmp_w1'], consts['cmp_w2'])


def _compress_paged(page_table, pool, new_rows, consts):
    db, n_pages = page_table.shape
    ts = new_rows.shape[1]
    n_cmp, m_rows, ncp, tok_rows = _compress_geometry(n_pages * PAGE_SIZE + ts)
    cm = lambda nd: (lambda i, pt: (0,) * nd)
    grid_spec = pltpu.PrefetchScalarGridSpec(
        num_scalar_prefetch=1,
        grid=(db,),
        in_specs=[pl.BlockSpec(memory_space=pl.ANY),
                  pl.BlockSpec((1, ts, KV_ROW), lambda i, pt: (i, 0, 0)),
                  pl.BlockSpec((CMP_BLOCK, KV_ROW), cm(2), pipeline_mode=pl.Buffered(1)),
                  pl.BlockSpec(CMP_W1_SHAPE, cm(3), pipeline_mode=pl.Buffered(1)),
                  pl.BlockSpec((KV_ROW, KV_ROW), cm(2), pipeline_mode=pl.Buffered(1))],
        out_specs=[pl.BlockSpec((1, ncp, LANES), lambda i, pt: (i, 0, 0))] * 2,
        scratch_shapes=[pltpu.VMEM((n_pages, KV_ROW, PAGE_SIZE), F32), pltpu.VMEM((tok_rows, LANES), F32),
                        pltpu.VMEM((tok_rows, LANES), F32), pltpu.SemaphoreType.DMA((n_pages,))],
    )
    return pl.pallas_call(
        functools.partial(_compress_paged_kernel, n_pages=n_pages, m_rows=m_rows, n_cmp=n_cmp),
        grid_spec=grid_spec,
        out_shape=[jax.ShapeDtypeStruct((db, ncp, LANES), BF16)] * 2,
        compiler_params=_params(("arbitrary",)),
        name="compress_paged",
    )(page_table, pool, new_rows, consts['cmp_pe'], consts['cmp_w1'], consts['cmp_w2'])


def _softmax_rows(s, mask):
    s = jnp.where(mask, s, NEG_INF)
    p = jnp.where(mask, jnp.exp(s - jnp.max(s, axis=-1, keepdims=True)), 0.0)
    return p / jnp.maximum(jnp.sum(p, axis=-1, keepdims=True), 1e-30)


def _online_step(carry, s, v):
    m, l, acc = carry
    m_new = jnp.maximum(m, jnp.max(s, axis=-1, keepdims=True))
    alpha = jnp.exp(m - m_new)
    p = jnp.exp(s - m_new)
    l = alpha * l + jnp.sum(p, axis=-1, keepdims=True)
    acc = alpha * acc + _dot(p.astype(BF16), v)
    return m_new, l, acc


def _online_init(rows):
    return (jnp.full((rows, 1), NEG_INF, F32), jnp.zeros((rows, 1), F32), jnp.zeros((rows, LANES), F32))


def _online_finish(carry):
    _, l, acc = carry
    return acc / jnp.maximum(l, 1e-30)


def _block_scores(p_sum, cover_ref, qpos, n_blocks):
    hi, lo = _split_bf16(p_sum)
    imp = _dot(jnp.concatenate([hi, lo], axis=-1), cover_ref[...])
    blk = lax.broadcasted_iota(jnp.int32, imp.shape, 1)
    cur = qpos // SLC_BLOCK
    forced = (blk == 0) | (blk == cur) | (blk == cur - 1)
    score = jnp.where(forced, FORCE_SCORE, jnp.where(blk <= cur, imp, -1.0))
    return jnp.where(blk < n_blocks, score, -2.0)


def _topk_select(score, n_blocks, n_sel):
    lane = lax.broadcasted_iota(jnp.int32, score.shape, 1)
    cnt = jnp.zeros(score.shape, F32)
    for s in range(n_blocks):
        col = score[:, s:s + 1]
        beats = (col > score) | ((col == score) & (lane > s))
        cnt = cnt + jnp.where(beats, 1.0, 0.0)
    return jnp.where((cnt < n_sel) & (lane < n_blocks), 1.0, 0.0)


def _assemble_heads(heads, low64):
    chunks = []
    for k in range(A_HEADS // 2):
        a, b = heads[2 * k], heads[2 * k + 1]
        if k < A_HPG // 2:
            chunks.append(jnp.where(low64, a, pltpu.roll(b, A_HEAD_DIM, 1)))
        else:
            chunks.append(jnp.where(low64, pltpu.roll(a, A_HEAD_DIM, 1), b))
    return chunks


def _topk_rows(score_t, n_blocks, n_sel):
    n_tiles = score_t.shape[0] // SUBLANES
    tiles = [score_t[t * SUBLANES:(t + 1) * SUBLANES] for t in range(n_tiles)]
    sub = lax.broadcasted_iota(jnp.int32, tiles[0].shape, 0)
    cnt = [jnp.zeros(tiles[0].shape, F32) for _ in range(n_tiles)]
    for s in range(n_blocks):
        row = score_t[s:s + 1, :]
        for t in range(n_tiles):
            if t * SUBLANES > s:
                beats = row >= tiles[t]
            elif (t + 1) * SUBLANES - 1 < s:
                beats = row > tiles[t]
            else:
                later = sub + t * SUBLANES > s
                beats = (row > tiles[t]) | ((row == tiles[t]) & later)
            cnt[t] = cnt[t] + jnp.where(beats, 1.0, 0.0)
    blk = lax.broadcasted_iota(jnp.int32, score_t.shape, 0)
    return jnp.where((jnp.concatenate(cnt, axis=0) < n_sel) & (blk < n_blocks), 1.0, 0.0)


def _online_step_t(carry, s, v_t):
    m, l, acc = carry
    m_new = jnp.maximum(m, jnp.max(s, axis=-1, keepdims=True))
    alpha = jnp.exp(m - m_new)
    p = jnp.exp(s - m_new)
    l = alpha * l + jnp.sum(p, axis=-1, keepdims=True)
    acc = alpha * acc + _dot_nt(p.astype(BF16), v_t)
    return m_new, l, acc


def _nsa_prompt_kernel(bound_ref, q_ref, gate_ref, kc_ref, vc_ref, ks_ref, vs_ref, kw_ref, vw_ref, cover_ref,
                       exp_ref, o_ref, *, n_blocks, n_sel):
    i = pl.program_id(1)
    s0 = i * Q_BLOCK
    q = q_ref[0]
    gates = gate_ref[0]
    rows = A_HEADS * Q_BLOCK
    q_all = jnp.concatenate([q[:, h * LANES:(h + 1) * LANES] for h in range(A_HEADS)], axis=0)
    qpos1 = s0 + lax.broadcasted_iota(jnp.int32, (Q_BLOCK, 1), 0)
    qpos_all = jnp.concatenate([qpos1] * A_HEADS, axis=0)

    ncp = kc_ref.shape[1]
    c_end = lax.broadcasted_iota(jnp.int32, (1, ncp), 1) * CMP_STRIDE + (CMP_BLOCK - 1)
    p_c = _softmax_rows(_dot_nt(q_all, kc_ref[0]), c_end <= qpos_all)
    o_c = _dot(p_c.astype(BF16), vc_ref[0])

    cur = (s0 + lax.broadcasted_iota(jnp.int32, (1, Q_BLOCK), 1)) // SLC_BLOCK
    blk = lax.broadcasted_iota(jnp.int32, (LANES, Q_BLOCK), 0)
    forced = (blk == 0) | (blk == cur) | (blk == cur - 1)
    scores = []
    for g in range(A_KV_GROUPS):
        base = g * A_HPG * Q_BLOCK
        p_sum = p_c[base:base + Q_BLOCK]
        for hh in range(1, A_HPG):
            p_sum = p_sum + p_c[base + hh * Q_BLOCK:base + (hh + 1) * Q_BLOCK]
        hi, lo = _split_bf16(p_sum)
        imp_t = _dot(jnp.concatenate([hi, lo], axis=-1), cover_ref[...]).T
        score = jnp.where(forced, FORCE_SCORE, jnp.where(blk <= cur, imp_t, -1.0))
        scores.append(jnp.where(blk < n_blocks, score, -2.0))
    nb8 = _round_up(n_blocks, SUBLANES)
    sel_t = _topk_rows(jnp.concatenate(scores, axis=1)[0:nb8], n_blocks, n_sel)
    if nb8 < LANES:
        sel_t = jnp.concatenate([sel_t, jnp.zeros((LANES - nb8, sel_t.shape[1]), F32)], axis=0)

    aug = []
    for g in range(A_KV_GROUPS):
        sel = sel_t[:, g * Q_BLOCK:(g + 1) * Q_BLOCK].T
        aug += [((sel - 1.0) * MASK_BIG).astype(BF16)] * A_HPG
    q_aug = jnp.concatenate([q_all, jnp.concatenate(aug, axis=0)], axis=1)
    per_tile = SEL_TK // S_TILE
    cat = lambda kt, ref_tile: jnp.concatenate([ref_tile(kt * per_tile + j) for j in range(per_tile)], axis=1)

    n_full = s0 // SEL_TK
    kpos = n_full * SEL_TK + lax.broadcasted_iota(jnp.int32, (1, SEL_TK), 1)
    causal = jnp.concatenate([jnp.where(kpos <= qpos1, 0.0, NEG_INF)] * A_HEADS, axis=0)

    st = jnp.maximum(i - WINDOW // Q_BLOCK, 0) * (Q_BLOCK // W_TILE)
    n_wt = (WINDOW + Q_BLOCK) // W_TILE
    d = qpos1 - (st * W_TILE + lax.broadcasted_iota(jnp.int32, (1, n_wt * W_TILE), 1))
    band = jnp.concatenate([jnp.where((d >= 0) & (d < WINDOW), 0.0, NEG_INF)] * A_HEADS, axis=0)

    def attend(shifted):
        def step(state, s, v_t):
            if shifted:
                return _online_step_t(state, s, v_t)
            l, acc = state
            p = jnp.exp(s)
            return l + jnp.sum(p, axis=-1, keepdims=True), acc + _dot_nt(p.astype(BF16), v_t)

        def sel_step(kt, state, bias):
            k_aug = jnp.concatenate([cat(kt, lambda n: ks_ref[0, n]), cat(kt, lambda n: exp_ref[n])], axis=0)
            s = _dot(q_aug, k_aug)
            return step(state, s if bias is None else s + bias, cat(kt, lambda n: vs_ref[0, n]))

        init = _online_init(rows)
        state = lax.fori_loop(0, n_full, lambda kt, carry: sel_step(kt, carry, None), init if shifted else init[1:])
        state = sel_step(n_full, state, causal)
        o_sel = state[-1] / jnp.maximum(state[-2], 1e-30)

        kw_t = jnp.concatenate([kw_ref[0, st + j] for j in range(n_wt)], axis=1)
        vw_t = jnp.concatenate([vw_ref[0, st + j] for j in range(n_wt)], axis=1)
        s_w = _dot(q_all, kw_t) + band
        if shifted:
            s_w = s_w - jnp.max(s_w, axis=-1, keepdims=True)
        p_w = jnp.exp(s_w)
        o_win = _dot_nt(p_w.astype(BF16), vw_t) / jnp.maximum(jnp.sum(p_w, axis=-1, keepdims=True), 1e-30)
        return o_sel, o_win

    o_s, o_w = lax.cond(bound_ref[0] <= SCORE_SAFE, lambda: attend(False), lambda: attend(True))

    heads = []
    for h in range(A_HEADS):
        r = slice(h * Q_BLOCK, (h + 1) * Q_BLOCK)
        heads.append(gates[:, 3 * h:3 * h + 1] * o_c[r] + gates[:, 3 * h + 1:3 * h + 2] * o_s[r]
                     + gates[:, 3 * h + 2:3 * h + 3] * o_w[r])
    low64 = lax.broadcasted_iota(jnp.int32, (Q_BLOCK, LANES), 1) < A_HEAD_DIM
    for k, chunk in enumerate(_assemble_heads(heads, low64)):
        o_ref[0, :, k * LANES:(k + 1) * LANES] = chunk.astype(BF16)


def _cover_matrix(n_cmp, ncp, n_blocks, nsp):
    c = np.arange(ncp)[:, None]
    s = np.arange(nsp)[None, :]
    cover = ((c * CMP_STRIDE < s * SLC_BLOCK + SLC_BLOCK) & (c * CMP_STRIDE + CMP_BLOCK > s * SLC_BLOCK)
             & (c < n_cmp) & (s < n_blocks))
    return jnp.asarray(np.concatenate([cover, cover], axis=0), BF16)


def _expand_matrix(n_keys, block0=0):
    e = (np.arange(n_keys)[None, :] // SLC_BLOCK) == (block0 + np.arange(LANES)[:, None])
    return e


def _expand_tiles(n_keys, tk):
    e = _expand_matrix(n_keys).reshape(LANES, n_keys // tk, tk)
    return jnp.asarray(np.transpose(e, (1, 0, 2)), BF16)


def _nsa_prompt(score_bound, q, gates, kc, vc, ks_bf, vs_bf, kw_bf, vw_bf, t):
    b = q.shape[0]
    ncp = kc.shape[1]
    n_cmp = t // CMP_STRIDE - CMP_BLOCK // CMP_STRIDE + 1
    n_blocks = t // SLC_BLOCK
    assert n_blocks <= LANES and t % SEL_TK == 0 and t >= WINDOW + Q_BLOCK
    cover = _cover_matrix(n_cmp, ncp, n_blocks, LANES)
    expand = _expand_tiles(t, S_TILE)
    per_b = lambda rows, w: pl.BlockSpec((1, rows, w), lambda bi, i: (bi, 0, 0))
    tiles = lambda a: pl.BlockSpec((1,) + a.shape[1:], lambda bi, i: (bi, 0, 0, 0))
    return pl.pallas_call(
        functools.partial(_nsa_prompt_kernel, n_blocks=n_blocks, n_sel=min(N_SELECT, n_blocks)),
        grid=(b, t // Q_BLOCK),
        in_specs=[pl.BlockSpec(memory_space=pltpu.SMEM),
                  pl.BlockSpec((1, Q_BLOCK, A_HEADS * LANES), lambda bi, i: (bi, i, 0)),
                  pl.BlockSpec((1, Q_BLOCK, LANES), lambda bi, i: (bi, i, 0)),
                  per_b(ncp, LANES), per_b(ncp, LANES),
                  tiles(ks_bf), tiles(vs_bf), tiles(kw_bf), tiles(vw_bf),
                  _const_spec(cover.shape), _const_spec(expand.shape)],
        out_specs=pl.BlockSpec((1, Q_BLOCK, A_HEADS * A_HEAD_DIM), lambda bi, i: (bi, i, 0)),
        out_shape=jax.ShapeDtypeStruct((b, t, A_HEADS * A_HEAD_DIM), BF16),
        compiler_params=_params(("parallel", "parallel")),
        name="nsa_prompt",
    )(score_bound, q, gates, kc, vc, ks_bf, vs_bf, kw_bf, vw_bf, cover, expand)


def _key_page_copy(pool_ref, page, bufs, half, slot, sems):
    dst = bufs.at[half, :, pl.ds(pl.multiple_of(slot * PAGE_SIZE, PAGE_SIZE), PAGE_SIZE)]
    return pltpu.make_async_copy(pool_ref.at[page], dst, sems.at[half])


def _nsa_sample_kernel(pt_ref, q_ref, gate_ref, kc_ref, vc_ref, pool_ref, new_ref, win_ref, wnew_ref,
                       cover_ref, exp_ref, o_ref, bufs, sems, *, n_pages, n_blocks, n_sel, key_chunk):
    b = pl.program_id(0)
    half = b % 2

    def start_row(row, into):
        def start(p, carry):
            _key_page_copy(pool_ref, pt_ref[row, p], bufs, into, p, sems).start()
            return carry
        lax.fori_loop(0, n_pages, start, 0)

    @pl.when(b == 0)
    def _():
        start_row(0, 0)

    @pl.when(b + 1 < pl.num_programs(0))
    def _():
        start_row(b + 1, 1 - half)

    buf = bufs.at[half]
    past = n_pages * PAGE_SIZE
    ts = q_ref.shape[1]
    buf[:, past:past + LANES] = new_ref[0]

    q = q_ref[0].astype(F32)
    q_all = jnp.concatenate([q[:, h * LANES:(h + 1) * LANES] for h in range(A_HEADS)], axis=0).astype(BF16)
    rows = A_HEADS * ts
    qpos1 = past + lax.broadcasted_iota(jnp.int32, (ts, 1), 0)
    qpos_all = jnp.concatenate([qpos1] * A_HEADS, axis=0)

    ncp = kc_ref.shape[1]
    c_end = lax.broadcasted_iota(jnp.int32, (1, ncp), 1) * CMP_STRIDE + (CMP_BLOCK - 1)
    p_c = _softmax_rows(_dot_nt(q_all, kc_ref[0]), c_end <= qpos_all)
    o_c = _dot(p_c.astype(BF16), vc_ref[0])

    p_groups = []
    for g in range(A_KV_GROUPS):
        base = g * A_HPG * ts
        p_sum = p_c[base:base + ts]
        for hh in range(1, A_HPG):
            p_sum = p_sum + p_c[base + hh * ts:base + (hh + 1) * ts]
        p_groups.append(p_sum)
    qpos_g = jnp.concatenate([qpos1] * A_KV_GROUPS, axis=0)
    score = _block_scores(jnp.concatenate(p_groups, axis=0), cover_ref, qpos_g, n_blocks)
    sel = _topk_select(score, n_blocks, n_sel).astype(BF16)

    def wait(p, carry):
        _key_page_copy(pool_ref, 0, bufs, half, p, sems).wait()
        return carry

    lax.fori_loop(0, n_pages, wait, 0)

    n_keys = buf.shape[1]
    carry = _online_init(rows)
    for ck in range(-(-n_keys // key_chunk)):
        k0 = ck * key_chunk
        kn = min(key_chunk, n_keys - k0)
        s = _dot(q_all, buf[0:LANES, k0:k0 + kn].astype(BF16))
        blk0 = ck * (key_chunk // SLC_BLOCK)
        em = _dot(sel[:, blk0:blk0 + LANES], exp_ref[:, 0:kn])
        kpos = k0 + lax.broadcasted_iota(jnp.int32, (1, kn), 1)
        bias = jnp.where((em > 0.5) & (kpos <= qpos_g), 0.0, NEG_INF)
        biases = []
        for g in range(A_KV_GROUPS):
            biases += [bias[g * ts:(g + 1) * ts]] * A_HPG
        carry = _online_step_t(carry, s + jnp.concatenate(biases, axis=0),
                               buf[LANES:KV_ROW, k0:k0 + kn].astype(BF16))
    o_s = _online_finish(carry)

    wb = win_ref.shape[2]
    kv_w = jnp.concatenate([win_ref[0], wnew_ref[0]], axis=1).astype(BF16)
    idx = lax.broadcasted_iota(jnp.int32, (1, wb + LANES), 1)
    w_pos = jnp.where(idx < wb, past - wb + idx, past + idx - wb)
    d = qpos_all - w_pos
    m_w = (d >= 0) & (d < WINDOW) & (w_pos >= 0)
    p_w = _softmax_rows(_dot(q_all, kv_w[0:LANES]), m_w)
    o_w = _dot_nt(p_w.astype(BF16), kv_w[LANES:KV_ROW])

    gates = gate_ref[0]
    heads = []
    for h in range(A_HEADS):
        r = slice(h * ts, (h + 1) * ts)
        heads.append(gates[:, 3 * h:3 * h + 1] * o_c[r] + gates[:, 3 * h + 1:3 * h + 2] * o_s[r]
                     + gates[:, 3 * h + 2:3 * h + 3] * o_w[r])
    low64 = lax.broadcasted_iota(jnp.int32, (ts, LANES), 1) < A_HEAD_DIM
    for k, chunk in enumerate(_assemble_heads(heads, low64)):
        o_ref[0, :, k * LANES:(k + 1) * LANES] = chunk.astype(BF16)


def _nsa_sample(page_table, q, gates, kc, vc, pool, new_tile, win_t, wnew_tile):
    db, n_pages = page_table.shape
    wb = win_t.shape[2]
    assert wb % LANES == 0
    ts = q.shape[1]
    past = n_pages * PAGE_SIZE
    ncp = kc.shape[1]
    t_pad = _round_up(past + ts, SLC_BLOCK)
    n_cmp = t_pad // CMP_STRIDE - CMP_BLOCK // CMP_STRIDE + 1
    n_blocks = t_pad // SLC_BLOCK
    key_chunk = LANES * SLC_BLOCK
    n_keys = past + LANES
    nsp = LANES * (-(-n_keys // key_chunk))
    assert nsp >= n_blocks and ts % SUBLANES == 0
    cover = _cover_matrix(n_cmp, ncp, n_blocks, nsp)
    expand = jnp.asarray(_expand_matrix(min(key_chunk, n_keys)), BF16)
    cm = lambda nd: (lambda i, pt: (0,) * nd)
    per_b = lambda r, w: pl.BlockSpec((1, r, w), lambda i, pt: (i, 0, 0))
    grid_spec = pltpu.PrefetchScalarGridSpec(
        num_scalar_prefetch=1,
        grid=(db,),
        in_specs=[per_b(ts, A_HEADS * LANES), per_b(ts, LANES), per_b(ncp, LANES), per_b(ncp, LANES),
                  pl.BlockSpec(memory_space=pl.ANY), per_b(KV_ROW, LANES), per_b(KV_ROW, wb), per_b(KV_ROW, LANES),
                  pl.BlockSpec(cover.shape, cm(2), pipeline_mode=pl.Buffered(1)),
                  pl.BlockSpec(expand.shape, cm(2), pipeline_mode=pl.Buffered(1))],
        out_specs=per_b(ts, A_HEADS * A_HEAD_DIM),
        scratch_shapes=[pltpu.VMEM((2, KV_ROW, n_keys), F32), pltpu.SemaphoreType.DMA((2,))],
    )
    return pl.pallas_call(
        functools.partial(_nsa_sample_kernel, n_pages=n_pages, n_blocks=n_blocks,
                          n_sel=min(N_SELECT, n_blocks), key_chunk=key_chunk),
        grid_spec=grid_spec,
        out_shape=jax.ShapeDtypeStruct((db, ts, A_HEADS * A_HEAD_DIM), BF16),
        compiler_params=_params(("arbitrary",)),
        name="nsa_sample",
    )(page_table, q, gates, kc, vc, pool, new_tile, win_t, wnew_tile, cover, expand)


def _hgrn_matrices():
    c = HGRN_CHUNK
    t = np.arange(c)[:, None]
    u = np.arange(c)[None, :]
    mats = [u <= t]
    masks = [t == u]
    for lvl in range(HGRN_LEVELS):
        m = 1 << lvl
        mid = (t // (2 * m)) * (2 * m) + m - 1
        mats.append((u > mid) & (u <= t))
        mats.append((u > t) & (u <= mid))
        masks.append((t // (2 * m) == u // (2 * m)) & (t % (2 * m) >= m) & (u % (2 * m) < m))
    mats.append(u > t)
    pm = np.concatenate(mats, axis=0)
    return (jnp.asarray(np.concatenate([pm, pm], axis=1), BF16),
            jnp.asarray(np.stack(masks).astype(np.float32)))


def _hgrn_kernel(qb_ref, kb_ref, vb_ref, lf_ref, gb_ref, s0_ref, gn_ref, pm_ref, lm_ref,
                 ob_ref, sout_ref, st_scr, pad_scr):
    j = pl.program_id(1)
    c = HGRN_CHUNK
    t_blk = qb_ref.shape[1]

    @pl.when(j == 0)
    def _():
        for h in range(B_HEADS):
            st_scr[h] = s0_ref[0, h].T

    def load(ref, slot, rows):
        if t_blk >= c:
            return ref[0, rows, :]
        pad_scr[slot] = jnp.zeros((c, B_HEADS * B_KEY_DIM), F32)
        pad_scr[slot, 0:t_blk, :] = ref[0]
        return pad_scr[slot]

    gn = gn_ref[...]
    states = [st_scr[h] for h in range(B_HEADS)]
    for sub in range(max(t_blk // c, 1)):
        rows = slice(sub * c, (sub + 1) * c)
        out_rows = rows if t_blk >= c else slice(0, t_blk)
        qb, kb, vb, lf = load(qb_ref, 0, rows), load(kb_ref, 1, rows), load(vb_ref, 2, rows), load(lf_ref, 3, rows)
        lf_hi, lf_lo = _split_bf16(lf)
        ex = _dot(pm_ref[...], jnp.concatenate([lf_hi, lf_lo], axis=0))
        for h in range(B_HEADS):
            sl = slice(h * B_KEY_DIM, (h + 1) * B_KEY_DIM)
            q, k = qb[:, sl], kb[:, sl]
            v = vb[:, sl].astype(BF16)
            b_cum = ex[0:c, sl]
            a = lm_ref[0] * _dot_nt(q.astype(BF16), k.astype(BF16))
            for lvl in range(HGRN_LEVELS):
                eq = ex[(2 * lvl + 1) * c:(2 * lvl + 2) * c, sl]
                ek = ex[(2 * lvl + 2) * c:(2 * lvl + 3) * c, sl]
                a = a + lm_ref[lvl + 1] * _dot_nt((q * jnp.exp(eq)).astype(BF16), (k * jnp.exp(ek)).astype(BF16))
            st = states[h]
            o = _dot_nt((q * jnp.exp(b_cum)).astype(BF16), st.astype(BF16)) + _dot(a.astype(BF16), v)
            e_end = ex[(2 * HGRN_LEVELS + 1) * c:(2 * HGRN_LEVELS + 2) * c, sl]
            states[h] = st * jnp.exp(b_cum[c - 1:c, :]) + _dot_tn(v, (k * jnp.exp(e_end)).astype(BF16))
            y = o * lax.rsqrt(jnp.mean(o * o, axis=-1, keepdims=True) + EPS) * gn
            ob_ref[0, out_rows, sl] = y[0:min(t_blk, c)] * gb_ref[0, out_rows, sl]
    for h in range(B_HEADS):
        st_scr[h] = states[h]

    @pl.when(j == pl.num_programs(1) - 1)
    def _():
        for h in range(B_HEADS):
            sout_ref[0, h] = st_scr[h].T


def _hgrn(qb, kb, vb, lf, gb, s0, consts):
    b, t, w = qb.shape
    t_blk = min(t, HGRN_STEP_CHUNKS * HGRN_CHUNK)
    assert t % t_blk == 0 and (t_blk % HGRN_CHUNK == 0 or t == t_blk < HGRN_CHUNK)
    tok = pl.BlockSpec((1, t_blk, w), lambda bi, j: (bi, j, 0))
    st = pl.BlockSpec((1, B_HEADS, B_KEY_DIM, B_KEY_DIM), lambda bi, j: (bi, 0, 0, 0))
    return pl.pallas_call(
        _hgrn_kernel,
        grid=(b, t // t_blk),
        in_specs=[tok, tok, tok, tok, tok, st, _const_spec((1, B_KEY_DIM)),
                  _const_spec(consts['hgrn_pm'].shape), _const_spec(consts['hgrn_lm'].shape)],
        out_specs=[tok, st],
        out_shape=[jax.ShapeDtypeStruct((b, t, w), F32),
                   jax.ShapeDtypeStruct((b, B_HEADS, B_KEY_DIM, B_KEY_DIM), F32)],
        scratch_shapes=[pltpu.VMEM((B_HEADS, B_KEY_DIM, B_KEY_DIM), F32),
                        pltpu.VMEM((4, HGRN_CHUNK, w), F32)],
        compiler_params=_params(("parallel", "arbitrary")),
        name="hgrn",
    )(qb, kb, vb, lf, gb, s0, consts['hgrn_g'], consts['hgrn_pm'], consts['hgrn_lm'])


def _merge_ffn_kernel(x_ref, oa_ref, ob_ref, mg_ref, p1_ref, p2_ref, wa_ref, wb_ref, wo_ref, fg_ref, win_ref,
                      cw_ref, cb_ref, wout_ref, y_ref, a_ref, carry_scr, *, seq_len):
    tm = x_ref.shape[0]
    mg = mg_ref[...]
    m = (mg[:, :D_MODEL] * _dot(oa_ref[...], wa_ref[...])
         + mg[:, D_MODEL:] * _dot(ob_ref[...].astype(BF16), wb_ref[...]))
    x2 = x_ref[...] + _dot(m.astype(BF16), wo_ref[...])
    h = (x2 * lax.rsqrt(jnp.mean(x2 * x2, axis=-1, keepdims=True) + EPS) * fg_ref[...]).astype(BF16)
    a = _dot(h, win_ref[:, :D_FF])
    gate = _dot(h, win_ref[:, D_FF:])
    row = lax.broadcasted_iota(jnp.int32, (tm, 1), 0)
    if seq_len >= tm:
        j = pl.program_id(0) % (seq_len // tm)

        @pl.when(j == 0)
        def _():
            carry_scr[0:2, :] = p1_ref[0]

        prev = carry_scr[...]
        a1 = jnp.where(row == 0, prev[1:2], pltpu.roll(a, 1, 0))
        a2 = jnp.where(row == 0, prev[0:1], jnp.where(row == 1, prev[1:2], pltpu.roll(a, 2, 0)))
        carry_scr[0:2, :] = a[tm - 2:tm]
        a_ref[0] = a[tm - 2:tm]
    else:
        t = row % seq_len
        a1 = jnp.where(t == 0, p1_ref[...], pltpu.roll(a, 1, 0))
        a2 = jnp.where(t < 2, p2_ref[...], pltpu.roll(a, 2, 0))
        a_ref[...] = a
    a_conv = cb_ref[...] + a2 * cw_ref[0:1, :] + a1 * cw_ref[1:2, :] + a * cw_ref[2:3, :]
    act = a_conv * jax.nn.sigmoid(a_conv) * gate
    y_ref[...] = x2 + _dot(act.astype(BF16), wout_ref[...])


def _merge_ffn(x2d, oa, ob, mg, conv_state, seq_len, consts, tm):
    n = x2d.shape[0]
    b = n // seq_len
    row = lambda w: pl.BlockSpec((tm, w), lambda i: (i, 0))
    if seq_len >= tm:
        assert seq_len % tm == 0
        per_seq = seq_len // tm
        p1, p2 = conv_state, conv_state
        p_spec = pl.BlockSpec((1, FFN_CONV - 1, D_FF), lambda i: (i // per_seq, 0, 0))
        a_shape = jax.ShapeDtypeStruct((b, FFN_CONV - 1, D_FF), F32)
        a_spec = pl.BlockSpec((1, FFN_CONV - 1, D_FF), lambda i: (i // per_seq, 0, 0))
    else:
        assert tm % seq_len == 0 and seq_len >= FFN_CONV - 1
        zeros = jnp.zeros((b, seq_len - 1, D_FF), F32)
        p1 = jnp.concatenate([conv_state[:, 1:2], zeros], axis=1).reshape(n, D_FF)
        p2 = jnp.concatenate([conv_state, zeros[:, 1:]], axis=1).reshape(n, D_FF)
        p_spec = row(D_FF)
        a_shape = jax.ShapeDtypeStruct((n, D_FF), F32)
        a_spec = row(D_FF)
    y, a_out = pl.pallas_call(
        functools.partial(_merge_ffn_kernel, seq_len=seq_len),
        grid=(n // tm,),
        in_specs=[row(D_MODEL), row(A_HEADS * A_HEAD_DIM), row(B_HEADS * B_KEY_DIM), row(2 * D_MODEL),
                  p_spec, p_spec,
                  _const_spec((A_HEADS * A_HEAD_DIM, D_MODEL)), _const_spec((B_HEADS * B_KEY_DIM, D_MODEL)),
                  _const_spec((D_MODEL, D_MODEL)), _const_spec((1, D_MODEL)),
                  _const_spec((D_MODEL, 2 * D_FF)), _const_spec((FFN_CONV, D_FF)), _const_spec((1, D_FF)),
                  _const_spec((D_FF, D_MODEL))],
        out_specs=[row(D_MODEL), a_spec],
        out_shape=[jax.ShapeDtypeStruct((n, D_MODEL), F32), a_shape],
        scratch_shapes=[pltpu.VMEM((SUBLANES, D_FF), F32)],
        compiler_params=_params(("arbitrary",)),
        name="merge_ffn",
    )(x2d, oa, ob, mg, p1, p2, consts['w_a'], consts['w_b'], consts['w_out'], consts['ffn_g'],
      consts['ffn_w_in'], consts['conv_w'], consts['conv_b'], consts['ffn_w_out'])
    if seq_len >= tm:
        return y, a_out
    return y, a_out.reshape(b, seq_len, D_FF)[:, seq_len - (FFN_CONV - 1):]


def _prepare_consts(attn_norm_g, w_in, q_norm_g, k_norm_g, cmp_pos_emb, cmp_w1, cmp_w2, hgrn_lb_logits,
                    hgrn_norm_g, w_branch, w_out, ffn_norm_g, ffn_w_in, ffn_conv_w, ffn_conv_b, ffn_w_out):
    n_q = A_HEADS * A_HEAD_DIM
    gate_lo = n_q + 3 * KV_ROW
    gate_hi = gate_lo + 3 * A_HEADS
    w_gate = jnp.pad(w_in[:, gate_lo:gate_hi], ((0, 0), (0, LANES - 3 * A_HEADS))).astype(BF16)
    seg = np.arange(LANES) // A_HEAD_DIM
    mseg = (seg[:, None] == seg[None, :]).astype(np.float32) / A_HEAD_DIM
    eye = jnp.eye(2 * A_KV_GROUPS, dtype=F32)
    jsel = np.repeat(np.arange(2), A_KV_GROUPS)
    w1 = cmp_w1.reshape(2, CMP_BLOCK, A_HEAD_DIM, A_HEAD_DIM)[jsel]
    w1_bd = jnp.einsum('ab,apde->padbe', eye, w1).reshape(CMP_W1_SHAPE).astype(BF16)
    w2_bd = jnp.einsum('ab,ade->adbe', eye, cmp_w2[jsel]).reshape(KV_ROW, KV_ROW).astype(BF16)
    pe = jnp.transpose(cmp_pos_emb[jsel], (1, 0, 2)).reshape(CMP_BLOCK, KV_ROW)
    pm, lm = _hgrn_matrices()
    return {
        'attn_g': attn_norm_g.reshape(1, D_MODEL), 'w_attn': w_in[:, :gate_lo].astype(BF16),
        'w_rest': w_in[:, gate_hi:].astype(BF16), 'w_gate': w_gate,
        'q_g': jnp.tile(q_norm_g, 2).reshape(1, LANES), 'k_g': jnp.tile(k_norm_g, (1, 2)),
        'lbl': hgrn_lb_logits.astype(F32), 'mseg': jnp.asarray(np.concatenate([mseg, mseg], axis=0), BF16),
        'cmp_pe': pe, 'cmp_w1': w1_bd, 'cmp_w2': w2_bd,
        'hgrn_g': hgrn_norm_g.reshape(1, B_KEY_DIM), 'hgrn_pm': pm, 'hgrn_lm': lm,
        'w_a': w_branch[:n_q].astype(BF16), 'w_b': w_branch[n_q:].astype(BF16), 'w_out': w_out.astype(BF16),
        'ffn_g': ffn_norm_g.reshape(1, D_MODEL), 'ffn_w_in': ffn_w_in.astype(BF16),
        'conv_w': ffn_conv_w, 'conv_b': ffn_conv_b.reshape(1, D_FF), 'ffn_w_out': ffn_w_out.astype(BF16),
    }


def _rope_tables(pos, reps):
    half = A_HEAD_DIM // 2
    inv = ROPE_THETA ** (-jnp.arange(half, dtype=F32) / half)
    ang = pos.astype(F32)[:, None] * inv[None, :]
    cos, sin = jnp.cos(ang), jnp.sin(ang)
    cos_t = jnp.tile(cos, (reps, LANES // half))
    sin_t = jnp.tile(jnp.concatenate([-sin, sin], axis=-1), (reps, LANES // A_HEAD_DIM))
    return cos_t, sin_t


def kernel(x_prompt, x_sample, cache_cmp_kv, cache_slc_kv, page_table, state_win_kv, state_hgrn, state_ffn_conv, attn_norm_g, w_in, q_norm_g, k_norm_g, cmp_pos_emb, cmp_w1, cmp_w2, hgrn_lb_logits, hgrn_norm_g, w_branch, w_out, ffn_norm_g, ffn_w_in, ffn_conv_w, ffn_conv_b, ffn_w_out):
    assert w_in.shape[0] == 1, "single-layer step"
    b, t, _ = x_prompt.shape
    db, ts, _ = x_sample.shape
    n_pool = cache_cmp_kv.shape[1]
    past = page_table.shape[1] * PAGE_SIZE
    wb = state_win_kv.shape[2]
    assert t % SLC_BLOCK == 0 and t % Q_BLOCK == 0
    consts = _prepare_consts(attn_norm_g[0], w_in[0], q_norm_g[0], k_norm_g[0], cmp_pos_emb[0], cmp_w1[0],
                             cmp_w2[0], hgrn_lb_logits, hgrn_norm_g[0], w_branch[0], w_out[0], ffn_norm_g[0],
                             ffn_w_in[0], ffn_conv_w[0], ffn_conv_b[0], ffn_w_out[0])
    kv_shape = (2, A_KV_GROUPS, A_HEAD_DIM)
    tm_p = min(256, t)
    n_s = db * ts

    fp = _inproj(x_prompt.reshape(b * t, D_MODEL), *_rope_tables(jnp.arange(t, dtype=jnp.int32), 1), consts, tm_p,
                 seq_len=t)
    seq = lambda a: a.reshape(b, t, a.shape[-1])
    kc_p, vc_p = _compress_dense(seq(fp['kvc']), consts)
    score_bound = (1.01 * A_HEAD_DIM ** 0.5 * jnp.max(jnp.abs(q_norm_g[0]))
                   * jnp.max(jnp.abs(k_norm_g[0, 1:]))).reshape(1).astype(F32)
    oa_p = _nsa_prompt(score_bound, seq(fp['q']), seq(fp['gate']), kc_p, vc_p, fp['ks_bf'], fp['vs_bf'],
                       fp['kw_bf'], fp['vw_bf'], t)
    ob_p, s_p = _hgrn(seq(fp['qb']), seq(fp['kb']), seq(fp['vb']), seq(fp['lf']), seq(fp['gb']),
                      jnp.zeros((b, B_HEADS, B_KEY_DIM, B_KEY_DIM), F32), consts)
    y_p, conv_p = _merge_ffn(x_prompt.reshape(b * t, D_MODEL), oa_p.reshape(b * t, -1), ob_p.reshape(b * t, -1),
                             fp['mg'], jnp.zeros((b, FFN_CONV - 1, D_FF), F32), t, consts, min(512, t))

    fs = _inproj(x_sample.reshape(n_s, D_MODEL), *_rope_tables(past + jnp.arange(ts, dtype=jnp.int32), db),
                 consts, n_s)
    sseq = lambda a: a.reshape(db, ts, a.shape[-1])
    feat = lambda c: jnp.transpose(c, (0, 2, 3, 4, 1)).reshape(c.shape[0], KV_ROW, c.shape[1])
    new_tile = lambda rows: jnp.pad(jnp.transpose(sseq(rows), (0, 2, 1)), ((0, 0), (0, 0), (0, LANES - ts)))
    kc_s, vc_s = _compress_paged(page_table, feat(cache_cmp_kv[0]), sseq(fs['kvc']), consts)
    oa_s = _nsa_sample(page_table, sseq(fs['q']), sseq(fs['gate']), kc_s, vc_s, feat(cache_slc_kv[0]),
                       new_tile(fs['kvs']), feat(state_win_kv[0]), new_tile(fs['kvw']))
    win_cat = jnp.concatenate([state_win_kv[0].reshape(db, wb, KV_ROW), sseq(fs['kvw'])], axis=1)
    ob_s, s_s = _hgrn(sseq(fs['qb']), sseq(fs['kb']), sseq(fs['vb']), sseq(fs['lf']), sseq(fs['gb']),
                      state_hgrn[0].astype(F32), consts)
    y_s, conv_s = _merge_ffn(x_sample.reshape(n_s, D_MODEL), oa_s.reshape(n_s, -1), ob_s.reshape(n_s, -1),
                             fs['mg'], state_ffn_conv[0], ts, consts, n_s)

    wkeep = min(WINDOW, t)
    unfeat = lambda a: jnp.transpose(a.reshape(b, *kv_shape, a.shape[-1]), (0, 4, 1, 2, 3))[None]
    return (y_p.reshape(b, t, D_MODEL), y_s.reshape(db, ts, D_MODEL),
            unfeat(fp['kvc_t']), fs['kvc'].reshape(1, db, ts, *kv_shape),
            unfeat(fp['kvs_t']), fs['kvs'].reshape(1, db, ts, *kv_shape),
            unfeat(fp['kvw_t'][:, :, t - wkeep:]),
            win_cat[:, ts:].reshape(1, db, wb, *kv_shape),
            s_p[None], s_s[None], conv_p[None], conv_s[None])
```

```python
import functools

import numpy as np
import jax
import jax.numpy as jnp
from jax import lax
from jax.experimental import pallas as pl
from jax.experimental.pallas import tpu as pltpu

F32 = jnp.float32
BF16 = jnp.bfloat16

D_MODEL = 1024
PAGE_SIZE = 128
A_HEADS = 8
A_KV_GROUPS = 2
A_HPG = A_HEADS // A_KV_GROUPS
A_HEAD_DIM = 64
CMP_BLOCK = 32
CMP_STRIDE = 16
SLC_BLOCK = 64
N_SELECT = 16
WINDOW = 512
Q_BLOCK = 128
ROPE_THETA = 10000.0
FORCE_SCORE = 1e4
NEG_INF = -1e30
B_HEADS = 4
B_KEY_DIM = 128
D_FF = 2816
FFN_CONV = 3
EPS = 1e-6

LANES = 128
SUBLANES = 8
KV_ROW = 2 * A_KV_GROUPS * A_HEAD_DIM
S_TILE = 256
W_TILE = 128
SEL_TK = 512
MASK_BIG = 1e30
SCORE_SAFE = 40.0
CHUNK_PITCH = 24
LAND_GROUP = 8
CMP_GROUP = 4
CMP_W1_SHAPE = (CMP_BLOCK // CMP_GROUP, CMP_GROUP * KV_ROW, KV_ROW)
FFN_CHUNK = 1408
HGRN_CHUNK = 128
HGRN_LEVELS = 7
HGRN_STEP_CHUNKS = 4
HGRN_SHORT_ROWS = 4
VMEM_LIMIT = 56 * 1024 * 1024

OFF_Q, OFF_KVC, OFF_KVS, OFF_KVW = 0, 512, 768, 1024
OFF_QB, OFF_FB, OFF_IB, OFF_GB, OFF_MG, OFF_GATE = 1280, 1792, 2304, 2816, 3328, 5376
W_PACK = 5504


def _dot(a, b):
    return jnp.dot(a, b, preferred_element_type=F32)


def _dot_nt(a, b):
    return lax.dot_general(a, b, (((1,), (1,)), ((), ())), preferred_element_type=F32)


def _dot_tn(a, b):
    return lax.dot_general(a, b, (((0,), (0,)), ((), ())), preferred_element_type=F32)


def _split_bf16(x):
    hi = x.astype(BF16)
    lo = (x - hi.astype(F32)).astype(BF16)
    return hi, lo


def _round_up(n, m):
    return -(-n // m) * m


def _const_spec(shape):
    nd = len(shape)
    return pl.BlockSpec(shape, lambda *_: (0,) * nd, pipeline_mode=pl.Buffered(1))


def _params(semantics):
    return pltpu.CompilerParams(dimension_semantics=semantics, vmem_limit_bytes=VMEM_LIMIT)


INPROJ_COMMON = ('q', 'gate', 'qb', 'kb', 'vb', 'lf', 'gb', 'mg')
INPROJ_PROMPT = INPROJ_COMMON + ('kvc', 'kvc_t', 'kvs_t', 'kvw_t', 'ks_bf', 'vs_bf', 'kw_bf', 'vw_bf')
INPROJ_SAMPLE = INPROJ_COMMON + ('kvc', 'kvs', 'kvw')


def _inproj_kernel(x_ref, g_ref, wa_ref, wb_ref, wg_ref, cos_ref, sin_ref, qg_ref, kg_ref, lbl_ref, mseg_ref,
                   *out_refs, names):
    o = dict(zip(names, out_refs))
    q_ref, gate_ref, mg_ref = o['q'], o['gate'], o['mg']
    qb_ref, kb_ref, vb_ref, lf_ref, gb_ref = o['qb'], o['kb'], o['vb'], o['lf'], o['gb']
    x = x_ref[...]
    ms = jnp.mean(x * x, axis=-1, keepdims=True)
    h = (x * lax.rsqrt(ms + EPS) * g_ref[...]).astype(BF16)
    cos = cos_ref[...]
    sin = sin_ref[...]
    tm = x.shape[0]
    lane = lax.broadcasted_iota(jnp.int32, (tm, LANES), 1)
    first_half = (lane & (A_HEAD_DIM // 2)) == 0
    low64 = lane < A_HEAD_DIM
    mseg = mseg_ref[...]

    def proj(lo, width):
        for start, ref in ((OFF_GATE, wg_ref), (OFF_QB, wb_ref), (OFF_Q, wa_ref)):
            if lo >= start:
                return _dot(h, ref[:, lo - start:lo - start + width])

    def head_norm_rope(chunk, gain):
        s_hi, s_lo = _split_bf16(chunk * chunk)
        mean = _dot(jnp.concatenate([s_hi, s_lo], axis=-1), mseg)
        y = chunk * lax.rsqrt(mean + EPS) * gain
        rot = jnp.where(first_half, pltpu.roll(y, LANES - A_HEAD_DIM // 2, 1),
                        pltpu.roll(y, A_HEAD_DIM // 2, 1))
        return y * cos + rot * sin

    zq = proj(OFF_Q, A_HEADS * A_HEAD_DIM)
    qg = qg_ref[...]
    for k in range(A_HEADS // 2):
        c = head_norm_rope(zq[:, k * LANES:(k + 1) * LANES], qg) * (A_HEAD_DIM ** -0.5)
        r = pltpu.roll(c, A_HEAD_DIM, 1)
        if k < A_HPG // 2:
            h0, h1 = jnp.where(low64, c, 0.0), jnp.where(low64, r, 0.0)
        else:
            h0, h1 = jnp.where(low64, 0.0, r), jnp.where(low64, 0.0, c)
        q_ref[:, (2 * k) * LANES:(2 * k + 1) * LANES] = h0.astype(BF16)
        q_ref[:, (2 * k + 1) * LANES:(2 * k + 2) * LANES] = h1.astype(BF16)

    for i, (off, name) in enumerate(((OFF_KVC, 'kvc'), (OFF_KVS, 'kvs'), (OFF_KVW, 'kvw'))):
        z = proj(off, KV_ROW)
        rows = jnp.concatenate([head_norm_rope(z[:, :LANES], kg_ref[i:i + 1, :]), z[:, LANES:]], axis=-1)
        if name in o:
            o[name][...] = rows
        if name + '_t' in o:
            rows_t = rows.T
            o[name + '_t'][0] = rows_t
            for half, tag in ((rows_t[:LANES], 'k'), (rows_t[LANES:], 'v')):
                key = tag + name[-1] + '_bf'
                if key in o:
                    n_tiles, _, width = o[key].shape[1:]
                    for j in range(n_tiles):
                        o[key][0, j] = half[:, j * width:(j + 1) * width].astype(BF16)

    gate_ref[...] = jax.nn.sigmoid(proj(OFF_GATE, LANES))

    lbl = lbl_ref[...]
    e = jnp.exp(lbl - jnp.max(lbl, axis=0, keepdims=True))
    lb = e[0:1, :] / jnp.sum(e, axis=0, keepdims=True)
    fz = proj(OFF_FB, B_HEADS * B_KEY_DIM)
    lf_ref[...] = jnp.log(lb + (1.0 - lb) * jax.nn.sigmoid(fz))
    kb_ref[...] = (1.0 - lb) * jax.nn.sigmoid(-fz)
    zqb = proj(OFF_QB, B_HEADS * B_KEY_DIM)
    qb_ref[...] = zqb * jax.nn.sigmoid(zqb)
    vb_ref[...] = proj(OFF_IB, B_HEADS * B_KEY_DIM)
    zg = proj(OFF_GB, B_HEADS * B_KEY_DIM)
    gb_ref[...] = zg * jax.nn.sigmoid(zg)
    mg_ref[...] = jax.nn.sigmoid(proj(OFF_MG, 2 * D_MODEL))


def _inproj(x2d, cos_tab, sin_tab, consts, tm, seq_len=None):
    n = x2d.shape[0]
    tab_blocks = cos_tab.shape[0] // tm
    row = lambda w: pl.BlockSpec((tm, w), lambda i: (i, 0))
    tab = pl.BlockSpec((tm, LANES), lambda i: (i % tab_blocks, 0))
    token_major = {'q': (A_HEADS * LANES, BF16), 'gate': (LANES, F32), 'qb': (512, F32), 'kb': (512, F32),
                   'vb': (512, F32), 'lf': (512, F32), 'gb': (512, F32), 'mg': (2 * D_MODEL, F32),
                   'kvc': (KV_ROW, F32), 'kvs': (KV_ROW, F32), 'kvw': (KV_ROW, F32)}
    names = INPROJ_SAMPLE if seq_len is None else INPROJ_PROMPT
    specs, shapes = [], []
    for name in names:
        if name in token_major:
            w, dt = token_major[name]
            specs.append(row(w))
            shapes.append(jax.ShapeDtypeStruct((n, w), dt))
        elif name.endswith('_t'):
            per_seq = seq_len // tm
            specs.append(pl.BlockSpec((1, KV_ROW, tm), lambda i: (i // per_seq, 0, i % per_seq)))
            shapes.append(jax.ShapeDtypeStruct((n // seq_len, KV_ROW, seq_len), F32))
        else:
            width = S_TILE if name[1] == 's' else W_TILE
            per_seq = seq_len // tm
            specs.append(pl.BlockSpec((1, tm // width, LANES, width), lambda i: (i // per_seq, i % per_seq, 0, 0)))
            shapes.append(jax.ShapeDtypeStruct((n // seq_len, seq_len // width, LANES, width), BF16))
    outs = pl.pallas_call(
        functools.partial(_inproj_kernel, names=names),
        grid=(n // tm,),
        in_specs=[row(D_MODEL), _const_spec((1, D_MODEL)), _const_spec((D_MODEL, OFF_QB)),
                  _const_spec((D_MODEL, OFF_GATE - OFF_QB)), _const_spec((D_MODEL, LANES)), tab, tab,
                  _const_spec((1, LANES)), _const_spec((3, LANES)),
                  _const_spec(consts['lbl'].shape), _const_spec((2 * LANES, LANES))],
        out_specs=specs,
        out_shape=shapes,
        compiler_params=_params(("parallel",)),
        name="inproj",
    )(x2d, consts['attn_g'], consts['w_attn'], consts['w_rest'], consts['w_gate'], cos_tab, sin_tab,
      consts['q_g'], consts['k_g'],
      consts['lbl'], consts['mseg'])
    return dict(zip(names, outs))


def _chunk_row(chunk):
    return chunk * CHUNK_PITCH


def _compress_rows(tok_k, tok_v, pe_ref, w1_ref, w2_ref, kc_ref, vc_ref, m_rows, n_cmp):
    acc = jnp.zeros((m_rows, KV_ROW), F32)
    for p0 in range(0, CMP_BLOCK, CMP_GROUP):
        xs = []
        for p in range(p0, p0 + CMP_GROUP):
            start = _chunk_row(p // CMP_STRIDE) + p % CMP_STRIDE
            xp = jnp.concatenate([tok_k[pl.ds(start, m_rows, stride=CHUNK_PITCH), :],
                                  tok_v[pl.ds(start, m_rows, stride=CHUNK_PITCH), :]], axis=-1) + pe_ref[p:p + 1, :]
            xs.append(xp.astype(BF16))
        acc = acc + _dot(jnp.concatenate(xs, axis=-1), w1_ref[p0 // CMP_GROUP])
    hid = acc * jax.nn.sigmoid(acc)
    out = _dot(hid.astype(BF16), w2_ref[...])
    row = lax.broadcasted_iota(jnp.int32, out.shape, 0)
    out = jnp.where(row < n_cmp, out, 0.0)
    ncp = kc_ref.shape[1]
    kc_ref[0, 0:m_rows, :] = out[:, :LANES].astype(BF16)
    vc_ref[0, 0:m_rows, :] = out[:, LANES:].astype(BF16)
    if ncp > m_rows:
        kc_ref[0, m_rows:ncp, :] = jnp.zeros((ncp - m_rows, LANES), BF16)
        vc_ref[0, m_rows:ncp, :] = jnp.zeros((ncp - m_rows, LANES), BF16)


def _zero_chunks(tok_k, tok_v, first_chunk):
    r0 = _chunk_row(first_chunk)
    tok_k[r0:, :] = jnp.zeros((tok_k.shape[0] - r0, LANES), F32)
    tok_v[r0:, :] = jnp.zeros((tok_v.shape[0] - r0, LANES), F32)


def _compress_dense_kernel(rows_ref, pe_ref, w1_ref, w2_ref, kc_ref, vc_ref, tok_k, tok_v, *, m_rows, n_cmp):
    n_chunks = rows_ref.shape[1] // CMP_STRIDE

    def body(c, carry):
        src = pl.multiple_of(c * CMP_STRIDE, CMP_STRIDE)
        dst = pl.multiple_of(_chunk_row(c), SUBLANES)
        tok_k[pl.ds(dst, CMP_STRIDE), :] = rows_ref[0, pl.ds(src, CMP_STRIDE), 0:LANES]
        tok_v[pl.ds(dst, CMP_STRIDE), :] = rows_ref[0, pl.ds(src, CMP_STRIDE), LANES:KV_ROW]
        return carry

    lax.fori_loop(0, n_chunks, body, 0)
    _zero_chunks(tok_k, tok_v, n_chunks)
    _compress_rows(tok_k, tok_v, pe_ref, w1_ref, w2_ref, kc_ref, vc_ref, m_rows, n_cmp)


def _page_copy(pool_ref, page, stage, slot, sems):
    return pltpu.make_async_copy(pool_ref.at[page], stage.at[slot], sems.at[slot])


def _compress_paged_kernel(pt_ref, pool_ref, new_ref, pe_ref, w1_ref, w2_ref, kc_ref, vc_ref,
                           stage, tok_k, tok_v, sems, *, n_pages, m_rows, n_cmp):
    b = pl.program_id(0)

    def start_row(row):
        def start(p, carry):
            _page_copy(pool_ref, pt_ref[row, p], stage, p, sems).start()
            return carry
        lax.fori_loop(0, n_pages, start, 0)

    @pl.when(b == 0)
    def _():
        start_row(0)

    chunks_per_page = PAGE_SIZE // CMP_STRIDE
    first_new = n_pages * chunks_per_page
    _zero_chunks(tok_k, tok_v, first_new)
    ts = new_ref.shape[1]
    assert ts <= CMP_STRIDE
    tok_k[_chunk_row(first_new):_chunk_row(first_new) + ts, :] = new_ref[0, :, 0:LANES]
    tok_v[_chunk_row(first_new):_chunk_row(first_new) + ts, :] = new_ref[0, :, LANES:KV_ROW]

    group = int(np.gcd(n_pages, LAND_GROUP))

    def land(i, carry):
        for j in range(group):
            _page_copy(pool_ref, 0, stage, i * group + j, sems).wait()
        for j in range(group):
            p = i * group + j
            page_t = stage[p].T
            base = pl.multiple_of(_chunk_row(p * chunks_per_page), SUBLANES)
            for c in range(chunks_per_page):
                rows = slice(c * CMP_STRIDE, (c + 1) * CMP_STRIDE)
                tok_k[pl.ds(base + _chunk_row(c), CMP_STRIDE), :] = page_t[rows, 0:LANES]
                tok_v[pl.ds(base + _chunk_row(c), CMP_STRIDE), :] = page_t[rows, LANES:KV_ROW]
        return carry

    lax.fori_loop(0, n_pages // group, land, 0)

    @pl.when(b + 1 < pl.num_programs(0))
    def _():
        start_row(b + 1)

    _compress_rows(tok_k, tok_v, pe_ref, w1_ref, w2_ref, kc_ref, vc_ref, m_rows, n_cmp)


def _compress_geometry(t_real):
    t_pad = _round_up(t_real, SLC_BLOCK)
    n_cmp = t_pad // CMP_STRIDE - CMP_BLOCK // CMP_STRIDE + 1
    m_rows = _round_up(n_cmp, SUBLANES)
    ncp = _round_up(n_cmp, LANES)
    n_chunks = max(m_rows + CMP_BLOCK // CMP_STRIDE - 1, -(-t_real // CMP_STRIDE))
    return n_cmp, m_rows, ncp, _chunk_row(n_chunks)


def _compress_dense(rows, consts):
    b, t, _ = rows.shape
    n_cmp, m_rows, ncp, tok_rows = _compress_geometry(t)
    return pl.pallas_call(
        functools.partial(_compress_dense_kernel, m_rows=m_rows, n_cmp=n_cmp),
        grid=(b,),
        in_specs=[pl.BlockSpec((1, t, KV_ROW), lambda i: (i, 0, 0)),
                  _const_spec((CMP_BLOCK, KV_ROW)), _const_spec(CMP_W1_SHAPE),
                  _const_spec((KV_ROW, KV_ROW))],
        out_specs=[pl.BlockSpec((1, ncp, LANES), lambda i: (i, 0, 0))] * 2,
        out_shape=[jax.ShapeDtypeStruct((b, ncp, LANES), BF16)] * 2,
        scratch_shapes=[pltpu.VMEM((tok_rows, LANES), F32)] * 2,
        compiler_params=_params(("parallel",)),
        name="compress_dense",
    )(rows, consts['cmp_pe'], consts['cmp_w1'], consts['cmp_w2'])


def _compress_paged(page_table, pool, new_rows, consts):
    db, n_pages = page_table.shape
    ts = new_rows.shape[1]
    n_cmp, m_rows, ncp, tok_rows = _compress_geometry(n_pages * PAGE_SIZE + ts)
    cm = lambda nd: (lambda i, pt: (0,) * nd)
    grid_spec = pltpu.PrefetchScalarGridSpec(
        num_scalar_prefetch=1,
        grid=(db,),
        in_specs=[pl.BlockSpec(memory_space=pl.ANY),
                  pl.BlockSpec((1, ts, KV_ROW), lambda i, pt: (i, 0, 0)),
                  pl.BlockSpec((CMP_BLOCK, KV_ROW), cm(2), pipeline_mode=pl.Buffered(1)),
                  pl.BlockSpec(CMP_W1_SHAPE, cm(3), pipeline_mode=pl.Buffered(1)),
                  pl.BlockSpec((KV_ROW, KV_ROW), cm(2), pipeline_mode=pl.Buffered(1))],
        out_specs=[pl.BlockSpec((1, ncp, LANES), lambda i, pt: (i, 0, 0))] * 2,
        scratch_shapes=[pltpu.VMEM((n_pages, KV_ROW, PAGE_SIZE), F32), pltpu.VMEM((tok_rows, LANES), F32),
                        pltpu.VMEM((tok_rows, LANES), F32), pltpu.SemaphoreType.DMA((n_pages,))],
    )
    return pl.pallas_call(
        functools.partial(_compress_paged_kernel, n_pages=n_pages, m_rows=m_rows, n_cmp=n_cmp),
        grid_spec=grid_spec,
        out_shape=[jax.ShapeDtypeStruct((db, ncp, LANES), BF16)] * 2,
        compiler_params=_params(("arbitrary",)),
        name="compress_paged",
    )(page_table, pool, new_rows, consts['cmp_pe'], consts['cmp_w1'], consts['cmp_w2'])


def _softmax_rows(s, mask):
    s = jnp.where(mask, s, NEG_INF)
    p = jnp.where(mask, jnp.exp(s - jnp.max(s, axis=-1, keepdims=True)), 0.0)
    return p / jnp.maximum(jnp.sum(p, axis=-1, keepdims=True), 1e-30)


def _online_step(carry, s, v):
    m, l, acc = carry
    m_new = jnp.maximum(m, jnp.max(s, axis=-1, keepdims=True))
    alpha = jnp.exp(m - m_new)
    p = jnp.exp(s - m_new)
    l = alpha * l + jnp.sum(p, axis=-1, keepdims=True)
    acc = alpha * acc + _dot(p.astype(BF16), v)
    return m_new, l, acc


def _online_init(rows):
    return (jnp.full((rows, 1), NEG_INF, F32), jnp.zeros((rows, 1), F32), jnp.zeros((rows, LANES), F32))


def _online_finish(carry):
    _, l, acc = carry
    return acc / jnp.maximum(l, 1e-30)


def _block_scores(p_sum, cover_ref, qpos, n_blocks):
    hi, lo = _split_bf16(p_sum)
    imp = _dot(jnp.concatenate([hi, lo], axis=-1), cover_ref[...])
    blk = lax.broadcasted_iota(jnp.int32, imp.shape, 1)
    cur = qpos // SLC_BLOCK
    forced = (blk == 0) | (blk == cur) | (blk == cur - 1)
    score = jnp.where(forced, FORCE_SCORE, jnp.where(blk <= cur, imp, -1.0))
    return jnp.where(blk < n_blocks, score, -2.0)


def _topk_select(score, n_blocks, n_sel):
    lane = lax.broadcasted_iota(jnp.int32, score.shape, 1)
    cnt = jnp.zeros(score.shape, F32)
    for s in range(n_blocks):
        col = score[:, s:s + 1]
        beats = (col > score) | ((col == score) & (lane > s))
        cnt = cnt + jnp.where(beats, 1.0, 0.0)
    return jnp.where((cnt < n_sel) & (lane < n_blocks), 1.0, 0.0)


def _assemble_heads(heads, low64):
    chunks = []
    for k in range(A_HEADS // 2):
        a, b = heads[2 * k], heads[2 * k + 1]
        if k < A_HPG // 2:
            chunks.append(jnp.where(low64, a, pltpu.roll(b, A_HEAD_DIM, 1)))
        else:
            chunks.append(jnp.where(low64, pltpu.roll(a, A_HEAD_DIM, 1), b))
    return chunks


def _topk_rows(score_t, n_blocks, n_sel):
    n_tiles = score_t.shape[0] // SUBLANES
    tiles = [score_t[t * SUBLANES:(t + 1) * SUBLANES] for t in range(n_tiles)]
    sub = lax.broadcasted_iota(jnp.int32, tiles[0].shape, 0)
    cnt = [jnp.zeros(tiles[0].shape, F32) for _ in range(n_tiles)]
    for s in range(n_blocks):
        row = score_t[s:s + 1, :]
        for t in range(n_tiles):
            if t * SUBLANES > s:
                beats = row >= tiles[t]
            elif (t + 1) * SUBLANES - 1 < s:
                beats = row > tiles[t]
            else:
                later = sub + t * SUBLANES > s
                beats = (row > tiles[t]) | ((row == tiles[t]) & later)
            cnt[t] = cnt[t] + jnp.where(beats, 1.0, 0.0)
    blk = lax.broadcasted_iota(jnp.int32, score_t.shape, 0)
    return jnp.where((jnp.concatenate(cnt, axis=0) < n_sel) & (blk < n_blocks), 1.0, 0.0)


def _online_step_t(carry, s, v_t):
    m, l, acc = carry
    m_new = jnp.maximum(m, jnp.max(s, axis=-1, keepdims=True))
    alpha = jnp.exp(m - m_new)
    p = jnp.exp(s - m_new)
    l = alpha * l + jnp.sum(p, axis=-1, keepdims=True)
    acc = alpha * acc + _dot_nt(p.astype(BF16), v_t)
    return m_new, l, acc


def _nsa_prompt_kernel(bound_ref, q_ref, gate_ref, kc_ref, vc_ref, ks_ref, vs_ref, kw_ref, vw_ref, cover_ref,
                       exp_ref, o_ref, *, n_blocks, n_sel):
    i = pl.program_id(1)
    s0 = i * Q_BLOCK
    q = q_ref[0]
    gates = gate_ref[0]
    rows = A_HEADS * Q_BLOCK
    q_all = jnp.concatenate([q[:, h * LANES:(h + 1) * LANES] for h in range(A_HEADS)], axis=0)
    qpos1 = s0 + lax.broadcasted_iota(jnp.int32, (Q_BLOCK, 1), 0)
    qpos_all = jnp.concatenate([qpos1] * A_HEADS, axis=0)

    ncp = kc_ref.shape[1]
    c_end = lax.broadcasted_iota(jnp.int32, (1, ncp), 1) * CMP_STRIDE + (CMP_BLOCK - 1)
    p_c = _softmax_rows(_dot_nt(q_all, kc_ref[0]), c_end <= qpos_all)
    o_c = _dot(p_c.astype(BF16), vc_ref[0])

    cur = (s0 + lax.broadcasted_iota(jnp.int32, (1, Q_BLOCK), 1)) // SLC_BLOCK
    blk = lax.broadcasted_iota(jnp.int32, (LANES, Q_BLOCK), 0)
    forced = (blk == 0) | (blk == cur) | (blk == cur - 1)
    scores = []
    for g in range(A_KV_GROUPS):
        base = g * A_HPG * Q_BLOCK
        p_sum = p_c[base:base + Q_BLOCK]
        for hh in range(1, A_HPG):
            p_sum = p_sum + p_c[base + hh * Q_BLOCK:base + (hh + 1) * Q_BLOCK]
        hi, lo = _split_bf16(p_sum)
        imp_t = _dot(jnp.concatenate([hi, lo], axis=-1), cover_ref[...]).T
        score = jnp.where(forced, FORCE_SCORE, jnp.where(blk <= cur, imp_t, -1.0))
        scores.append(jnp.where(blk < n_blocks, score, -2.0))
    nb8 = _round_up(n_blocks, SUBLANES)
    sel_t = _topk_rows(jnp.concatenate(scores, axis=1)[0:nb8], n_blocks, n_sel)
    if nb8 < LANES:
        sel_t = jnp.concatenate([sel_t, jnp.zeros((LANES - nb8, sel_t.shape[1]), F32)], axis=0)

    aug = []
    for g in range(A_KV_GROUPS):
        sel = sel_t[:, g * Q_BLOCK:(g + 1) * Q_BLOCK].T
        aug += [((sel - 1.0) * MASK_BIG).astype(BF16)] * A_HPG
    q_aug = jnp.concatenate([q_all, jnp.concatenate(aug, axis=0)], axis=1)
    per_tile = SEL_TK // S_TILE
    cat = lambda kt, ref_tile: jnp.concatenate([ref_tile(kt * per_tile + j) for j in range(per_tile)], axis=1)

    n_full = s0 // SEL_TK
    kpos = n_full * SEL_TK + lax.broadcasted_iota(jnp.int32, (1, SEL_TK), 1)
    causal = jnp.concatenate([jnp.where(kpos <= qpos1, 0.0, NEG_INF)] * A_HEADS, axis=0)

    st = jnp.maximum(i - WINDOW // Q_BLOCK, 0) * (Q_BLOCK // W_TILE)
    n_wt = (WINDOW + Q_BLOCK) // W_TILE
    d = qpos1 - (st * W_TILE + lax.broadcasted_iota(jnp.int32, (1, n_wt * W_TILE), 1))
    band = jnp.concatenate([jnp.where((d >= 0) & (d < WINDOW), 0.0, NEG_INF)] * A_HEADS, axis=0)

    def attend(shifted):
        half = rows // A_KV_GROUPS

        def weigh(p, v_t):
            ones = jnp.ones((A_HEAD_DIM, v_t.shape[1]), BF16)
            return jnp.concatenate(
                [_dot_nt(p[g * half:(g + 1) * half],
                         jnp.concatenate([v_t[g * A_HEAD_DIM:(g + 1) * A_HEAD_DIM], ones], axis=0))
                 for g in range(A_KV_GROUPS)], axis=0)

        def normalise(acc):
            swapped = pltpu.roll(acc, A_HEAD_DIM, 1)
            lane = lax.broadcasted_iota(jnp.int32, (half, LANES), 1)
            g0 = acc[:half] / jnp.maximum(swapped[:half], 1e-30)
            g1 = swapped[half:] / jnp.maximum(acc[half:], 1e-30)
            return jnp.concatenate([jnp.where(lane < A_HEAD_DIM, g0, 0.0), jnp.where(lane < A_HEAD_DIM, 0.0, g1)],
                                   axis=0)

        def step(state, s, v_t):
            if shifted:
                return _online_step_t(state, s, v_t)
            return state + weigh(jnp.exp(s).astype(BF16), v_t)

        def sel_step(kt, state, bias):
            k_aug = jnp.concatenate([cat(kt, lambda n: ks_ref[0, n]), cat(kt, lambda n: exp_ref[n])], axis=0)
            s = _dot(q_aug, k_aug)
            return step(state, s if bias is None else s + bias, cat(kt, lambda n: vs_ref[0, n]))

        init = _online_init(rows)
        state = lax.fori_loop(0, n_full, lambda kt, carry: sel_step(kt, carry, None), init if shifted else init[2])
        state = sel_step(n_full, state, causal)
        o_sel = _online_finish(state) if shifted else normalise(state)

        kw_t = jnp.concatenate([kw_ref[0, st + j] for j in range(n_wt)], axis=1)
        vw_t = jnp.concatenate([vw_ref[0, st + j] for j in range(n_wt)], axis=1)
        s_w = _dot(q_all, kw_t) + band
        if shifted:
            p_w = jnp.exp(s_w - jnp.max(s_w, axis=-1, keepdims=True))
            o_win = _dot_nt(p_w.astype(BF16), vw_t) / jnp.maximum(jnp.sum(p_w, axis=-1, keepdims=True), 1e-30)
        else:
            o_win = normalise(weigh(jnp.exp(s_w).astype(BF16), vw_t))
        return o_sel, o_win

    o_s, o_w = lax.cond(bound_ref[0] <= SCORE_SAFE, lambda: attend(False), lambda: attend(True))

    heads = []
    for h in range(A_HEADS):
        r = slice(h * Q_BLOCK, (h + 1) * Q_BLOCK)
        heads.append(gates[:, 3 * h:3 * h + 1] * o_c[r] + gates[:, 3 * h + 1:3 * h + 2] * o_s[r]
                     + gates[:, 3 * h + 2:3 * h + 3] * o_w[r])
    low64 = lax.broadcasted_iota(jnp.int32, (Q_BLOCK, LANES), 1) < A_HEAD_DIM
    for k, chunk in enumerate(_assemble_heads(heads, low64)):
        o_ref[0, :, k * LANES:(k + 1) * LANES] = chunk.astype(BF16)


def _cover_matrix(n_cmp, ncp, n_blocks, nsp):
    c = np.arange(ncp)[:, None]
    s = np.arange(nsp)[None, :]
    cover = ((c * CMP_STRIDE < s * SLC_BLOCK + SLC_BLOCK) & (c * CMP_STRIDE + CMP_BLOCK > s * SLC_BLOCK)
             & (c < n_cmp) & (s < n_blocks))
    return jnp.asarray(np.concatenate([cover, cover], axis=0), BF16)


def _expand_matrix(n_keys, block0=0):
    e = (np.arange(n_keys)[None, :] // SLC_BLOCK) == (block0 + np.arange(LANES)[:, None])
    return e


def _expand_tiles(n_keys, tk):
    e = _expand_matrix(n_keys).reshape(LANES, n_keys // tk, tk)
    return jnp.asarray(np.transpose(e, (1, 0, 2)), BF16)


def _nsa_prompt(score_bound, q, gates, kc, vc, ks_bf, vs_bf, kw_bf, vw_bf, t):
    b = q.shape[0]
    ncp = kc.shape[1]
    n_cmp = t // CMP_STRIDE - CMP_BLOCK // CMP_STRIDE + 1
    n_blocks = t // SLC_BLOCK
    assert n_blocks <= LANES and t % SEL_TK == 0 and t >= WINDOW + Q_BLOCK
    cover = _cover_matrix(n_cmp, ncp, n_blocks, LANES)
    expand = _expand_tiles(t, S_TILE)
    per_b = lambda rows, w: pl.BlockSpec((1, rows, w), lambda bi, i: (bi, 0, 0))
    tiles = lambda a: pl.BlockSpec((1,) + a.shape[1:], lambda bi, i: (bi, 0, 0, 0))
    return pl.pallas_call(
        functools.partial(_nsa_prompt_kernel, n_blocks=n_blocks, n_sel=min(N_SELECT, n_blocks)),
        grid=(b, t // Q_BLOCK),
        in_specs=[pl.BlockSpec(memory_space=pltpu.SMEM),
                  pl.BlockSpec((1, Q_BLOCK, A_HEADS * LANES), lambda bi, i: (bi, i, 0)),
                  pl.BlockSpec((1, Q_BLOCK, LANES), lambda bi, i: (bi, i, 0)),
                  per_b(ncp, LANES), per_b(ncp, LANES),
                  tiles(ks_bf), tiles(vs_bf), tiles(kw_bf), tiles(vw_bf),
                  _const_spec(cover.shape), _const_spec(expand.shape)],
        out_specs=pl.BlockSpec((1, Q_BLOCK, A_HEADS * A_HEAD_DIM), lambda bi, i: (bi, i, 0)),
        out_shape=jax.ShapeDtypeStruct((b, t, A_HEADS * A_HEAD_DIM), BF16),
        compiler_params=_params(("parallel", "parallel")),
        name="nsa_prompt",
    )(score_bound, q, gates, kc, vc, ks_bf, vs_bf, kw_bf, vw_bf, cover, expand)


def _key_page_copy(pool_ref, page, bufs, half, slot, sems):
    dst = bufs.at[half, :, pl.ds(pl.multiple_of(slot * PAGE_SIZE, PAGE_SIZE), PAGE_SIZE)]
    return pltpu.make_async_copy(pool_ref.at[page], dst, sems.at[half])


def _nsa_sample_kernel(pt_ref, q_ref, gate_ref, kc_ref, vc_ref, pool_ref, new_ref, win_ref, wnew_ref,
                       cover_ref, exp_ref, o_ref, bufs, sems, *, n_pages, n_blocks, n_sel, key_chunk):
    b = pl.program_id(0)
    half = b % 2

    def start_row(row, into):
        def start(p, carry):
            _key_page_copy(pool_ref, pt_ref[row, p], bufs, into, p, sems).start()
            return carry
        lax.fori_loop(0, n_pages, start, 0)

    @pl.when(b == 0)
    def _():
        start_row(0, 0)

    @pl.when(b + 1 < pl.num_programs(0))
    def _():
        start_row(b + 1, 1 - half)

    buf = bufs.at[half]
    past = n_pages * PAGE_SIZE
    ts = q_ref.shape[1]
    buf[:, past:past + LANES] = new_ref[0]

    q = q_ref[0].astype(F32)
    q_all = jnp.concatenate([q[:, h * LANES:(h + 1) * LANES] for h in range(A_HEADS)], axis=0).astype(BF16)
    rows = A_HEADS * ts
    qpos1 = past + lax.broadcasted_iota(jnp.int32, (ts, 1), 0)
    qpos_all = jnp.concatenate([qpos1] * A_HEADS, axis=0)

    ncp = kc_ref.shape[1]
    c_end = lax.broadcasted_iota(jnp.int32, (1, ncp), 1) * CMP_STRIDE + (CMP_BLOCK - 1)
    p_c = _softmax_rows(_dot_nt(q_all, kc_ref[0]), c_end <= qpos_all)
    o_c = _dot(p_c.astype(BF16), vc_ref[0])

    p_groups = []
    for g in range(A_KV_GROUPS):
        base = g * A_HPG * ts
        p_sum = p_c[base:base + ts]
        for hh in range(1, A_HPG):
            p_sum = p_sum + p_c[base + hh * ts:base + (hh + 1) * ts]
        p_groups.append(p_sum)
    qpos_g = jnp.concatenate([qpos1] * A_KV_GROUPS, axis=0)
    score = _block_scores(jnp.concatenate(p_groups, axis=0), cover_ref, qpos_g, n_blocks)
    sel = _topk_select(score, n_blocks, n_sel).astype(BF16)

    def wait(p, carry):
        _key_page_copy(pool_ref, 0, bufs, half, p, sems).wait()
        return carry

    lax.fori_loop(0, n_pages, wait, 0)

    n_keys = buf.shape[1]
    carry = _online_init(rows)
    for ck in range(-(-n_keys // key_chunk)):
        k0 = ck * key_chunk
        kn = min(key_chunk, n_keys - k0)
        s = _dot(q_all, buf[0:LANES, k0:k0 + kn].astype(BF16))
        blk0 = ck * (key_chunk // SLC_BLOCK)
        em = _dot(sel[:, blk0:blk0 + LANES], exp_ref[:, 0:kn])
        kpos = k0 + lax.broadcasted_iota(jnp.int32, (1, kn), 1)
        bias = jnp.where((em > 0.5) & (kpos <= qpos_g), 0.0, NEG_INF)
        biases = []
        for g in range(A_KV_GROUPS):
            biases += [bias[g * ts:(g + 1) * ts]] * A_HPG
        carry = _online_step_t(carry, s + jnp.concatenate(biases, axis=0),
                               buf[LANES:KV_ROW, k0:k0 + kn].astype(BF16))
    o_s = _online_finish(carry)

    wb = win_ref.shape[2]
    kv_w = jnp.concatenate([win_ref[0], wnew_ref[0]], axis=1).astype(BF16)
    idx = lax.broadcasted_iota(jnp.int32, (1, wb + LANES), 1)
    w_pos = jnp.where(idx < wb, past - wb + idx, past + idx - wb)
    d = qpos_all - w_pos
    m_w = (d >= 0) & (d < WINDOW) & (w_pos >= 0)
    p_w = _softmax_rows(_dot(q_all, kv_w[0:LANES]), m_w)
    o_w = _dot_nt(p_w.astype(BF16), kv_w[LANES:KV_ROW])

    gates = gate_ref[0]
    heads = []
    for h in range(A_HEADS):
        r = slice(h * ts, (h + 1) * ts)
        heads.append(gates[:, 3 * h:3 * h + 1] * o_c[r] + gates[:, 3 * h + 1:3 * h + 2] * o_s[r]
                     + gates[:, 3 * h + 2:3 * h + 3] * o_w[r])
    low64 = lax.broadcasted_iota(jnp.int32, (ts, LANES), 1) < A_HEAD_DIM
    for k, chunk in enumerate(_assemble_heads(heads, low64)):
        o_ref[0, :, k * LANES:(k + 1) * LANES] = chunk.astype(BF16)


def _nsa_sample(page_table, q, gates, kc, vc, pool, new_tile, win_t, wnew_tile):
    db, n_pages = page_table.shape
    wb = win_t.shape[2]
    assert wb % LANES == 0
    ts = q.shape[1]
    past = n_pages * PAGE_SIZE
    ncp = kc.shape[1]
    t_pad = _round_up(past + ts, SLC_BLOCK)
    n_cmp = t_pad // CMP_STRIDE - CMP_BLOCK // CMP_STRIDE + 1
    n_blocks = t_pad // SLC_BLOCK
    key_chunk = LANES * SLC_BLOCK
    n_keys = past + LANES
    nsp = LANES * (-(-n_keys // key_chunk))
    assert nsp >= n_blocks and ts % SUBLANES == 0
    cover = _cover_matrix(n_cmp, ncp, n_blocks, nsp)
    expand = jnp.asarray(_expand_matrix(min(key_chunk, n_keys)), BF16)
    cm = lambda nd: (lambda i, pt: (0,) * nd)
    per_b = lambda r, w: pl.BlockSpec((1, r, w), lambda i, pt: (i, 0, 0))
    grid_spec = pltpu.PrefetchScalarGridSpec(
        num_scalar_prefetch=1,
        grid=(db,),
        in_specs=[per_b(ts, A_HEADS * LANES), per_b(ts, LANES), per_b(ncp, LANES), per_b(ncp, LANES),
                  pl.BlockSpec(memory_space=pl.ANY), per_b(KV_ROW, LANES), per_b(KV_ROW, wb), per_b(KV_ROW, LANES),
                  pl.BlockSpec(cover.shape, cm(2), pipeline_mode=pl.Buffered(1)),
                  pl.BlockSpec(expand.shape, cm(2), pipeline_mode=pl.Buffered(1))],
        out_specs=per_b(ts, A_HEADS * A_HEAD_DIM),
        scratch_shapes=[pltpu.VMEM((2, KV_ROW, n_keys), F32), pltpu.SemaphoreType.DMA((2,))],
    )
    return pl.pallas_call(
        functools.partial(_nsa_sample_kernel, n_pages=n_pages, n_blocks=n_blocks,
                          n_sel=min(N_SELECT, n_blocks), key_chunk=key_chunk),
        grid_spec=grid_spec,
        out_shape=jax.ShapeDtypeStruct((db, ts, A_HEADS * A_HEAD_DIM), BF16),
        compiler_params=_params(("arbitrary",)),
        name="nsa_sample",
    )(page_table, q, gates, kc, vc, pool, new_tile, win_t, wnew_tile, cover, expand)


def _hgrn_matrices():
    c = HGRN_CHUNK
    t = np.arange(c)[:, None]
    u = np.arange(c)[None, :]
    mats = [u <= t]
    masks = [t == u]
    for lvl in range(HGRN_LEVELS):
        m = 1 << lvl
        mid = (t // (2 * m)) * (2 * m) + m - 1
        mats.append((u > mid) & (u <= t))
        mats.append((u > t) & (u <= mid))
        masks.append((t // (2 * m) == u // (2 * m)) & (t % (2 * m) >= m) & (u % (2 * m) < m))
    mats.append(u > t)
    pm = np.concatenate(mats, axis=0)
    return (jnp.asarray(np.concatenate([pm, pm], axis=1), BF16),
            jnp.asarray(np.stack(masks).astype(np.float32)))


def _hgrn_kernel(qb_ref, kb_ref, vb_ref, lf_ref, gb_ref, s0_ref, gn_ref, pm_ref, lm_ref,
                 ob_ref, sout_ref, st_scr, pad_scr):
    j = pl.program_id(1)
    c = HGRN_CHUNK
    n_rows, t_blk = qb_ref.shape[0], qb_ref.shape[1]

    @pl.when(j == 0)
    def _():
        for r in range(n_rows):
            for h in range(B_HEADS):
                st_scr[r * B_HEADS + h] = s0_ref[r, h].T

    def load(ref, r, slot, rows):
        if t_blk >= c:
            return ref[r, rows, :]
        pad_scr[4 * r + slot] = jnp.zeros((c, B_HEADS * B_KEY_DIM), F32)
        pad_scr[4 * r + slot, 0:t_blk, :] = ref[r]
        return pad_scr[4 * r + slot]

    gn = gn_ref[...]
    for r in range(n_rows):
        states = [st_scr[r * B_HEADS + h] for h in range(B_HEADS)]
        for sub in range(max(t_blk // c, 1)):
            rows = slice(sub * c, (sub + 1) * c)
            out_rows = rows if t_blk >= c else slice(0, t_blk)
            qb, kb, vb, lf = (load(ref, r, slot, rows) for slot, ref in enumerate((qb_ref, kb_ref, vb_ref, lf_ref)))
            lf_hi, lf_lo = _split_bf16(lf)
            ex = _dot(pm_ref[...], jnp.concatenate([lf_hi, lf_lo], axis=0))
            for h in range(B_HEADS):
                sl = slice(h * B_KEY_DIM, (h + 1) * B_KEY_DIM)
                q, k = qb[:, sl], kb[:, sl]
                v = vb[:, sl].astype(BF16)
                b_cum = ex[0:c, sl]
                a = lm_ref[0] * _dot_nt(q.astype(BF16), k.astype(BF16))
                for lvl in range(HGRN_LEVELS):
                    eq = ex[(2 * lvl + 1) * c:(2 * lvl + 2) * c, sl]
                    ek = ex[(2 * lvl + 2) * c:(2 * lvl + 3) * c, sl]
                    a = a + lm_ref[lvl + 1] * _dot_nt((q * jnp.exp(eq)).astype(BF16),
                                                      (k * jnp.exp(ek)).astype(BF16))
                st = states[h]
                o = _dot_nt((q * jnp.exp(b_cum)).astype(BF16), st.astype(BF16)) + _dot(a.astype(BF16), v)
                e_end = ex[(2 * HGRN_LEVELS + 1) * c:(2 * HGRN_LEVELS + 2) * c, sl]
                states[h] = st * jnp.exp(b_cum[c - 1:c, :]) + _dot_tn(v, (k * jnp.exp(e_end)).astype(BF16))
                y = o * lax.rsqrt(jnp.mean(o * o, axis=-1, keepdims=True) + EPS) * gn
                ob_ref[r, out_rows, sl] = y[0:min(t_blk, c)] * gb_ref[r, out_rows, sl]
        for h in range(B_HEADS):
            st_scr[r * B_HEADS + h] = states[h]

    @pl.when(j == pl.num_programs(1) - 1)
    def _():
        for r in range(n_rows):
            for h in range(B_HEADS):
                sout_ref[r, h] = st_scr[r * B_HEADS + h].T


def _hgrn(qb, kb, vb, lf, gb, s0, consts):
    b, t, w = qb.shape
    t_blk = min(t, HGRN_STEP_CHUNKS * HGRN_CHUNK)
    assert t % t_blk == 0 and (t_blk % HGRN_CHUNK == 0 or t == t_blk < HGRN_CHUNK)
    n_rows = HGRN_SHORT_ROWS if (t < HGRN_CHUNK and b % HGRN_SHORT_ROWS == 0) else 1
    tok = pl.BlockSpec((n_rows, t_blk, w), lambda bi, j: (bi, j, 0))
    st = pl.BlockSpec((n_rows, B_HEADS, B_KEY_DIM, B_KEY_DIM), lambda bi, j: (bi, 0, 0, 0))
    return pl.pallas_call(
        _hgrn_kernel,
        grid=(b // n_rows, t // t_blk),
        in_specs=[tok, tok, tok, tok, tok, st, _const_spec((1, B_KEY_DIM)),
                  _const_spec(consts['hgrn_pm'].shape), _const_spec(consts['hgrn_lm'].shape)],
        out_specs=[tok, st],
        out_shape=[jax.ShapeDtypeStruct((b, t, w), F32),
                   jax.ShapeDtypeStruct((b, B_HEADS, B_KEY_DIM, B_KEY_DIM), F32)],
        scratch_shapes=[pltpu.VMEM((n_rows * B_HEADS, B_KEY_DIM, B_KEY_DIM), F32),
                        pltpu.VMEM((4 * n_rows, HGRN_CHUNK, w), F32)],
        compiler_params=_params(("parallel", "arbitrary")),
        name="hgrn",
    )(qb, kb, vb, lf, gb, s0, consts['hgrn_g'], consts['hgrn_pm'], consts['hgrn_lm'])


def _merge_ffn_kernel(x_ref, oa_ref, ob_ref, mg_ref, p1_ref, p2_ref, wa_ref, wb_ref, wo_ref, fg_ref, win_ref,
                      cw_ref, cb_ref, wout_ref, y_ref, a_ref, carry_scr, *, seq_len):
    tm = x_ref.shape[0]
    mg = mg_ref[...]
    m = (mg[:, :D_MODEL] * _dot(oa_ref[...], wa_ref[...])
         + mg[:, D_MODEL:] * _dot(ob_ref[...].astype(BF16), wb_ref[...]))
    x2 = x_ref[...] + _dot(m.astype(BF16), wo_ref[...])
    h = (x2 * lax.rsqrt(jnp.mean(x2 * x2, axis=-1, keepdims=True) + EPS) * fg_ref[...]).astype(BF16)
    row = lax.broadcasted_iota(jnp.int32, (tm, 1), 0)
    carried = seq_len >= tm
    if carried:
        j = pl.program_id(0) % (seq_len // tm)

        @pl.when(j == 0)
        def _():
            carry_scr[0:2, :] = p1_ref[0]
    else:
        t = row % seq_len

    y = x2
    for lo in range(0, D_FF, FFN_CHUNK):
        cols = slice(lo, lo + FFN_CHUNK)
        a = _dot(h, win_ref[:, cols])
        gate = _dot(h, win_ref[:, D_FF + lo:D_FF + lo + FFN_CHUNK])
        if carried:
            prev = carry_scr[:, cols]
            a1 = jnp.where(row == 0, prev[1:2], pltpu.roll(a, 1, 0))
            a2 = jnp.where(row == 0, prev[0:1], jnp.where(row == 1, prev[1:2], pltpu.roll(a, 2, 0)))
            carry_scr[0:2, cols] = a[tm - 2:tm]
            a_ref[0, :, cols] = a[tm - 2:tm]
        else:
            a1 = jnp.where(t == 0, p1_ref[:, cols], pltpu.roll(a, 1, 0))
            a2 = jnp.where(t < 2, p2_ref[:, cols], pltpu.roll(a, 2, 0))
            a_ref[:, cols] = a
        a_conv = cb_ref[:, cols] + a2 * cw_ref[0:1, cols] + a1 * cw_ref[1:2, cols] + a * cw_ref[2:3, cols]
        act = a_conv * jax.nn.sigmoid(a_conv) * gate
        y = y + _dot(act.astype(BF16), wout_ref[cols, :])
    y_ref[...] = y


def _merge_ffn(x2d, oa, ob, mg, conv_state, seq_len, consts, tm):
    n = x2d.shape[0]
    b = n // seq_len
    row = lambda w: pl.BlockSpec((tm, w), lambda i: (i, 0))
    if seq_len >= tm:
        assert seq_len % tm == 0
        per_seq = seq_len // tm
        p1, p2 = conv_state, conv_state
        p_spec = pl.BlockSpec((1, FFN_CONV - 1, D_FF), lambda i: (i // per_seq, 0, 0))
        a_shape = jax.ShapeDtypeStruct((b, FFN_CONV - 1, D_FF), F32)
        a_spec = pl.BlockSpec((1, FFN_CONV - 1, D_FF), lambda i: (i // per_seq, 0, 0))
    else:
        assert tm % seq_len == 0 and seq_len >= FFN_CONV - 1
        zeros = jnp.zeros((b, seq_len - 1, D_FF), F32)
        p1 = jnp.concatenate([conv_state[:, 1:2], zeros], axis=1).reshape(n, D_FF)
        p2 = jnp.concatenate([conv_state, zeros[:, 1:]], axis=1).reshape(n, D_FF)
        p_spec = row(D_FF)
        a_shape = jax.ShapeDtypeStruct((n, D_FF), F32)
        a_spec = row(D_FF)
    y, a_out = pl.pallas_call(
        functools.partial(_merge_ffn_kernel, seq_len=seq_len),
        grid=(n // tm,),
        in_specs=[row(D_MODEL), row(A_HEADS * A_HEAD_DIM), row(B_HEADS * B_KEY_DIM), row(2 * D_MODEL),
                  p_spec, p_spec,
                  _const_spec((A_HEADS * A_HEAD_DIM, D_MODEL)), _const_spec((B_HEADS * B_KEY_DIM, D_MODEL)),
                  _const_spec((D_MODEL, D_MODEL)), _const_spec((1, D_MODEL)),
                  _const_spec((D_MODEL, 2 * D_FF)), _const_spec((FFN_CONV, D_FF)), _const_spec((1, D_FF)),
                  _const_spec((D_FF, D_MODEL))],
        out_specs=[row(D_MODEL), a_spec],
        out_shape=[jax.ShapeDtypeStruct((n, D_MODEL), F32), a_shape],
        scratch_shapes=[pltpu.VMEM((SUBLANES, D_FF), F32)],
        compiler_params=_params(("arbitrary",)),
        name="merge_ffn",
    )(x2d, oa, ob, mg, p1, p2, consts['w_a'], consts['w_b'], consts['w_out'], consts['ffn_g'],
      consts['ffn_w_in'], consts['conv_w'], consts['conv_b'], consts['ffn_w_out'])
    if seq_len >= tm:
        return y, a_out
    return y, a_out.reshape(b, seq_len, D_FF)[:, seq_len - (FFN_CONV - 1):]


def _prepare_consts(attn_norm_g, w_in, q_norm_g, k_norm_g, cmp_pos_emb, cmp_w1, cmp_w2, hgrn_lb_logits,
                    hgrn_norm_g, w_branch, w_out, ffn_norm_g, ffn_w_in, ffn_conv_w, ffn_conv_b, ffn_w_out):
    n_q = A_HEADS * A_HEAD_DIM
    gate_lo = n_q + 3 * KV_ROW
    gate_hi = gate_lo + 3 * A_HEADS
    w_gate = jnp.pad(w_in[:, gate_lo:gate_hi], ((0, 0), (0, LANES - 3 * A_HEADS))).astype(BF16)
    seg = np.arange(LANES) // A_HEAD_DIM
    mseg = (seg[:, None] == seg[None, :]).astype(np.float32) / A_HEAD_DIM
    eye = jnp.eye(2 * A_KV_GROUPS, dtype=F32)
    jsel = np.repeat(np.arange(2), A_KV_GROUPS)
    w1 = cmp_w1.reshape(2, CMP_BLOCK, A_HEAD_DIM, A_HEAD_DIM)[jsel]
    w1_bd = jnp.einsum('ab,apde->padbe', eye, w1).reshape(CMP_W1_SHAPE).astype(BF16)
    w2_bd = jnp.einsum('ab,ade->adbe', eye, cmp_w2[jsel]).reshape(KV_ROW, KV_ROW).astype(BF16)
    pe = jnp.transpose(cmp_pos_emb[jsel], (1, 0, 2)).reshape(CMP_BLOCK, KV_ROW)
    pm, lm = _hgrn_matrices()
    return {
        'attn_g': attn_norm_g.reshape(1, D_MODEL), 'w_attn': w_in[:, :gate_lo].astype(BF16),
        'w_rest': w_in[:, gate_hi:].astype(BF16), 'w_gate': w_gate,
        'q_g': jnp.tile(q_norm_g, 2).reshape(1, LANES), 'k_g': jnp.tile(k_norm_g, (1, 2)),
        'lbl': hgrn_lb_logits.astype(F32), 'mseg': jnp.asarray(np.concatenate([mseg, mseg], axis=0), BF16),
        'cmp_pe': pe, 'cmp_w1': w1_bd, 'cmp_w2': w2_bd,
        'hgrn_g': hgrn_norm_g.reshape(1, B_KEY_DIM), 'hgrn_pm': pm, 'hgrn_lm': lm,
        'w_a': w_branch[:n_q].astype(BF16), 'w_b': w_branch[n_q:].astype(BF16), 'w_out': w_out.astype(BF16),
        'ffn_g': ffn_norm_g.reshape(1, D_MODEL), 'ffn_w_in': ffn_w_in.astype(BF16),
        'conv_w': ffn_conv_w, 'conv_b': ffn_conv_b.reshape(1, D_FF), 'ffn_w_out': ffn_w_out.astype(BF16),
    }


def _rope_tables(pos, reps):
    half = A_HEAD_DIM // 2
    inv = ROPE_THETA ** (-jnp.arange(half, dtype=F32) / half)
    ang = pos.astype(F32)[:, None] * inv[None, :]
    cos, sin = jnp.cos(ang), jnp.sin(ang)
    cos_t = jnp.tile(cos, (reps, LANES // half))
    sin_t = jnp.tile(jnp.concatenate([-sin, sin], axis=-1), (reps, LANES // A_HEAD_DIM))
    return cos_t, sin_t


def kernel(x_prompt, x_sample, cache_cmp_kv, cache_slc_kv, page_table, state_win_kv, state_hgrn, state_ffn_conv, attn_norm_g, w_in, q_norm_g, k_norm_g, cmp_pos_emb, cmp_w1, cmp_w2, hgrn_lb_logits, hgrn_norm_g, w_branch, w_out, ffn_norm_g, ffn_w_in, ffn_conv_w, ffn_conv_b, ffn_w_out):
    assert w_in.shape[0] == 1, "single-layer step"
    b, t, _ = x_prompt.shape
    db, ts, _ = x_sample.shape
    n_pool = cache_cmp_kv.shape[1]
    past = page_table.shape[1] * PAGE_SIZE
    wb = state_win_kv.shape[2]
    assert t % SLC_BLOCK == 0 and t % Q_BLOCK == 0
    consts = _prepare_consts(attn_norm_g[0], w_in[0], q_norm_g[0], k_norm_g[0], cmp_pos_emb[0], cmp_w1[0],
                             cmp_w2[0], hgrn_lb_logits, hgrn_norm_g[0], w_branch[0], w_out[0], ffn_norm_g[0],
                             ffn_w_in[0], ffn_conv_w[0], ffn_conv_b[0], ffn_w_out[0])
    kv_shape = (2, A_KV_GROUPS, A_HEAD_DIM)
    tm_p = min(256, t)
    n_s = db * ts

    fp = _inproj(x_prompt.reshape(b * t, D_MODEL), *_rope_tables(jnp.arange(t, dtype=jnp.int32), 1), consts, tm_p,
                 seq_len=t)
    seq = lambda a: a.reshape(b, t, a.shape[-1])
    kc_p, vc_p = _compress_dense(seq(fp['kvc']), consts)
    score_bound = (1.01 * A_HEAD_DIM ** 0.5 * jnp.max(jnp.abs(q_norm_g[0]))
                   * jnp.max(jnp.abs(k_norm_g[0, 1:]))).reshape(1).astype(F32)
    oa_p = _nsa_prompt(score_bound, seq(fp['q']), seq(fp['gate']), kc_p, vc_p, fp['ks_bf'], fp['vs_bf'],
                       fp['kw_bf'], fp['vw_bf'], t)
    ob_p, s_p = _hgrn(seq(fp['qb']), seq(fp['kb']), seq(fp['vb']), seq(fp['lf']), seq(fp['gb']),
                      jnp.zeros((b, B_HEADS, B_KEY_DIM, B_KEY_DIM), F32), consts)
    y_p, conv_p = _merge_ffn(x_prompt.reshape(b * t, D_MODEL), oa_p.reshape(b * t, -1), ob_p.reshape(b * t, -1),
                             fp['mg'], jnp.zeros((b, FFN_CONV - 1, D_FF), F32), t, consts, min(512, t))

    fs = _inproj(x_sample.reshape(n_s, D_MODEL), *_rope_tables(past + jnp.arange(ts, dtype=jnp.int32), db),
                 consts, n_s)
    sseq = lambda a: a.reshape(db, ts, a.shape[-1])
    feat = lambda c: jnp.transpose(c, (0, 2, 3, 4, 1)).reshape(c.shape[0], KV_ROW, c.shape[1])
    new_tile = lambda rows: jnp.pad(jnp.transpose(sseq(rows), (0, 2, 1)), ((0, 0), (0, 0), (0, LANES - ts)))
    kc_s, vc_s = _compress_paged(page_table, feat(cache_cmp_kv[0]), sseq(fs['kvc']), consts)
    oa_s = _nsa_sample(page_table, sseq(fs['q']), sseq(fs['gate']), kc_s, vc_s, feat(cache_slc_kv[0]),
                       new_tile(fs['kvs']), feat(state_win_kv[0]), new_tile(fs['kvw']))
    win_cat = jnp.concatenate([state_win_kv[0].reshape(db, wb, KV_ROW), sseq(fs['kvw'])], axis=1)
    ob_s, s_s = _hgrn(sseq(fs['qb']), sseq(fs['kb']), sseq(fs['vb']), sseq(fs['lf']), sseq(fs['gb']),
                      state_hgrn[0].astype(F32), consts)
    y_s, conv_s = _merge_ffn(x_sample.reshape(n_s, D_MODEL), oa_s.reshape(n_s, -1), ob_s.reshape(n_s, -1),
                             fs['mg'], state_ffn_conv[0], ts, consts, n_s)

    wkeep = min(WINDOW, t)
    unfeat = lambda a: jnp.transpose(a.reshape(b, *kv_shape, a.shape[-1]), (0, 4, 1, 2, 3))[None]
    return (y_p.reshape(b, t, D_MODEL), y_s.reshape(db, ts, D_MODEL),
            unfeat(fp['kvc_t']), fs['kvc'].reshape(1, db, ts, *kv_shape),
            unfeat(fp['kvs_t']), fs['kvs'].reshape(1, db, ts, *kv_shape),
            unfeat(fp['kvw_t'][:, :, t - wkeep:]),
            win_cat[:, ts:].reshape(1, db, wb, *kv_shape),
            s_p[None], s_s[None], conv_p[None], conv_s[None])
```

```python
import functools

import numpy as np
import jax
import jax.numpy as jnp
from jax import lax
from jax.experimental import pallas as pl
from jax.experimental.pallas import tpu as pltpu

F32 = jnp.float32
BF16 = jnp.bfloat16

D_MODEL = 1024
PAGE_SIZE = 128
A_HEADS = 8
A_KV_GROUPS = 2
A_HPG = A_HEADS // A_KV_GROUPS
A_HEAD_DIM = 64
CMP_BLOCK = 32
CMP_STRIDE = 16
SLC_BLOCK = 64
N_SELECT = 16
WINDOW = 512
Q_BLOCK = 128
ROPE_THETA = 10000.0
FORCE_SCORE = 1e4
NEG_INF = -1e30
B_HEADS = 4
B_KEY_DIM = 128
D_FF = 2816
FFN_CONV = 3
EPS = 1e-6

LANES = 128
SUBLANES = 8
KV_ROW = 2 * A_KV_GROUPS * A_HEAD_DIM
S_TILE = 256
W_TILE = 128
SEL_TK = 512
MASK_BIG = 1e30
SCORE_SAFE = 40.0
CHUNK_PITCH = 24
LAND_GROUP = 8
CMP_GROUP = 4
CMP_W1_SHAPE = (CMP_BLOCK // CMP_GROUP, CMP_GROUP * KV_ROW, KV_ROW)
FFN_CHUNK = 1408
HGRN_CHUNK = 128
HGRN_LEVELS = 7
HGRN_STEP_CHUNKS = 4
HGRN_SHORT_ROWS = 4
VMEM_LIMIT = 56 * 1024 * 1024

OFF_Q, OFF_KVC, OFF_KVS, OFF_KVW = 0, 512, 768, 1024
OFF_QB, OFF_FB, OFF_IB, OFF_GB, OFF_MG, OFF_GATE = 1280, 1792, 2304, 2816, 3328, 5376
W_PACK = 5504


def _dot(a, b):
    return jnp.dot(a, b, preferred_element_type=F32)


def _dot_nt(a, b):
    return lax.dot_general(a, b, (((1,), (1,)), ((), ())), preferred_element_type=F32)


def _dot_tn(a, b):
    return lax.dot_general(a, b, (((0,), (0,)), ((), ())), preferred_element_type=F32)


def _split_bf16(x):
    hi = x.astype(BF16)
    lo = (x - hi.astype(F32)).astype(BF16)
    return hi, lo


def _round_up(n, m):
    return -(-n // m) * m


def _const_spec(shape):
    nd = len(shape)
    return pl.BlockSpec(shape, lambda *_: (0,) * nd, pipeline_mode=pl.Buffered(1))


def _params(semantics):
    return pltpu.CompilerParams(dimension_semantics=semantics, vmem_limit_bytes=VMEM_LIMIT)


INPROJ_COMMON = ('q', 'gate', 'qb', 'kb', 'vb', 'lf', 'gb', 'mg')
INPROJ_PROMPT = INPROJ_COMMON + ('kvc', 'kvc_t', 'kvs_t', 'kvw_t', 'ks_bf', 'vs_bf', 'kw_bf', 'vw_bf')
INPROJ_SAMPLE = INPROJ_COMMON + ('kvc', 'kvs', 'kvw')


def _inproj_kernel(x_ref, g_ref, wa_ref, wb_ref, wg_ref, cos_ref, sin_ref, qg_ref, kg_ref, lbl_ref, mseg_ref,
                   *out_refs, names):
    o = dict(zip(names, out_refs))
    q_ref, gate_ref, mg_ref = o['q'], o['gate'], o['mg']
    qb_ref, kb_ref, vb_ref, lf_ref, gb_ref = o['qb'], o['kb'], o['vb'], o['lf'], o['gb']
    x = x_ref[...]
    ms = jnp.mean(x * x, axis=-1, keepdims=True)
    h = (x * lax.rsqrt(ms + EPS) * g_ref[...]).astype(BF16)
    cos = cos_ref[...]
    sin = sin_ref[...]
    tm = x.shape[0]
    lane = lax.broadcasted_iota(jnp.int32, (tm, LANES), 1)
    first_half = (lane & (A_HEAD_DIM // 2)) == 0
    low64 = lane < A_HEAD_DIM
    mseg = mseg_ref[...]

    def proj(lo, width):
        for start, ref in ((OFF_GATE, wg_ref), (OFF_QB, wb_ref), (OFF_Q, wa_ref)):
            if lo >= start:
                return _dot(h, ref[:, lo - start:lo - start + width])

    def head_norm_rope(chunk, gain):
        s_hi, s_lo = _split_bf16(chunk * chunk)
        mean = _dot(jnp.concatenate([s_hi, s_lo], axis=-1), mseg)
        y = chunk * lax.rsqrt(mean + EPS) * gain
        rot = jnp.where(first_half, pltpu.roll(y, LANES - A_HEAD_DIM // 2, 1),
                        pltpu.roll(y, A_HEAD_DIM // 2, 1))
        return y * cos + rot * sin

    zq = proj(OFF_Q, A_HEADS * A_HEAD_DIM)
    z_kv = [proj(off, KV_ROW) for off in (OFF_KVC, OFF_KVS, OFF_KVW)]
    qg = qg_ref[...]

    def emit_q(k):
        c = head_norm_rope(zq[:, k * LANES:(k + 1) * LANES], qg) * (A_HEAD_DIM ** -0.5)
        r = pltpu.roll(c, A_HEAD_DIM, 1)
        if k < A_HPG // 2:
            h0, h1 = jnp.where(low64, c, 0.0), jnp.where(low64, r, 0.0)
        else:
            h0, h1 = jnp.where(low64, 0.0, r), jnp.where(low64, 0.0, c)
        q_ref[:, (2 * k) * LANES:(2 * k + 1) * LANES] = h0.astype(BF16)
        q_ref[:, (2 * k + 1) * LANES:(2 * k + 2) * LANES] = h1.astype(BF16)

    def emit_kv(i, name):
        z = z_kv[i]
        rows = jnp.concatenate([head_norm_rope(z[:, :LANES], kg_ref[i:i + 1, :]), z[:, LANES:]], axis=-1)
        if name in o:
            o[name][...] = rows
        if name + '_t' in o:
            rows_t = rows.T
            o[name + '_t'][0] = rows_t
            for half, tag in ((rows_t[:LANES], 'k'), (rows_t[LANES:], 'v')):
                key = tag + name[-1] + '_bf'
                if key in o:
                    n_tiles, _, width = o[key].shape[1:]
                    for j in range(n_tiles):
                        o[key][0, j] = half[:, j * width:(j + 1) * width].astype(BF16)

    lbl = lbl_ref[...]
    e = jnp.exp(lbl - jnp.max(lbl, axis=0, keepdims=True))
    lb = e[0:1, :] / jnp.sum(e, axis=0, keepdims=True)

    emit_q(0)
    fz = proj(OFF_FB, B_HEADS * B_KEY_DIM)
    lf_ref[...] = jnp.log(lb + (1.0 - lb) * jax.nn.sigmoid(fz))
    kb_ref[...] = (1.0 - lb) * jax.nn.sigmoid(-fz)
    emit_q(1)
    zqb = proj(OFF_QB, B_HEADS * B_KEY_DIM)
    qb_ref[...] = zqb * jax.nn.sigmoid(zqb)
    emit_q(2)
    vb_ref[...] = proj(OFF_IB, B_HEADS * B_KEY_DIM)
    emit_q(3)
    zg = proj(OFF_GB, B_HEADS * B_KEY_DIM)
    gb_ref[...] = zg * jax.nn.sigmoid(zg)
    emit_kv(0, 'kvc')
    mg_ref[:, :D_MODEL] = jax.nn.sigmoid(proj(OFF_MG, D_MODEL))
    emit_kv(1, 'kvs')
    mg_ref[:, D_MODEL:] = jax.nn.sigmoid(proj(OFF_MG + D_MODEL, D_MODEL))
    emit_kv(2, 'kvw')
    gate_ref[...] = jax.nn.sigmoid(proj(OFF_GATE, LANES))


def _inproj(x2d, cos_tab, sin_tab, consts, tm, seq_len=None):
    n = x2d.shape[0]
    tab_blocks = cos_tab.shape[0] // tm
    row = lambda w: pl.BlockSpec((tm, w), lambda i: (i, 0))
    tab = pl.BlockSpec((tm, LANES), lambda i: (i % tab_blocks, 0))
    token_major = {'q': (A_HEADS * LANES, BF16), 'gate': (LANES, F32), 'qb': (512, F32), 'kb': (512, F32),
                   'vb': (512, F32), 'lf': (512, F32), 'gb': (512, F32), 'mg': (2 * D_MODEL, F32),
                   'kvc': (KV_ROW, F32), 'kvs': (KV_ROW, F32), 'kvw': (KV_ROW, F32)}
    names = INPROJ_SAMPLE if seq_len is None else INPROJ_PROMPT
    specs, shapes = [], []
    for name in names:
        if name in token_major:
            w, dt = token_major[name]
            specs.append(row(w))
            shapes.append(jax.ShapeDtypeStruct((n, w), dt))
        elif name.endswith('_t'):
            per_seq = seq_len // tm
            specs.append(pl.BlockSpec((1, KV_ROW, tm), lambda i: (i // per_seq, 0, i % per_seq)))
            shapes.append(jax.ShapeDtypeStruct((n // seq_len, KV_ROW, seq_len), F32))
        else:
            width = S_TILE if name[1] == 's' else W_TILE
            per_seq = seq_len // tm
            specs.append(pl.BlockSpec((1, tm // width, LANES, width), lambda i: (i // per_seq, i % per_seq, 0, 0)))
            shapes.append(jax.ShapeDtypeStruct((n // seq_len, seq_len // width, LANES, width), BF16))
    outs = pl.pallas_call(
        functools.partial(_inproj_kernel, names=names),
        grid=(n // tm,),
        in_specs=[row(D_MODEL), _const_spec((1, D_MODEL)), _const_spec((D_MODEL, OFF_QB)),
                  _const_spec((D_MODEL, OFF_GATE - OFF_QB)), _const_spec((D_MODEL, LANES)), tab, tab,
                  _const_spec((1, LANES)), _const_spec((3, LANES)),
                  _const_spec(consts['lbl'].shape), _const_spec((2 * LANES, LANES))],
        out_specs=specs,
        out_shape=shapes,
        compiler_params=_params(("parallel",)),
        name="inproj",
    )(x2d, consts['attn_g'], consts['w_attn'], consts['w_rest'], consts['w_gate'], cos_tab, sin_tab,
      consts['q_g'], consts['k_g'],
      consts['lbl'], consts['mseg'])
    return dict(zip(names, outs))


def _chunk_row(chunk):
    return chunk * CHUNK_PITCH


def _compress_rows(tok_k, tok_v, pe_ref, w1_ref, w2_ref, kc_ref, vc_ref, m_rows, n_cmp):
    acc = jnp.zeros((m_rows, KV_ROW), F32)
    for p0 in range(0, CMP_BLOCK, CMP_GROUP):
        xs = []
        for p in range(p0, p0 + CMP_GROUP):
            start = _chunk_row(p // CMP_STRIDE) + p % CMP_STRIDE
            xp = jnp.concatenate([tok_k[pl.ds(start, m_rows, stride=CHUNK_PITCH), :],
                                  tok_v[pl.ds(start, m_rows, stride=CHUNK_PITCH), :]], axis=-1) + pe_ref[p:p + 1, :]
            xs.append(xp.astype(BF16))
        acc = acc + _dot(jnp.concatenate(xs, axis=-1), w1_ref[p0 // CMP_GROUP])
    hid = acc * jax.nn.sigmoid(acc)
    out = _dot(hid.astype(BF16), w2_ref[...])
    row = lax.broadcasted_iota(jnp.int32, out.shape, 0)
    out = jnp.where(row < n_cmp, out, 0.0)
    ncp = kc_ref.shape[1]
    kc_ref[0, 0:m_rows, :] = out[:, :LANES].astype(BF16)
    vc_ref[0, 0:m_rows, :] = out[:, LANES:].astype(BF16)
    if ncp > m_rows:
        kc_ref[0, m_rows:ncp, :] = jnp.zeros((ncp - m_rows, LANES), BF16)
        vc_ref[0, m_rows:ncp, :] = jnp.zeros((ncp - m_rows, LANES), BF16)


def _zero_chunks(tok_k, tok_v, first_chunk):
    r0 = _chunk_row(first_chunk)
    tok_k[r0:, :] = jnp.zeros((tok_k.shape[0] - r0, LANES), F32)
    tok_v[r0:, :] = jnp.zeros((tok_v.shape[0] - r0, LANES), F32)


def _compress_dense_kernel(rows_ref, pe_ref, w1_ref, w2_ref, kc_ref, vc_ref, tok_k, tok_v, *, m_rows, n_cmp):
    n_chunks = rows_ref.shape[1] // CMP_STRIDE

    def body(c, carry):
        src = pl.multiple_of(c * CMP_STRIDE, CMP_STRIDE)
        dst = pl.multiple_of(_chunk_row(c), SUBLANES)
        tok_k[pl.ds(dst, CMP_STRIDE), :] = rows_ref[0, pl.ds(src, CMP_STRIDE), 0:LANES]
        tok_v[pl.ds(dst, CMP_STRIDE), :] = rows_ref[0, pl.ds(src, CMP_STRIDE), LANES:KV_ROW]
        return carry

    lax.fori_loop(0, n_chunks, body, 0)
    _zero_chunks(tok_k, tok_v, n_chunks)
    _compress_rows(tok_k, tok_v, pe_ref, w1_ref, w2_ref, kc_ref, vc_ref, m_rows, n_cmp)


def _page_copy(pool_ref, page, stage, slot, sems):
    return pltpu.make_async_copy(pool_ref.at[page], stage.at[slot], sems.at[slot])


def _compress_paged_kernel(pt_ref, pool_ref, new_ref, pe_ref, w1_ref, w2_ref, kc_ref, vc_ref,
                           stage, tok_k, tok_v, sems, *, n_pages, m_rows, n_cmp):
    b = pl.program_id(0)

    def start_row(row):
        def start(p, carry):
            _page_copy(pool_ref, pt_ref[row, p], stage, p, sems).start()
            return carry
        lax.fori_loop(0, n_pages, start, 0)

    @pl.when(b == 0)
    def _():
        start_row(0)

    chunks_per_page = PAGE_SIZE // CMP_STRIDE
    first_new = n_pages * chunks_per_page
    _zero_chunks(tok_k, tok_v, first_new)
    ts = new_ref.shape[1]
    assert ts <= CMP_STRIDE
    tok_k[_chunk_row(first_new):_chunk_row(first_new) + ts, :] = new_ref[0, :, 0:LANES]
    tok_v[_chunk_row(first_new):_chunk_row(first_new) + ts, :] = new_ref[0, :, LANES:KV_ROW]

    group = int(np.gcd(n_pages, LAND_GROUP))

    def land(i, carry):
        for j in range(group):
            _page_copy(pool_ref, 0, stage, i * group + j, sems).wait()
        for j in range(group):
            p = i * group + j
            page_t = stage[p].T
            base = pl.multiple_of(_chunk_row(p * chunks_per_page), SUBLANES)
            for c in range(chunks_per_page):
                rows = slice(c * CMP_STRIDE, (c + 1) * CMP_STRIDE)
                tok_k[pl.ds(base + _chunk_row(c), CMP_STRIDE), :] = page_t[rows, 0:LANES]
                tok_v[pl.ds(base + _chunk_row(c), CMP_STRIDE), :] = page_t[rows, LANES:KV_ROW]
        return carry

    lax.fori_loop(0, n_pages // group, land, 0)

    @pl.when(b + 1 < pl.num_programs(0))
    def _():
        start_row(b + 1)

    _compress_rows(tok_k, tok_v, pe_ref, w1_ref, w2_ref, kc_ref, vc_ref, m_rows, n_cmp)


def _compress_geometry(t_real):
    t_pad = _round_up(t_real, SLC_BLOCK)
    n_cmp = t_pad // CMP_STRIDE - CMP_BLOCK // CMP_STRIDE + 1
    m_rows = _round_up(n_cmp, SUBLANES)
    ncp = _round_up(n_cmp, LANES)
    n_chunks = max(m_rows + CMP_BLOCK // CMP_STRIDE - 1, -(-t_real // CMP_STRIDE))
    return n_cmp, m_rows, ncp, _chunk_row(n_chunks)


def _compress_dense(rows, consts):
    b, t, _ = rows.shape
    n_cmp, m_rows, ncp, tok_rows = _compress_geometry(t)
    return pl.pallas_call(
        functools.partial(_compress_dense_kernel, m_rows=m_rows, n_cmp=n_cmp),
        grid=(b,),
        in_specs=[pl.BlockSpec((1, t, KV_ROW), lambda i: (i, 0, 0)),
                  _const_spec((CMP_BLOCK, KV_ROW)), _const_spec(CMP_W1_SHAPE),
                  _const_spec((KV_ROW, KV_ROW))],
        out_specs=[pl.BlockSpec((1, ncp, LANES), lambda i: (i, 0, 0))] * 2,
        out_shape=[jax.ShapeDtypeStruct((b, ncp, LANES), BF16)] * 2,
        scratch_shapes=[pltpu.VMEM((tok_rows, LANES), F32)] * 2,
        compiler_params=_params(("parallel",)),
        name="compress_dense",
    )(rows, consts['cmp_pe'], consts['cmp_w1'], consts['cmp_w2'])


def _compress_paged(page_table, pool, new_rows, consts):
    db, n_pages = page_table.shape
    ts = new_rows.shape[1]
    n_cmp, m_rows, ncp, tok_rows = _compress_geometry(n_pages * PAGE_SIZE + ts)
    cm = lambda nd: (lambda i, pt: (0,) * nd)
    grid_spec = pltpu.PrefetchScalarGridSpec(
        num_scalar_prefetch=1,
        grid=(db,),
        in_specs=[pl.BlockSpec(memory_space=pl.ANY),
                  pl.BlockSpec((1, ts, KV_ROW), lambda i, pt: (i, 0, 0)),
                  pl.BlockSpec((CMP_BLOCK, KV_ROW), cm(2), pipeline_mode=pl.Buffered(1)),
                  pl.BlockSpec(CMP_W1_SHAPE, cm(3), pipeline_mode=pl.Buffered(1)),
                  pl.BlockSpec((KV_ROW, KV_ROW), cm(2), pipeline_mode=pl.Buffered(1))],
        out_specs=[pl.BlockSpec((1, ncp, LANES), lambda i, pt: (i, 0, 0))] * 2,
        scratch_shapes=[pltpu.VMEM((n_pages, KV_ROW, PAGE_SIZE), F32), pltpu.VMEM((tok_rows, LANES), F32),
                        pltpu.VMEM((tok_rows, LANES), F32), pltpu.SemaphoreType.DMA((n_pages,))],
    )
    return pl.pallas_call(
        functools.partial(_compress_paged_kernel, n_pages=n_pages, m_rows=m_rows, n_cmp=n_cmp),
        grid_spec=grid_spec,
        out_shape=[jax.ShapeDtypeStruct((db, ncp, LANES), BF16)] * 2,
        compiler_params=_params(("arbitrary",)),
        name="compress_paged",
    )(page_table, pool, new_rows, consts['cmp_pe'], consts['cmp_w1'], consts['cmp_w2'])


def _softmax_rows(s, mask):
    s = jnp.where(mask, s, NEG_INF)
    p = jnp.where(mask, jnp.exp(s - jnp.max(s, axis=-1, keepdims=True)), 0.0)
    return p / jnp.maximum(jnp.sum(p, axis=-1, keepdims=True), 1e-30)


def _online_step(carry, s, v):
    m, l, acc = carry
    m_new = jnp.maximum(m, jnp.max(s, axis=-1, keepdims=True))
    alpha = jnp.exp(m - m_new)
    p = jnp.exp(s - m_new)
    l = alpha * l + jnp.sum(p, axis=-1, keepdims=True)
    acc = alpha * acc + _dot(p.astype(BF16), v)
    return m_new, l, acc


def _online_init(rows):
    return (jnp.full((rows, 1), NEG_INF, F32), jnp.zeros((rows, 1), F32), jnp.zeros((rows, LANES), F32))


def _online_finish(carry):
    _, l, acc = carry
    return acc / jnp.maximum(l, 1e-30)


def _block_scores(p_sum, cover_ref, qpos, n_blocks):
    hi, lo = _split_bf16(p_sum)
    imp = _dot(jnp.concatenate([hi, lo], axis=-1), cover_ref[...])
    blk = lax.broadcasted_iota(jnp.int32, imp.shape, 1)
    cur = qpos // SLC_BLOCK
    forced = (blk == 0) | (blk == cur) | (blk == cur - 1)
    score = jnp.where(forced, FORCE_SCORE, jnp.where(blk <= cur, imp, -1.0))
    return jnp.where(blk < n_blocks, score, -2.0)


def _topk_select(score, n_blocks, n_sel):
    n_tiles = score.shape[1] // LANES
    tiles = [score[:, t * LANES:(t + 1) * LANES] for t in range(n_tiles)]
    lane = lax.broadcasted_iota(jnp.int32, tiles[0].shape, 1)
    cnt = [jnp.zeros(tiles[0].shape, F32) for _ in range(n_tiles)]
    for s in range(n_blocks):
        col = score[:, s:s + 1]
        for t in range(n_tiles):
            if t * LANES > s:
                beats = col >= tiles[t]
            elif (t + 1) * LANES - 1 < s:
                beats = col > tiles[t]
            else:
                beats = (col > tiles[t]) | ((col == tiles[t]) & (lane + t * LANES > s))
            cnt[t] = cnt[t] + jnp.where(beats, 1.0, 0.0)
    blk = lax.broadcasted_iota(jnp.int32, score.shape, 1)
    return jnp.where((jnp.concatenate(cnt, axis=1) < n_sel) & (blk < n_blocks), 1.0, 0.0)


def _assemble_heads(heads, low64):
    chunks = []
    for k in range(A_HEADS // 2):
        a, b = heads[2 * k], heads[2 * k + 1]
        if k < A_HPG // 2:
            chunks.append(jnp.where(low64, a, pltpu.roll(b, A_HEAD_DIM, 1)))
        else:
            chunks.append(jnp.where(low64, pltpu.roll(a, A_HEAD_DIM, 1), b))
    return chunks


def _topk_rows(score_t, n_blocks, n_sel):
    n_tiles = score_t.shape[0] // SUBLANES
    tiles = [score_t[t * SUBLANES:(t + 1) * SUBLANES] for t in range(n_tiles)]
    sub = lax.broadcasted_iota(jnp.int32, tiles[0].shape, 0)
    cnt = [jnp.zeros(tiles[0].shape, F32) for _ in range(n_tiles)]
    for s in range(n_blocks):
        row = score_t[s:s + 1, :]
        for t in range(n_tiles):
            if t * SUBLANES > s:
                beats = row >= tiles[t]
            elif (t + 1) * SUBLANES - 1 < s:
                beats = row > tiles[t]
            else:
                later = sub + t * SUBLANES > s
                beats = (row > tiles[t]) | ((row == tiles[t]) & later)
            cnt[t] = cnt[t] + jnp.where(beats, 1.0, 0.0)
    blk = lax.broadcasted_iota(jnp.int32, score_t.shape, 0)
    return jnp.where((jnp.concatenate(cnt, axis=0) < n_sel) & (blk < n_blocks), 1.0, 0.0)


def _online_step_t(carry, s, v_t):
    m, l, acc = carry
    m_new = jnp.maximum(m, jnp.max(s, axis=-1, keepdims=True))
    alpha = jnp.exp(m - m_new)
    p = jnp.exp(s - m_new)
    l = alpha * l + jnp.sum(p, axis=-1, keepdims=True)
    acc = alpha * acc + _dot_nt(p.astype(BF16), v_t)
    return m_new, l, acc


def _nsa_prompt_kernel(bound_ref, q_ref, gate_ref, kc_ref, vc_ref, ks_ref, vs_ref, kw_ref, vw_ref, cover_ref,
                       exp_ref, o_ref, *, n_blocks, n_sel):
    i = pl.program_id(1)
    s0 = i * Q_BLOCK
    q = q_ref[0]
    gates = gate_ref[0]
    rows = A_HEADS * Q_BLOCK
    q_all = jnp.concatenate([q[:, h * LANES:(h + 1) * LANES] for h in range(A_HEADS)], axis=0)
    qpos1 = s0 + lax.broadcasted_iota(jnp.int32, (Q_BLOCK, 1), 0)
    qpos_all = jnp.concatenate([qpos1] * A_HEADS, axis=0)

    ncp = kc_ref.shape[1]
    c_end = lax.broadcasted_iota(jnp.int32, (1, ncp), 1) * CMP_STRIDE + (CMP_BLOCK - 1)
    p_c = _softmax_rows(_dot_nt(q_all, kc_ref[0]), c_end <= qpos_all)
    o_c = _dot(p_c.astype(BF16), vc_ref[0])

    cur = (s0 + lax.broadcasted_iota(jnp.int32, (1, Q_BLOCK), 1)) // SLC_BLOCK
    blk = lax.broadcasted_iota(jnp.int32, (LANES, Q_BLOCK), 0)
    forced = (blk == 0) | (blk == cur) | (blk == cur - 1)
    scores = []
    for g in range(A_KV_GROUPS):
        base = g * A_HPG * Q_BLOCK
        p_sum = p_c[base:base + Q_BLOCK]
        for hh in range(1, A_HPG):
            p_sum = p_sum + p_c[base + hh * Q_BLOCK:base + (hh + 1) * Q_BLOCK]
        hi, lo = _split_bf16(p_sum)
        imp_t = _dot(jnp.concatenate([hi, lo], axis=-1), cover_ref[...]).T
        score = jnp.where(forced, FORCE_SCORE, jnp.where(blk <= cur, imp_t, -1.0))
        scores.append(jnp.where(blk < n_blocks, score, -2.0))
    nb8 = _round_up(n_blocks, SUBLANES)
    sel_t = _topk_rows(jnp.concatenate(scores, axis=1)[0:nb8], n_blocks, n_sel)
    if nb8 < LANES:
        sel_t = jnp.concatenate([sel_t, jnp.zeros((LANES - nb8, sel_t.shape[1]), F32)], axis=0)

    aug = []
    for g in range(A_KV_GROUPS):
        sel = sel_t[:, g * Q_BLOCK:(g + 1) * Q_BLOCK].T
        aug += [((sel - 1.0) * MASK_BIG).astype(BF16)] * A_HPG
    q_aug = jnp.concatenate([q_all, jnp.concatenate(aug, axis=0)], axis=1)
    per_tile = SEL_TK // S_TILE
    cat = lambda kt, ref_tile: jnp.concatenate([ref_tile(kt * per_tile + j) for j in range(per_tile)], axis=1)

    n_full = s0 // SEL_TK
    kpos = n_full * SEL_TK + lax.broadcasted_iota(jnp.int32, (1, SEL_TK), 1)
    causal = jnp.concatenate([jnp.where(kpos <= qpos1, 0.0, NEG_INF)] * A_HEADS, axis=0)

    st = jnp.maximum(i - WINDOW // Q_BLOCK, 0) * (Q_BLOCK // W_TILE)
    n_wt = (WINDOW + Q_BLOCK) // W_TILE
    d = qpos1 - (st * W_TILE + lax.broadcasted_iota(jnp.int32, (1, n_wt * W_TILE), 1))
    band = jnp.concatenate([jnp.where((d >= 0) & (d < WINDOW), 0.0, NEG_INF)] * A_HEADS, axis=0)

    def attend(shifted):
        half = rows // A_KV_GROUPS

        def weigh(p, v_t):
            ones = jnp.ones((A_HEAD_DIM, v_t.shape[1]), BF16)
            return jnp.concatenate(
                [_dot_nt(p[g * half:(g + 1) * half],
                         jnp.concatenate([v_t[g * A_HEAD_DIM:(g + 1) * A_HEAD_DIM], ones], axis=0))
                 for g in range(A_KV_GROUPS)], axis=0)

        def normalise(acc):
            swapped = pltpu.roll(acc, A_HEAD_DIM, 1)
            lane = lax.broadcasted_iota(jnp.int32, (half, LANES), 1)
            g0 = acc[:half] / jnp.maximum(swapped[:half], 1e-30)
            g1 = swapped[half:] / jnp.maximum(acc[half:], 1e-30)
            return jnp.concatenate([jnp.where(lane < A_HEAD_DIM, g0, 0.0), jnp.where(lane < A_HEAD_DIM, 0.0, g1)],
                                   axis=0)

        def step(state, s, v_t):
            if shifted:
                return _online_step_t(state, s, v_t)
            return state + weigh(jnp.exp(s).astype(BF16), v_t)

        def sel_step(kt, state, bias):
            k_aug = jnp.concatenate([cat(kt, lambda n: ks_ref[0, n]), cat(kt, lambda n: exp_ref[n])], axis=0)
            s = _dot(q_aug, k_aug)
            return step(state, s if bias is None else s + bias, cat(kt, lambda n: vs_ref[0, n]))

        init = _online_init(rows)
        state = lax.fori_loop(0, n_full, lambda kt, carry: sel_step(kt, carry, None), init if shifted else init[2])
        state = sel_step(n_full, state, causal)
        o_sel = _online_finish(state) if shifted else normalise(state)

        kw_t = jnp.concatenate([kw_ref[0, st + j] for j in range(n_wt)], axis=1)
        vw_t = jnp.concatenate([vw_ref[0, st + j] for j in range(n_wt)], axis=1)
        s_w = _dot(q_all, kw_t) + band
        if shifted:
            p_w = jnp.exp(s_w - jnp.max(s_w, axis=-1, keepdims=True))
            o_win = _dot_nt(p_w.astype(BF16), vw_t) / jnp.maximum(jnp.sum(p_w, axis=-1, keepdims=True), 1e-30)
        else:
            o_win = normalise(weigh(jnp.exp(s_w).astype(BF16), vw_t))
        return o_sel, o_win

    o_s, o_w = lax.cond(bound_ref[0] <= SCORE_SAFE, lambda: attend(False), lambda: attend(True))

    heads = []
    for h in range(A_HEADS):
        r = slice(h * Q_BLOCK, (h + 1) * Q_BLOCK)
        heads.append(gates[:, 3 * h:3 * h + 1] * o_c[r] + gates[:, 3 * h + 1:3 * h + 2] * o_s[r]
                     + gates[:, 3 * h + 2:3 * h + 3] * o_w[r])
    low64 = lax.broadcasted_iota(jnp.int32, (Q_BLOCK, LANES), 1) < A_HEAD_DIM
    for k, chunk in enumerate(_assemble_heads(heads, low64)):
        o_ref[0, :, k * LANES:(k + 1) * LANES] = chunk.astype(BF16)


def _cover_matrix(n_cmp, ncp, n_blocks, nsp):
    c = np.arange(ncp)[:, None]
    s = np.arange(nsp)[None, :]
    cover = ((c * CMP_STRIDE < s * SLC_BLOCK + SLC_BLOCK) & (c * CMP_STRIDE + CMP_BLOCK > s * SLC_BLOCK)
             & (c < n_cmp) & (s < n_blocks))
    return jnp.asarray(np.concatenate([cover, cover], axis=0), BF16)


def _expand_matrix(n_keys, block0=0):
    e = (np.arange(n_keys)[None, :] // SLC_BLOCK) == (block0 + np.arange(LANES)[:, None])
    return e


def _expand_tiles(n_keys, tk):
    e = _expand_matrix(n_keys).reshape(LANES, n_keys // tk, tk)
    return jnp.asarray(np.transpose(e, (1, 0, 2)), BF16)


def _nsa_prompt(score_bound, q, gates, kc, vc, ks_bf, vs_bf, kw_bf, vw_bf, t):
    b = q.shape[0]
    ncp = kc.shape[1]
    n_cmp = t // CMP_STRIDE - CMP_BLOCK // CMP_STRIDE + 1
    n_blocks = t // SLC_BLOCK
    assert n_blocks <= LANES and t % SEL_TK == 0 and t >= WINDOW + Q_BLOCK
    cover = _cover_matrix(n_cmp, ncp, n_blocks, LANES)
    expand = _expand_tiles(t, S_TILE)
    per_b = lambda rows, w: pl.BlockSpec((1, rows, w), lambda bi, i: (bi, 0, 0))
    tiles = lambda a: pl.BlockSpec((1,) + a.shape[1:], lambda bi, i: (bi, 0, 0, 0))
    return pl.pallas_call(
        functools.partial(_nsa_prompt_kernel, n_blocks=n_blocks, n_sel=min(N_SELECT, n_blocks)),
        grid=(b, t // Q_BLOCK),
        in_specs=[pl.BlockSpec(memory_space=pltpu.SMEM),
                  pl.BlockSpec((1, Q_BLOCK, A_HEADS * LANES), lambda bi, i: (bi, i, 0)),
                  pl.BlockSpec((1, Q_BLOCK, LANES), lambda bi, i: (bi, i, 0)),
                  per_b(ncp, LANES), per_b(ncp, LANES),
                  tiles(ks_bf), tiles(vs_bf), tiles(kw_bf), tiles(vw_bf),
                  _const_spec(cover.shape), _const_spec(expand.shape)],
        out_specs=pl.BlockSpec((1, Q_BLOCK, A_HEADS * A_HEAD_DIM), lambda bi, i: (bi, i, 0)),
        out_shape=jax.ShapeDtypeStruct((b, t, A_HEADS * A_HEAD_DIM), BF16),
        compiler_params=_params(("parallel", "parallel")),
        name="nsa_prompt",
    )(score_bound, q, gates, kc, vc, ks_bf, vs_bf, kw_bf, vw_bf, cover, expand)


def _key_page_copy(pool_ref, page, bufs, half, slot, sems):
    dst = bufs.at[half, :, pl.ds(pl.multiple_of(slot * PAGE_SIZE, PAGE_SIZE), PAGE_SIZE)]
    return pltpu.make_async_copy(pool_ref.at[page], dst, sems.at[half])


def _nsa_sample_kernel(pt_ref, q_ref, gate_ref, kc_ref, vc_ref, pool_ref, new_ref, win_ref, wnew_ref,
                       cover_ref, exp_ref, o_ref, bufs, sems, *, n_pages, n_blocks, n_sel, key_chunk):
    b = pl.program_id(0)
    half = b % 2

    def start_row(row, into):
        def start(p, carry):
            _key_page_copy(pool_ref, pt_ref[row, p], bufs, into, p, sems).start()
            return carry
        lax.fori_loop(0, n_pages, start, 0)

    @pl.when(b == 0)
    def _():
        start_row(0, 0)

    @pl.when(b + 1 < pl.num_programs(0))
    def _():
        start_row(b + 1, 1 - half)

    buf = bufs.at[half]
    past = n_pages * PAGE_SIZE
    ts = q_ref.shape[1]
    buf[:, past:past + LANES] = new_ref[0]

    q = q_ref[0].astype(F32)
    q_all = jnp.concatenate([q[:, h * LANES:(h + 1) * LANES] for h in range(A_HEADS)], axis=0).astype(BF16)
    rows = A_HEADS * ts
    qpos1 = past + lax.broadcasted_iota(jnp.int32, (ts, 1), 0)
    qpos_all = jnp.concatenate([qpos1] * A_HEADS, axis=0)

    ncp = kc_ref.shape[1]
    c_end = lax.broadcasted_iota(jnp.int32, (1, ncp), 1) * CMP_STRIDE + (CMP_BLOCK - 1)
    p_c = _softmax_rows(_dot_nt(q_all, kc_ref[0]), c_end <= qpos_all)
    o_c = _dot(p_c.astype(BF16), vc_ref[0])

    p_groups = []
    for g in range(A_KV_GROUPS):
        base = g * A_HPG * ts
        p_sum = p_c[base:base + ts]
        for hh in range(1, A_HPG):
            p_sum = p_sum + p_c[base + hh * ts:base + (hh + 1) * ts]
        p_groups.append(p_sum)
    qpos_g = jnp.concatenate([qpos1] * A_KV_GROUPS, axis=0)
    score = _block_scores(jnp.concatenate(p_groups, axis=0), cover_ref, qpos_g, n_blocks)
    sel = _topk_select(score, n_blocks, n_sel).astype(BF16)

    def wait(p, carry):
        _key_page_copy(pool_ref, 0, bufs, half, p, sems).wait()
        return carry

    lax.fori_loop(0, n_pages, wait, 0)

    n_keys = buf.shape[1]
    carry = _online_init(rows)
    for ck in range(-(-n_keys // key_chunk)):
        k0 = ck * key_chunk
        kn = min(key_chunk, n_keys - k0)
        s = _dot(q_all, buf[0:LANES, k0:k0 + kn].astype(BF16))
        blk0 = ck * (key_chunk // SLC_BLOCK)
        em = _dot(sel[:, blk0:blk0 + LANES], exp_ref[:, 0:kn])
        kpos = k0 + lax.broadcasted_iota(jnp.int32, (1, kn), 1)
        bias = jnp.where((em > 0.5) & (kpos <= qpos_g), 0.0, NEG_INF)
        biases = []
        for g in range(A_KV_GROUPS):
            biases += [bias[g * ts:(g + 1) * ts]] * A_HPG
        carry = _online_step_t(carry, s + jnp.concatenate(biases, axis=0),
                               buf[LANES:KV_ROW, k0:k0 + kn].astype(BF16))
    o_s = _online_finish(carry)

    wb = win_ref.shape[2]
    kv_w = jnp.concatenate([win_ref[0], wnew_ref[0]], axis=1).astype(BF16)
    idx = lax.broadcasted_iota(jnp.int32, (1, wb + LANES), 1)
    w_pos = jnp.where(idx < wb, past - wb + idx, past + idx - wb)
    d = qpos_all - w_pos
    m_w = (d >= 0) & (d < WINDOW) & (w_pos >= 0)
    p_w = _softmax_rows(_dot(q_all, kv_w[0:LANES]), m_w)
    o_w = _dot_nt(p_w.astype(BF16), kv_w[LANES:KV_ROW])

    gates = gate_ref[0]
    heads = []
    for h in range(A_HEADS):
        r = slice(h * ts, (h + 1) * ts)
        heads.append(gates[:, 3 * h:3 * h + 1] * o_c[r] + gates[:, 3 * h + 1:3 * h + 2] * o_s[r]
                     + gates[:, 3 * h + 2:3 * h + 3] * o_w[r])
    low64 = lax.broadcasted_iota(jnp.int32, (ts, LANES), 1) < A_HEAD_DIM
    for k, chunk in enumerate(_assemble_heads(heads, low64)):
        o_ref[0, :, k * LANES:(k + 1) * LANES] = chunk.astype(BF16)


def _nsa_sample(page_table, q, gates, kc, vc, pool, new_tile, win_t, wnew_tile):
    db, n_pages = page_table.shape
    wb = win_t.shape[2]
    assert wb % LANES == 0
    ts = q.shape[1]
    past = n_pages * PAGE_SIZE
    ncp = kc.shape[1]
    t_pad = _round_up(past + ts, SLC_BLOCK)
    n_cmp = t_pad // CMP_STRIDE - CMP_BLOCK // CMP_STRIDE + 1
    n_blocks = t_pad // SLC_BLOCK
    key_chunk = LANES * SLC_BLOCK
    n_keys = past + LANES
    nsp = LANES * (-(-n_keys // key_chunk))
    assert nsp >= n_blocks and ts % SUBLANES == 0
    cover = _cover_matrix(n_cmp, ncp, n_blocks, nsp)
    expand = jnp.asarray(_expand_matrix(min(key_chunk, n_keys)), BF16)
    cm = lambda nd: (lambda i, pt: (0,) * nd)
    per_b = lambda r, w: pl.BlockSpec((1, r, w), lambda i, pt: (i, 0, 0))
    grid_spec = pltpu.PrefetchScalarGridSpec(
        num_scalar_prefetch=1,
        grid=(db,),
        in_specs=[per_b(ts, A_HEADS * LANES), per_b(ts, LANES), per_b(ncp, LANES), per_b(ncp, LANES),
                  pl.BlockSpec(memory_space=pl.ANY), per_b(KV_ROW, LANES), per_b(KV_ROW, wb), per_b(KV_ROW, LANES),
                  pl.BlockSpec(cover.shape, cm(2), pipeline_mode=pl.Buffered(1)),
                  pl.BlockSpec(expand.shape, cm(2), pipeline_mode=pl.Buffered(1))],
        out_specs=per_b(ts, A_HEADS * A_HEAD_DIM),
        scratch_shapes=[pltpu.VMEM((2, KV_ROW, n_keys), F32), pltpu.SemaphoreType.DMA((2,))],
    )
    return pl.pallas_call(
        functools.partial(_nsa_sample_kernel, n_pages=n_pages, n_blocks=n_blocks,
                          n_sel=min(N_SELECT, n_blocks), key_chunk=key_chunk),
        grid_spec=grid_spec,
        out_shape=jax.ShapeDtypeStruct((db, ts, A_HEADS * A_HEAD_DIM), BF16),
        compiler_params=_params(("arbitrary",)),
        name="nsa_sample",
    )(page_table, q, gates, kc, vc, pool, new_tile, win_t, wnew_tile, cover, expand)


def _hgrn_matrices():
    c = HGRN_CHUNK
    t = np.arange(c)[:, None]
    u = np.arange(c)[None, :]
    mats = [u <= t]
    masks = [t == u]
    for lvl in range(HGRN_LEVELS):
        m = 1 << lvl
        mid = (t // (2 * m)) * (2 * m) + m - 1
        mats.append((u > mid) & (u <= t))
        mats.append((u > t) & (u <= mid))
        masks.append((t // (2 * m) == u // (2 * m)) & (t % (2 * m) >= m) & (u % (2 * m) < m))
    mats.append(u > t)
    pm = np.concatenate(mats, axis=0)
    return (jnp.asarray(np.concatenate([pm, pm], axis=1), BF16),
            jnp.asarray(np.stack(masks).astype(np.float32)))


def _hgrn_kernel(qb_ref, kb_ref, vb_ref, lf_ref, gb_ref, s0_ref, gn_ref, pm_ref, lm_ref,
                 ob_ref, sout_ref, st_scr, pad_scr):
    j = pl.program_id(1)
    c = HGRN_CHUNK
    n_rows, t_blk = qb_ref.shape[0], qb_ref.shape[1]

    @pl.when(j == 0)
    def _():
        for r in range(n_rows):
            for h in range(B_HEADS):
                st_scr[r * B_HEADS + h] = s0_ref[r, h].T

    def load(ref, r, slot, rows):
        if t_blk >= c:
            return ref[r, rows, :]
        pad_scr[4 * r + slot] = jnp.zeros((c, B_HEADS * B_KEY_DIM), F32)
        pad_scr[4 * r + slot, 0:t_blk, :] = ref[r]
        return pad_scr[4 * r + slot]

    gn = gn_ref[...]
    n_sub = max(t_blk // c, 1)
    prepared = {}
    for r in range(n_rows):
        for sub in range(n_sub):
            rows = slice(sub * c, (sub + 1) * c)
            qb, kb, vb, lf = (load(ref, r, slot, rows) for slot, ref in enumerate((qb_ref, kb_ref, vb_ref, lf_ref)))
            lf_hi, lf_lo = _split_bf16(lf)
            ex = _dot(pm_ref[...], jnp.concatenate([lf_hi, lf_lo], axis=0))
            for h in range(B_HEADS):
                sl = slice(h * B_KEY_DIM, (h + 1) * B_KEY_DIM)
                q, k = qb[:, sl], kb[:, sl]
                v = vb[:, sl].astype(BF16)
                b_cum = ex[0:c, sl]
                a = lm_ref[0] * _dot_nt(q.astype(BF16), k.astype(BF16))
                for lvl in range(HGRN_LEVELS):
                    eq = ex[(2 * lvl + 1) * c:(2 * lvl + 2) * c, sl]
                    ek = ex[(2 * lvl + 2) * c:(2 * lvl + 3) * c, sl]
                    a = a + lm_ref[lvl + 1] * _dot_nt((q * jnp.exp(eq)).astype(BF16),
                                                      (k * jnp.exp(ek)).astype(BF16))
                e_end = ex[(2 * HGRN_LEVELS + 1) * c:(2 * HGRN_LEVELS + 2) * c, sl]
                prepared[r, sub, h] = ((q * jnp.exp(b_cum)).astype(BF16), _dot(a.astype(BF16), v),
                                       jnp.exp(b_cum[c - 1:c, :]), _dot_tn(v, (k * jnp.exp(e_end)).astype(BF16)))
    for r in range(n_rows):
        for h in range(B_HEADS):
            sl = slice(h * B_KEY_DIM, (h + 1) * B_KEY_DIM)
            st = st_scr[r * B_HEADS + h]
            for sub in range(n_sub):
                out_rows = slice(sub * c, (sub + 1) * c) if t_blk >= c else slice(0, t_blk)
                q_dec, o_intra, decay_end, kv_end = prepared[r, sub, h]
                o = _dot_nt(q_dec, st.astype(BF16)) + o_intra
                st = st * decay_end + kv_end
                y = o * lax.rsqrt(jnp.mean(o * o, axis=-1, keepdims=True) + EPS) * gn
                ob_ref[r, out_rows, sl] = y[0:min(t_blk, c)] * gb_ref[r, out_rows, sl]
            st_scr[r * B_HEADS + h] = st

    @pl.when(j == pl.num_programs(1) - 1)
    def _():
        for r in range(n_rows):
            for h in range(B_HEADS):
                sout_ref[r, h] = st_scr[r * B_HEADS + h].T


def _hgrn(qb, kb, vb, lf, gb, s0, consts):
    b, t, w = qb.shape
    t_blk = min(t, HGRN_STEP_CHUNKS * HGRN_CHUNK)
    assert t % t_blk == 0 and (t_blk % HGRN_CHUNK == 0 or t == t_blk < HGRN_CHUNK)
    n_rows = HGRN_SHORT_ROWS if (t < HGRN_CHUNK and b % HGRN_SHORT_ROWS == 0) else 1
    tok = pl.BlockSpec((n_rows, t_blk, w), lambda bi, j: (bi, j, 0))
    st = pl.BlockSpec((n_rows, B_HEADS, B_KEY_DIM, B_KEY_DIM), lambda bi, j: (bi, 0, 0, 0))
    return pl.pallas_call(
        _hgrn_kernel,
        grid=(b // n_rows, t // t_blk),
        in_specs=[tok, tok, tok, tok, tok, st, _const_spec((1, B_KEY_DIM)),
                  _const_spec(consts['hgrn_pm'].shape), _const_spec(consts['hgrn_lm'].shape)],
        out_specs=[tok, st],
        out_shape=[jax.ShapeDtypeStruct((b, t, w), F32),
                   jax.ShapeDtypeStruct((b, B_HEADS, B_KEY_DIM, B_KEY_DIM), F32)],
        scratch_shapes=[pltpu.VMEM((n_rows * B_HEADS, B_KEY_DIM, B_KEY_DIM), F32),
                        pltpu.VMEM((4 * n_rows, HGRN_CHUNK, w), F32)],
        compiler_params=_params(("parallel", "arbitrary")),
        name="hgrn",
    )(qb, kb, vb, lf, gb, s0, consts['hgrn_g'], consts['hgrn_pm'], consts['hgrn_lm'])


def _merge_ffn_kernel(x_ref, oa_ref, ob_ref, mg_ref, p1_ref, p2_ref, wa_ref, wb_ref, wo_ref, fg_ref, win_ref,
                      cw_ref, cb_ref, wout_ref, y_ref, a_ref, carry_scr, *, seq_len):
    tm = x_ref.shape[0]
    mg = mg_ref[...]
    m = (mg[:, :D_MODEL] * _dot(oa_ref[...], wa_ref[...])
         + mg[:, D_MODEL:] * _dot(ob_ref[...].astype(BF16), wb_ref[...]))
    x2 = x_ref[...] + _dot(m.astype(BF16), wo_ref[...])
    h = (x2 * lax.rsqrt(jnp.mean(x2 * x2, axis=-1, keepdims=True) + EPS) * fg_ref[...]).astype(BF16)
    row = lax.broadcasted_iota(jnp.int32, (tm, 1), 0)
    carried = seq_len >= tm
    if carried:
        j = pl.program_id(0) % (seq_len // tm)

        @pl.when(j == 0)
        def _():
            carry_scr[0:2, :] = p1_ref[0]
    else:
        t = row % seq_len

    y = x2
    chunks = range(0, D_FF, FFN_CHUNK)
    pre = [(_dot(h, win_ref[:, lo:lo + FFN_CHUNK]), _dot(h, win_ref[:, D_FF + lo:D_FF + lo + FFN_CHUNK]))
           for lo in chunks]
    for lo, (a, gate) in zip(chunks, pre):
        cols = slice(lo, lo + FFN_CHUNK)
        if carried:
            prev = carry_scr[:, cols]
            a1 = jnp.where(row == 0, prev[1:2], pltpu.roll(a, 1, 0))
            a2 = jnp.where(row == 0, prev[0:1], jnp.where(row == 1, prev[1:2], pltpu.roll(a, 2, 0)))
            carry_scr[0:2, cols] = a[tm - 2:tm]
            a_ref[0, :, cols] = a[tm - 2:tm]
        else:
            a1 = jnp.where(t == 0, p1_ref[:, cols], pltpu.roll(a, 1, 0))
            a2 = jnp.where(t < 2, p2_ref[:, cols], pltpu.roll(a, 2, 0))
            a_ref[:, cols] = a
        a_conv = cb_ref[:, cols] + a2 * cw_ref[0:1, cols] + a1 * cw_ref[1:2, cols] + a * cw_ref[2:3, cols]
        act = a_conv * jax.nn.sigmoid(a_conv) * gate
        y = y + _dot(act.astype(BF16), wout_ref[cols, :])
    y_ref[...] = y


def _merge_ffn(x2d, oa, ob, mg, conv_state, seq_len, consts, tm):
    n = x2d.shape[0]
    b = n // seq_len
    row = lambda w: pl.BlockSpec((tm, w), lambda i: (i, 0))
    if seq_len >= tm:
        assert seq_len % tm == 0
        per_seq = seq_len // tm
        p1, p2 = conv_state, conv_state
        p_spec = pl.BlockSpec((1, FFN_CONV - 1, D_FF), lambda i: (i // per_seq, 0, 0))
        a_shape = jax.ShapeDtypeStruct((b, FFN_CONV - 1, D_FF), F32)
        a_spec = pl.BlockSpec((1, FFN_CONV - 1, D_FF), lambda i: (i // per_seq, 0, 0))
    else:
        assert tm % seq_len == 0 and seq_len >= FFN_CONV - 1
        zeros = jnp.zeros((b, seq_len - 1, D_FF), F32)
        p1 = jnp.concatenate([conv_state[:, 1:2], zeros], axis=1).reshape(n, D_FF)
        p2 = jnp.concatenate([conv_state, zeros[:, 1:]], axis=1).reshape(n, D_FF)
        p_spec = row(D_FF)
        a_shape = jax.ShapeDtypeStruct((n, D_FF), F32)
        a_spec = row(D_FF)
    y, a_out = pl.pallas_call(
        functools.partial(_merge_ffn_kernel, seq_len=seq_len),
        grid=(n // tm,),
        in_specs=[row(D_MODEL), row(A_HEADS * A_HEAD_DIM), row(B_HEADS * B_KEY_DIM), row(2 * D_MODEL),
                  p_spec, p_spec,
                  _const_spec((A_HEADS * A_HEAD_DIM, D_MODEL)), _const_spec((B_HEADS * B_KEY_DIM, D_MODEL)),
                  _const_spec((D_MODEL, D_MODEL)), _const_spec((1, D_MODEL)),
                  _const_spec((D_MODEL, 2 * D_FF)), _const_spec((FFN_CONV, D_FF)), _const_spec((1, D_FF)),
                  _const_spec((D_FF, D_MODEL))],
        out_specs=[row(D_MODEL), a_spec],
        out_shape=[jax.ShapeDtypeStruct((n, D_MODEL), F32), a_shape],
        scratch_shapes=[pltpu.VMEM((SUBLANES, D_FF), F32)],
        compiler_params=_params(("arbitrary",)),
        name="merge_ffn",
    )(x2d, oa, ob, mg, p1, p2, consts['w_a'], consts['w_b'], consts['w_out'], consts['ffn_g'],
      consts['ffn_w_in'], consts['conv_w'], consts['conv_b'], consts['ffn_w_out'])
    if seq_len >= tm:
        return y, a_out
    return y, a_out.reshape(b, seq_len, D_FF)[:, seq_len - (FFN_CONV - 1):]


def _prepare_consts(attn_norm_g, w_in, q_norm_g, k_norm_g, cmp_pos_emb, cmp_w1, cmp_w2, hgrn_lb_logits,
                    hgrn_norm_g, w_branch, w_out, ffn_norm_g, ffn_w_in, ffn_conv_w, ffn_conv_b, ffn_w_out):
    n_q = A_HEADS * A_HEAD_DIM
    gate_lo = n_q + 3 * KV_ROW
    gate_hi = gate_lo + 3 * A_HEADS
    w_gate = jnp.pad(w_in[:, gate_lo:gate_hi], ((0, 0), (0, LANES - 3 * A_HEADS))).astype(BF16)
    seg = np.arange(LANES) // A_HEAD_DIM
    mseg = (seg[:, None] == seg[None, :]).astype(np.float32) / A_HEAD_DIM
    eye = jnp.eye(2 * A_KV_GROUPS, dtype=F32)
    jsel = np.repeat(np.arange(2), A_KV_GROUPS)
    w1 = cmp_w1.reshape(2, CMP_BLOCK, A_HEAD_DIM, A_HEAD_DIM)[jsel]
    w1_bd = jnp.einsum('ab,apde->padbe', eye, w1).reshape(CMP_W1_SHAPE).astype(BF16)
    w2_bd = jnp.einsum('ab,ade->adbe', eye, cmp_w2[jsel]).reshape(KV_ROW, KV_ROW).astype(BF16)
    pe = jnp.transpose(cmp_pos_emb[jsel], (1, 0, 2)).reshape(CMP_BLOCK, KV_ROW)
    pm, lm = _hgrn_matrices()
    return {
        'attn_g': attn_norm_g.reshape(1, D_MODEL), 'w_attn': w_in[:, :gate_lo].astype(BF16),
        'w_rest': w_in[:, gate_hi:].astype(BF16), 'w_gate': w_gate,
        'q_g': jnp.tile(q_norm_g, 2).reshape(1, LANES), 'k_g': jnp.tile(k_norm_g, (1, 2)),
        'lbl': hgrn_lb_logits.astype(F32), 'mseg': jnp.asarray(np.concatenate([mseg, mseg], axis=0), BF16),
        'cmp_pe': pe, 'cmp_w1': w1_bd, 'cmp_w2': w2_bd,
        'hgrn_g': hgrn_norm_g.reshape(1, B_KEY_DIM), 'hgrn_pm': pm, 'hgrn_lm': lm,
        'w_a': w_branch[:n_q].astype(BF16), 'w_b': w_branch[n_q:].astype(BF16), 'w_out': w_out.astype(BF16),
        'ffn_g': ffn_norm_g.reshape(1, D_MODEL), 'ffn_w_in': ffn_w_in.astype(BF16),
        'conv_w': ffn_conv_w, 'conv_b': ffn_conv_b.reshape(1, D_FF), 'ffn_w_out': ffn_w_out.astype(BF16),
    }


def _rope_tables(pos, reps):
    half = A_HEAD_DIM // 2
    inv = ROPE_THETA ** (-jnp.arange(half, dtype=F32) / half)
    ang = pos.astype(F32)[:, None] * inv[None, :]
    cos, sin = jnp.cos(ang), jnp.sin(ang)
    cos_t = jnp.tile(cos, (reps, LANES // half))
    sin_t = jnp.tile(jnp.concatenate([-sin, sin], axis=-1), (reps, LANES // A_HEAD_DIM))
    return cos_t, sin_t


def kernel(x_prompt, x_sample, cache_cmp_kv, cache_slc_kv, page_table, state_win_kv, state_hgrn, state_ffn_conv, attn_norm_g, w_in, q_norm_g, k_norm_g, cmp_pos_emb, cmp_w1, cmp_w2, hgrn_lb_logits, hgrn_norm_g, w_branch, w_out, ffn_norm_g, ffn_w_in, ffn_conv_w, ffn_conv_b, ffn_w_out):
    assert w_in.shape[0] == 1, "single-layer step"
    b, t, _ = x_prompt.shape
    db, ts, _ = x_sample.shape
    n_pool = cache_cmp_kv.shape[1]
    past = page_table.shape[1] * PAGE_SIZE
    wb = state_win_kv.shape[2]
    assert t % SLC_BLOCK == 0 and t % Q_BLOCK == 0
    consts = _prepare_consts(attn_norm_g[0], w_in[0], q_norm_g[0], k_norm_g[0], cmp_pos_emb[0], cmp_w1[0],
                             cmp_w2[0], hgrn_lb_logits, hgrn_norm_g[0], w_branch[0], w_out[0], ffn_norm_g[0],
                             ffn_w_in[0], ffn_conv_w[0], ffn_conv_b[0], ffn_w_out[0])
    kv_shape = (2, A_KV_GROUPS, A_HEAD_DIM)
    tm_p = min(256, t)
    n_s = db * ts

    fp = _inproj(x_prompt.reshape(b * t, D_MODEL), *_rope_tables(jnp.arange(t, dtype=jnp.int32), 1), consts, tm_p,
                 seq_len=t)
    seq = lambda a: a.reshape(b, t, a.shape[-1])
    kc_p, vc_p = _compress_dense(seq(fp['kvc']), consts)
    score_bound = (1.01 * A_HEAD_DIM ** 0.5 * jnp.max(jnp.abs(q_norm_g[0]))
                   * jnp.max(jnp.abs(k_norm_g[0, 1:]))).reshape(1).astype(F32)
    oa_p = _nsa_prompt(score_bound, seq(fp['q']), seq(fp['gate']), kc_p, vc_p, fp['ks_bf'], fp['vs_bf'],
                       fp['kw_bf'], fp['vw_bf'], t)
    ob_p, s_p = _hgrn(seq(fp['qb']), seq(fp['kb']), seq(fp['vb']), seq(fp['lf']), seq(fp['gb']),
                      jnp.zeros((b, B_HEADS, B_KEY_DIM, B_KEY_DIM), F32), consts)
    y_p, conv_p = _merge_ffn(x_prompt.reshape(b * t, D_MODEL), oa_p.reshape(b * t, -1), ob_p.reshape(b * t, -1),
                             fp['mg'], jnp.zeros((b, FFN_CONV - 1, D_FF), F32), t, consts, min(512, t))

    fs = _inproj(x_sample.reshape(n_s, D_MODEL), *_rope_tables(past + jnp.arange(ts, dtype=jnp.int32), db),
                 consts, n_s)
    sseq = lambda a: a.reshape(db, ts, a.shape[-1])
    feat = lambda c: jnp.transpose(c, (0, 2, 3, 4, 1)).reshape(c.shape[0], KV_ROW, c.shape[1])
    new_tile = lambda rows: jnp.pad(jnp.transpose(sseq(rows), (0, 2, 1)), ((0, 0), (0, 0), (0, LANES - ts)))
    kc_s, vc_s = _compress_paged(page_table, feat(cache_cmp_kv[0]), sseq(fs['kvc']), consts)
    oa_s = _nsa_sample(page_table, sseq(fs['q']), sseq(fs['gate']), kc_s, vc_s, feat(cache_slc_kv[0]),
                       new_tile(fs['kvs']), feat(state_win_kv[0]), new_tile(fs['kvw']))
    win_cat = jnp.concatenate([state_win_kv[0].reshape(db, wb, KV_ROW), sseq(fs['kvw'])], axis=1)
    ob_s, s_s = _hgrn(sseq(fs['qb']), sseq(fs['kb']), sseq(fs['vb']), sseq(fs['lf']), sseq(fs['gb']),
                      state_hgrn[0].astype(F32), consts)
    y_s, conv_s = _merge_ffn(x_sample.reshape(n_s, D_MODEL), oa_s.reshape(n_s, -1), ob_s.reshape(n_s, -1),
                             fs['mg'], state_ffn_conv[0], ts, consts, n_s)

    wkeep = min(WINDOW, t)
    unfeat = lambda a: jnp.transpose(a.reshape(b, *kv_shape, a.shape[-1]), (0, 4, 1, 2, 3))[None]
    return (y_p.reshape(b, t, D_MODEL), y_s.reshape(db, ts, D_MODEL),
            unfeat(fp['kvc_t']), fs['kvc'].reshape(1, db, ts, *kv_shape),
            unfeat(fp['kvs_t']), fs['kvs'].reshape(1, db, ts, *kv_shape),
            unfeat(fp['kvw_t'][:, :, t - wkeep:]),
            win_cat[:, ts:].reshape(1, db, wb, *kv_shape),
            s_p[None], s_s[None], conv_p[None], conv_s[None])
```

```python
import functools

import numpy as np
import jax
import jax.numpy as jnp
from jax import lax
from jax.experimental import pallas as pl
from jax.experimental.pallas import tpu as pltpu

F32 = jnp.float32
BF16 = jnp.bfloat16

D_MODEL = 1024
PAGE_SIZE = 128
A_HEADS = 8
A_KV_GROUPS = 2
A_HPG = A_HEADS // A_KV_GROUPS
A_HEAD_DIM = 64
CMP_BLOCK = 32
CMP_STRIDE = 16
SLC_BLOCK = 64
N_SELECT = 16
WINDOW = 512
Q_BLOCK = 128
ROPE_THETA = 10000.0
FORCE_SCORE = 1e4
NEG_INF = -1e30
B_HEADS = 4
B_KEY_DIM = 128
D_FF = 2816
FFN_CONV = 3
EPS = 1e-6

LANES = 128
SUBLANES = 8
KV_ROW = 2 * A_KV_GROUPS * A_HEAD_DIM
S_TILE = 256
W_TILE = 128
SEL_TK = 512
MASK_BIG = 1e30
SCORE_SAFE = 40.0
CHUNK_PITCH = 20
LAND_GROUP = 8
CMP_GROUP = 4
CMP_W1_SHAPE = (CMP_BLOCK // CMP_GROUP, CMP_GROUP * KV_ROW, KV_ROW)
FFN_CHUNK = 1408
HGRN_CHUNK = 128
HGRN_LEVELS = 7
HGRN_STEP_CHUNKS = 4
HGRN_SHORT_ROWS = 4
VMEM_LIMIT = 56 * 1024 * 1024

OFF_Q, OFF_KVC, OFF_KVS, OFF_KVW = 0, 512, 768, 1024
OFF_QB, OFF_FB, OFF_IB, OFF_GB, OFF_MG, OFF_GATE = 1280, 1792, 2304, 2816, 3328, 5376


def _dot(a, b):
    return jnp.dot(a, b, preferred_element_type=F32)


def _dot_nt(a, b):
    return lax.dot_general(a, b, (((1,), (1,)), ((), ())), preferred_element_type=F32)


def _dot_tn(a, b):
    return lax.dot_general(a, b, (((0,), (0,)), ((), ())), preferred_element_type=F32)


def _split_bf16(x):
    hi = x.astype(BF16)
    lo = (x - hi.astype(F32)).astype(BF16)
    return hi, lo


def _round_up(n, m):
    return -(-n // m) * m


def _const_spec(shape):
    nd = len(shape)
    return pl.BlockSpec(shape, lambda *_: (0,) * nd, pipeline_mode=pl.Buffered(1))


def _params(semantics):
    return pltpu.CompilerParams(dimension_semantics=semantics, vmem_limit_bytes=VMEM_LIMIT)


INPROJ_COMMON = ('q', 'gate', 'qb', 'kb', 'vb', 'lf', 'gb', 'mg')
INPROJ_PROMPT = INPROJ_COMMON + ('kvc', 'kvc_t', 'kvs_t', 'kvw_t', 'ks_bf', 'vs_bf', 'kw_bf', 'vw_bf')
INPROJ_SAMPLE = INPROJ_COMMON + ('kvc', 'kvs', 'kvw')


def _inproj_kernel(x_ref, g_ref, wa_ref, wb_ref, wg_ref, cos_ref, sin_ref, qg_ref, kg_ref, lbl_ref, mseg_ref,
                   *out_refs, names):
    o = dict(zip(names, out_refs))
    q_ref, gate_ref, mg_ref = o['q'], o['gate'], o['mg']
    qb_ref, kb_ref, vb_ref, lf_ref, gb_ref = o['qb'], o['kb'], o['vb'], o['lf'], o['gb']
    x = x_ref[...]
    ms = jnp.mean(x * x, axis=-1, keepdims=True)
    h = (x * lax.rsqrt(ms + EPS) * g_ref[...]).astype(BF16)
    cos = cos_ref[...]
    sin = sin_ref[...]
    tm = x.shape[0]
    lane = lax.broadcasted_iota(jnp.int32, (tm, LANES), 1)
    first_half = (lane & (A_HEAD_DIM // 2)) == 0
    low64 = lane < A_HEAD_DIM
    mseg = mseg_ref[...]

    def proj(lo, width):
        for start, ref in ((OFF_GATE, wg_ref), (OFF_QB, wb_ref), (OFF_Q, wa_ref)):
            if lo >= start:
                return _dot(h, ref[:, lo - start:lo - start + width])

    def head_norm_rope(chunk, gain):
        s_hi, s_lo = _split_bf16(chunk * chunk)
        mean = _dot(jnp.concatenate([s_hi, s_lo], axis=-1), mseg)
        y = chunk * lax.rsqrt(mean + EPS) * gain
        rot = jnp.where(first_half, pltpu.roll(y, LANES - A_HEAD_DIM // 2, 1),
                        pltpu.roll(y, A_HEAD_DIM // 2, 1))
        return y * cos + rot * sin

    zq = proj(OFF_Q, A_HEADS * A_HEAD_DIM)
    z_kv = [proj(off, KV_ROW) for off in (OFF_KVC, OFF_KVS, OFF_KVW)]
    qg = qg_ref[...]

    def emit_q(k):
        c = head_norm_rope(zq[:, k * LANES:(k + 1) * LANES], qg) * (A_HEAD_DIM ** -0.5)
        r = pltpu.roll(c, A_HEAD_DIM, 1)
        if k < A_HPG // 2:
            h0, h1 = jnp.where(low64, c, 0.0), jnp.where(low64, r, 0.0)
        else:
            h0, h1 = jnp.where(low64, 0.0, r), jnp.where(low64, 0.0, c)
        q_ref[:, (2 * k) * LANES:(2 * k + 1) * LANES] = h0.astype(BF16)
        q_ref[:, (2 * k + 1) * LANES:(2 * k + 2) * LANES] = h1.astype(BF16)

    def emit_kv(i, name):
        z = z_kv[i]
        rows = jnp.concatenate([head_norm_rope(z[:, :LANES], kg_ref[i:i + 1, :]), z[:, LANES:]], axis=-1)
        if name in o:
            o[name][...] = rows
        if name + '_t' in o:
            rows_t = rows.T
            o[name + '_t'][0] = rows_t
            for half, tag in ((rows_t[:LANES], 'k'), (rows_t[LANES:], 'v')):
                key = tag + name[-1] + '_bf'
                if key in o:
                    n_tiles, _, width = o[key].shape[1:]
                    for j in range(n_tiles):
                        o[key][0, j] = half[:, j * width:(j + 1) * width].astype(BF16)

    lbl = lbl_ref[...]
    e = jnp.exp(lbl - jnp.max(lbl, axis=0, keepdims=True))
    lb = e[0:1, :] / jnp.sum(e, axis=0, keepdims=True)

    emit_q(0)
    fz = proj(OFF_FB, B_HEADS * B_KEY_DIM)
    lf_ref[...] = jnp.log(lb + (1.0 - lb) * jax.nn.sigmoid(fz))
    kb_ref[...] = (1.0 - lb) * jax.nn.sigmoid(-fz)
    emit_q(1)
    zqb = proj(OFF_QB, B_HEADS * B_KEY_DIM)
    qb_ref[...] = zqb * jax.nn.sigmoid(zqb)
    emit_q(2)
    vb_ref[...] = proj(OFF_IB, B_HEADS * B_KEY_DIM)
    emit_q(3)
    zg = proj(OFF_GB, B_HEADS * B_KEY_DIM)
    gb_ref[...] = zg * jax.nn.sigmoid(zg)
    emit_kv(0, 'kvc')
    mg_ref[:, :D_MODEL] = jax.nn.sigmoid(proj(OFF_MG, D_MODEL))
    emit_kv(1, 'kvs')
    mg_ref[:, D_MODEL:] = jax.nn.sigmoid(proj(OFF_MG + D_MODEL, D_MODEL))
    emit_kv(2, 'kvw')
    gate_ref[...] = jax.nn.sigmoid(proj(OFF_GATE, LANES))


def _inproj(x2d, cos_tab, sin_tab, consts, tm, seq_len=None):
    n = x2d.shape[0]
    tab_blocks = cos_tab.shape[0] // tm
    row = lambda w: pl.BlockSpec((tm, w), lambda i: (i, 0))
    tab = pl.BlockSpec((tm, LANES), lambda i: (i % tab_blocks, 0))
    token_major = {'q': (A_HEADS * LANES, BF16), 'gate': (LANES, F32), 'qb': (512, F32), 'kb': (512, F32),
                   'vb': (512, F32), 'lf': (512, F32), 'gb': (512, F32), 'mg': (2 * D_MODEL, F32),
                   'kvc': (KV_ROW, F32), 'kvs': (KV_ROW, F32), 'kvw': (KV_ROW, F32)}
    names = INPROJ_SAMPLE if seq_len is None else INPROJ_PROMPT
    specs, shapes = [], []
    for name in names:
        if name in token_major:
            w, dt = token_major[name]
            specs.append(row(w))
            shapes.append(jax.ShapeDtypeStruct((n, w), dt))
        elif name.endswith('_t'):
            per_seq = seq_len // tm
            specs.append(pl.BlockSpec((1, KV_ROW, tm), lambda i: (i // per_seq, 0, i % per_seq)))
            shapes.append(jax.ShapeDtypeStruct((n // seq_len, KV_ROW, seq_len), F32))
        else:
            width = S_TILE if name[1] == 's' else W_TILE
            per_seq = seq_len // tm
            specs.append(pl.BlockSpec((1, tm // width, LANES, width), lambda i: (i // per_seq, i % per_seq, 0, 0)))
            shapes.append(jax.ShapeDtypeStruct((n // seq_len, seq_len // width, LANES, width), BF16))
    outs = pl.pallas_call(
        functools.partial(_inproj_kernel, names=names),
        grid=(n // tm,),
        in_specs=[row(D_MODEL), _const_spec((1, D_MODEL)), _const_spec((D_MODEL, OFF_QB)),
                  _const_spec((D_MODEL, OFF_GATE - OFF_QB)), _const_spec((D_MODEL, LANES)), tab, tab,
                  _const_spec((1, LANES)), _const_spec((3, LANES)),
                  _const_spec(consts['lbl'].shape), _const_spec((2 * LANES, LANES))],
        out_specs=specs,
        out_shape=shapes,
        compiler_params=_params(("parallel",)),
        name="inproj",
    )(x2d, consts['attn_g'], consts['w_attn'], consts['w_rest'], consts['w_gate'], cos_tab, sin_tab,
      consts['q_g'], consts['k_g'],
      consts['lbl'], consts['mseg'])
    return dict(zip(names, outs))


def _chunk_row(chunk):
    return chunk * CHUNK_PITCH


def _compress_rows(tok_k, tok_v, pe_ref, w1_ref, w2_ref, kc_ref, vc_ref, m_rows, n_cmp):
    acc = jnp.zeros((m_rows, KV_ROW), F32)
    for p0 in range(0, CMP_BLOCK, CMP_GROUP):
        xs = []
        for p in range(p0, p0 + CMP_GROUP):
            start = _chunk_row(p // CMP_STRIDE) + p % CMP_STRIDE
            xp = jnp.concatenate([tok_k[pl.ds(start, m_rows, stride=CHUNK_PITCH), :],
                                  tok_v[pl.ds(start, m_rows, stride=CHUNK_PITCH), :]], axis=-1) + pe_ref[p:p + 1, :]
            xs.append(xp.astype(BF16))
        acc = acc + _dot(jnp.concatenate(xs, axis=-1), w1_ref[p0 // CMP_GROUP])
    hid = acc * jax.nn.sigmoid(acc)
    out = _dot(hid.astype(BF16), w2_ref[...])
    row = lax.broadcasted_iota(jnp.int32, out.shape, 0)
    out = jnp.where(row < n_cmp, out, 0.0)
    ncp = kc_ref.shape[1]
    kc_ref[0, 0:m_rows, :] = out[:, :LANES].astype(BF16)
    vc_ref[0, 0:m_rows, :] = out[:, LANES:].astype(BF16)
    if ncp > m_rows:
        kc_ref[0, m_rows:ncp, :] = jnp.zeros((ncp - m_rows, LANES), BF16)
        vc_ref[0, m_rows:ncp, :] = jnp.zeros((ncp - m_rows, LANES), BF16)


def _zero_chunks(tok_k, tok_v, first_chunk):
    r0 = _chunk_row(first_chunk)
    tok_k[r0:, :] = jnp.zeros((tok_k.shape[0] - r0, LANES), F32)
    tok_v[r0:, :] = jnp.zeros((tok_v.shape[0] - r0, LANES), F32)


def _compress_dense_kernel(rows_ref, pe_ref, w1_ref, w2_ref, kc_ref, vc_ref, tok_k, tok_v, *, m_rows, n_cmp):
    n_chunks = rows_ref.shape[1] // CMP_STRIDE

    def body(c, carry):
        src = pl.multiple_of(c * CMP_STRIDE, CMP_STRIDE)
        dst = pl.multiple_of(_chunk_row(c), int(np.gcd(CHUNK_PITCH, SUBLANES)))
        tok_k[pl.ds(dst, CMP_STRIDE), :] = rows_ref[0, pl.ds(src, CMP_STRIDE), 0:LANES]
        tok_v[pl.ds(dst, CMP_STRIDE), :] = rows_ref[0, pl.ds(src, CMP_STRIDE), LANES:KV_ROW]
        return carry

    lax.fori_loop(0, n_chunks, body, 0)
    _zero_chunks(tok_k, tok_v, n_chunks)
    _compress_rows(tok_k, tok_v, pe_ref, w1_ref, w2_ref, kc_ref, vc_ref, m_rows, n_cmp)


def _page_copy(pool_ref, page, stage, slot, sems):
    return pltpu.make_async_copy(pool_ref.at[page], stage.at[slot], sems.at[slot])


def _compress_paged_kernel(pt_ref, pool_ref, new_ref, pe_ref, w1_ref, w2_ref, kc_ref, vc_ref,
                           stage, tok_k, tok_v, sems, *, n_pages, m_rows, n_cmp):
    b = pl.program_id(0)

    def start_row(row):
        def start(p, carry):
            _page_copy(pool_ref, pt_ref[row, p], stage, p, sems).start()
            return carry
        lax.fori_loop(0, n_pages, start, 0)

    @pl.when(b == 0)
    def _():
        start_row(0)

    chunks_per_page = PAGE_SIZE // CMP_STRIDE
    first_new = n_pages * chunks_per_page
    _zero_chunks(tok_k, tok_v, first_new)
    ts = new_ref.shape[1]
    assert ts <= CMP_STRIDE
    tok_k[_chunk_row(first_new):_chunk_row(first_new) + ts, :] = new_ref[0, :, 0:LANES]
    tok_v[_chunk_row(first_new):_chunk_row(first_new) + ts, :] = new_ref[0, :, LANES:KV_ROW]

    group = int(np.gcd(n_pages, LAND_GROUP))

    def land(i, carry):
        for j in range(group):
            _page_copy(pool_ref, 0, stage, i * group + j, sems).wait()
        for j in range(group):
            p = i * group + j
            page_t = stage[p].T
            base = pl.multiple_of(_chunk_row(p * chunks_per_page), SUBLANES)
            for c in range(chunks_per_page):
                rows = slice(c * CMP_STRIDE, (c + 1) * CMP_STRIDE)
                tok_k[pl.ds(base + _chunk_row(c), CMP_STRIDE), :] = page_t[rows, 0:LANES]
                tok_v[pl.ds(base + _chunk_row(c), CMP_STRIDE), :] = page_t[rows, LANES:KV_ROW]
        return carry

    lax.fori_loop(0, n_pages // group, land, 0)

    @pl.when(b + 1 < pl.num_programs(0))
    def _():
        start_row(b + 1)

    _compress_rows(tok_k, tok_v, pe_ref, w1_ref, w2_ref, kc_ref, vc_ref, m_rows, n_cmp)


def _compress_geometry(t_real):
    t_pad = _round_up(t_real, SLC_BLOCK)
    n_cmp = t_pad // CMP_STRIDE - CMP_BLOCK // CMP_STRIDE + 1
    m_rows = _round_up(n_cmp, SUBLANES)
    ncp = _round_up(n_cmp, LANES)
    n_chunks = max(m_rows + CMP_BLOCK // CMP_STRIDE - 1, -(-t_real // CMP_STRIDE))
    return n_cmp, m_rows, ncp, _chunk_row(n_chunks)


def _compress_dense(rows, consts):
    b, t, _ = rows.shape
    n_cmp, m_rows, ncp, tok_rows = _compress_geometry(t)
    return pl.pallas_call(
        functools.partial(_compress_dense_kernel, m_rows=m_rows, n_cmp=n_cmp),
        grid=(b,),
        in_specs=[pl.BlockSpec((1, t, KV_ROW), lambda i: (i, 0, 0)),
                  _const_spec((CMP_BLOCK, KV_ROW)), _const_spec(CMP_W1_SHAPE),
                  _const_spec((KV_ROW, KV_ROW))],
        out_specs=[pl.BlockSpec((1, ncp, LANES), lambda i: (i, 0, 0))] * 2,
        out_shape=[jax.ShapeDtypeStruct((b, ncp, LANES), BF16)] * 2,
        scratch_shapes=[pltpu.VMEM((tok_rows, LANES), F32)] * 2,
        compiler_params=_params(("parallel",)),
        name="compress_dense",
    )(rows, consts['cmp_pe'], consts['cmp_w1'], consts['cmp_w2'])


def _compress_paged(page_table, pool, new_rows, consts):
    db, n_pages = page_table.shape
    ts = new_rows.shape[1]
    n_cmp, m_rows, ncp, tok_rows = _compress_geometry(n_pages * PAGE_SIZE + ts)
    cm = lambda nd: (lambda i, pt: (0,) * nd)
    grid_spec = pltpu.PrefetchScalarGridSpec(
        num_scalar_prefetch=1,
        grid=(db,),
        in_specs=[pl.BlockSpec(memory_space=pl.ANY),
                  pl.BlockSpec((1, ts, KV_ROW), lambda i, pt: (i, 0, 0)),
                  pl.BlockSpec((CMP_BLOCK, KV_ROW), cm(2), pipeline_mode=pl.Buffered(1)),
                  pl.BlockSpec(CMP_W1_SHAPE, cm(3), pipeline_mode=pl.Buffered(1)),
                  pl.BlockSpec((KV_ROW, KV_ROW), cm(2), pipeline_mode=pl.Buffered(1))],
        out_specs=[pl.BlockSpec((1, ncp, LANES), lambda i, pt: (i, 0, 0))] * 2,
        scratch_shapes=[pltpu.VMEM((n_pages, KV_ROW, PAGE_SIZE), F32), pltpu.VMEM((tok_rows, LANES), F32),
                        pltpu.VMEM((tok_rows, LANES), F32), pltpu.SemaphoreType.DMA((n_pages,))],
    )
    return pl.pallas_call(
        functools.partial(_compress_paged_kernel, n_pages=n_pages, m_rows=m_rows, n_cmp=n_cmp),
        grid_spec=grid_spec,
        out_shape=[jax.ShapeDtypeStruct((db, ncp, LANES), BF16)] * 2,
        compiler_params=_params(("arbitrary",)),
        name="compress_paged",
    )(page_table, pool, new_rows, consts['cmp_pe'], consts['cmp_w1'], consts['cmp_w2'])


def _softmax_rows(s, mask):
    s = jnp.where(mask, s, NEG_INF)
    p = jnp.where(mask, jnp.exp(s - jnp.max(s, axis=-1, keepdims=True)), 0.0)
    return p / jnp.maximum(jnp.sum(p, axis=-1, keepdims=True), 1e-30)


def _online_init(rows):
    return (jnp.full((rows, 1), NEG_INF, F32), jnp.zeros((rows, 1), F32), jnp.zeros((rows, LANES), F32))


def _online_finish(carry):
    _, l, acc = carry
    return acc / jnp.maximum(l, 1e-30)


def _block_scores(p_sum, cover_ref, qpos, n_blocks):
    hi, lo = _split_bf16(p_sum)
    imp = _dot(jnp.concatenate([hi, lo], axis=-1), cover_ref[...])
    blk = lax.broadcasted_iota(jnp.int32, imp.shape, 1)
    cur = qpos // SLC_BLOCK
    forced = (blk == 0) | (blk == cur) | (blk == cur - 1)
    score = jnp.where(forced, FORCE_SCORE, jnp.where(blk <= cur, imp, -1.0))
    return jnp.where(blk < n_blocks, score, -2.0)


def _topk_select(score, n_blocks, n_sel):
    n_tiles = score.shape[1] // LANES
    tiles = [score[:, t * LANES:(t + 1) * LANES] for t in range(n_tiles)]
    lane = lax.broadcasted_iota(jnp.int32, tiles[0].shape, 1)
    cnt = [jnp.zeros(tiles[0].shape, F32) for _ in range(n_tiles)]
    for s in range(n_blocks):
        col = score[:, s:s + 1]
        for t in range(n_tiles):
            if t * LANES > s:
                beats = col >= tiles[t]
            elif (t + 1) * LANES - 1 < s:
                beats = col > tiles[t]
            else:
                beats = (col > tiles[t]) | ((col == tiles[t]) & (lane + t * LANES > s))
            cnt[t] = cnt[t] + jnp.where(beats, 1.0, 0.0)
    blk = lax.broadcasted_iota(jnp.int32, score.shape, 1)
    return jnp.where((jnp.concatenate(cnt, axis=1) < n_sel) & (blk < n_blocks), 1.0, 0.0)


def _assemble_heads(heads, low64):
    chunks = []
    for k in range(A_HEADS // 2):
        a, b = heads[2 * k], heads[2 * k + 1]
        if k < A_HPG // 2:
            chunks.append(jnp.where(low64, a, pltpu.roll(b, A_HEAD_DIM, 1)))
        else:
            chunks.append(jnp.where(low64, pltpu.roll(a, A_HEAD_DIM, 1), b))
    return chunks


def _topk_rows(score_t, n_blocks, n_sel):
    n_tiles = score_t.shape[0] // SUBLANES
    tiles = [score_t[t * SUBLANES:(t + 1) * SUBLANES] for t in range(n_tiles)]
    sub = lax.broadcasted_iota(jnp.int32, tiles[0].shape, 0)
    cnt = [jnp.zeros(tiles[0].shape, F32) for _ in range(n_tiles)]
    for s in range(n_blocks):
        row = score_t[s:s + 1, :]
        for t in range(n_tiles):
            if t * SUBLANES > s:
                beats = row >= tiles[t]
            elif (t + 1) * SUBLANES - 1 < s:
                beats = row > tiles[t]
            else:
                later = sub + t * SUBLANES > s
                beats = (row > tiles[t]) | ((row == tiles[t]) & later)
            cnt[t] = cnt[t] + jnp.where(beats, 1.0, 0.0)
    blk = lax.broadcasted_iota(jnp.int32, score_t.shape, 0)
    return jnp.where((jnp.concatenate(cnt, axis=0) < n_sel) & (blk < n_blocks), 1.0, 0.0)


def _online_step_t(carry, s, v_t):
    m, l, acc = carry
    m_new = jnp.maximum(m, jnp.max(s, axis=-1, keepdims=True))
    alpha = jnp.exp(m - m_new)
    p = jnp.exp(s - m_new)
    l = alpha * l + jnp.sum(p, axis=-1, keepdims=True)
    acc = alpha * acc + _dot_nt(p.astype(BF16), v_t)
    return m_new, l, acc


def _nsa_prompt_kernel(bound_ref, q_ref, gate_ref, kc_ref, vc_ref, ks_ref, vs_ref, kw_ref, vw_ref, cover_ref,
                       exp_ref, o_ref, *, n_blocks, n_sel):
    i = pl.program_id(1)
    s0 = i * Q_BLOCK
    q = q_ref[0]
    gates = gate_ref[0]
    rows = A_HEADS * Q_BLOCK
    q_all = jnp.concatenate([q[:, h * LANES:(h + 1) * LANES] for h in range(A_HEADS)], axis=0)
    qpos1 = s0 + lax.broadcasted_iota(jnp.int32, (Q_BLOCK, 1), 0)
    qpos_all = jnp.concatenate([qpos1] * A_HEADS, axis=0)

    ncp = kc_ref.shape[1]
    c_end = lax.broadcasted_iota(jnp.int32, (1, ncp), 1) * CMP_STRIDE + (CMP_BLOCK - 1)
    p_c = _softmax_rows(_dot_nt(q_all, kc_ref[0]), c_end <= qpos_all)
    o_c = _dot(p_c.astype(BF16), vc_ref[0])

    cur = (s0 + lax.broadcasted_iota(jnp.int32, (1, Q_BLOCK), 1)) // SLC_BLOCK
    blk = lax.broadcasted_iota(jnp.int32, (LANES, Q_BLOCK), 0)
    forced = (blk == 0) | (blk == cur) | (blk == cur - 1)
    scores = []
    for g in range(A_KV_GROUPS):
        base = g * A_HPG * Q_BLOCK
        p_sum = p_c[base:base + Q_BLOCK]
        for hh in range(1, A_HPG):
            p_sum = p_sum + p_c[base + hh * Q_BLOCK:base + (hh + 1) * Q_BLOCK]
        hi, lo = _split_bf16(p_sum)
        imp_t = _dot(jnp.concatenate([hi, lo], axis=-1), cover_ref[...]).T
        score = jnp.where(forced, FORCE_SCORE, jnp.where(blk <= cur, imp_t, -1.0))
        scores.append(jnp.where(blk < n_blocks, score, -2.0))
    nb8 = _round_up(n_blocks, SUBLANES)
    sel_t = _topk_rows(jnp.concatenate(scores, axis=1)[0:nb8], n_blocks, n_sel)
    if nb8 < LANES:
        sel_t = jnp.concatenate([sel_t, jnp.zeros((LANES - nb8, sel_t.shape[1]), F32)], axis=0)

    aug = []
    for g in range(A_KV_GROUPS):
        sel = sel_t[:, g * Q_BLOCK:(g + 1) * Q_BLOCK].T
        aug += [((sel - 1.0) * MASK_BIG).astype(BF16)] * A_HPG
    q_aug = jnp.concatenate([q_all, jnp.concatenate(aug, axis=0)], axis=1)
    per_tile = SEL_TK // S_TILE
    cat = lambda kt, ref_tile: jnp.concatenate([ref_tile(kt * per_tile + j) for j in range(per_tile)], axis=1)

    n_full = s0 // SEL_TK
    kpos = n_full * SEL_TK + lax.broadcasted_iota(jnp.int32, (1, SEL_TK), 1)
    causal = jnp.concatenate([jnp.where(kpos <= qpos1, 0.0, NEG_INF)] * A_HEADS, axis=0)

    st = jnp.maximum(i - WINDOW // Q_BLOCK, 0) * (Q_BLOCK // W_TILE)
    n_wt = (WINDOW + Q_BLOCK) // W_TILE
    d = qpos1 - (st * W_TILE + lax.broadcasted_iota(jnp.int32, (1, n_wt * W_TILE), 1))
    band = jnp.concatenate([jnp.where((d >= 0) & (d < WINDOW), 0.0, NEG_INF)] * A_HEADS, axis=0)

    def attend(shifted):
        half = rows // A_KV_GROUPS

        def weigh(p, v_t):
            ones = jnp.ones((A_HEAD_DIM, v_t.shape[1]), BF16)
            return jnp.concatenate(
                [_dot_nt(p[g * half:(g + 1) * half],
                         jnp.concatenate([v_t[g * A_HEAD_DIM:(g + 1) * A_HEAD_DIM], ones], axis=0))
                 for g in range(A_KV_GROUPS)], axis=0)

        def normalise(acc):
            swapped = pltpu.roll(acc, A_HEAD_DIM, 1)
            lane = lax.broadcasted_iota(jnp.int32, (half, LANES), 1)
            g0 = acc[:half] / jnp.maximum(swapped[:half], 1e-30)
            g1 = swapped[half:] / jnp.maximum(acc[half:], 1e-30)
            return jnp.concatenate([jnp.where(lane < A_HEAD_DIM, g0, 0.0), jnp.where(lane < A_HEAD_DIM, 0.0, g1)],
                                   axis=0)

        def step(state, s, v_t):
            if shifted:
                return _online_step_t(state, s, v_t)
            return state + weigh(jnp.exp(s).astype(BF16), v_t)

        def sel_step(kt, state, bias):
            k_aug = jnp.concatenate([cat(kt, lambda n: ks_ref[0, n]), cat(kt, lambda n: exp_ref[n])], axis=0)
            s = _dot(q_aug, k_aug)
            return step(state, s if bias is None else s + bias, cat(kt, lambda n: vs_ref[0, n]))

        init = _online_init(rows)
        state = lax.fori_loop(0, n_full, lambda kt, carry: sel_step(kt, carry, None), init if shifted else init[2])
        state = sel_step(n_full, state, causal)
        o_sel = _online_finish(state) if shifted else normalise(state)

        kw_t = jnp.concatenate([kw_ref[0, st + j] for j in range(n_wt)], axis=1)
        vw_t = jnp.concatenate([vw_ref[0, st + j] for j in range(n_wt)], axis=1)
        s_w = _dot(q_all, kw_t) + band
        if shifted:
            p_w = jnp.exp(s_w - jnp.max(s_w, axis=-1, keepdims=True))
            o_win = _dot_nt(p_w.astype(BF16), vw_t) / jnp.maximum(jnp.sum(p_w, axis=-1, keepdims=True), 1e-30)
        else:
            o_win = normalise(weigh(jnp.exp(s_w).astype(BF16), vw_t))
        return o_sel, o_win

    o_s, o_w = lax.cond(bound_ref[0] <= SCORE_SAFE, lambda: attend(False), lambda: attend(True))

    heads = []
    for h in range(A_HEADS):
        r = slice(h * Q_BLOCK, (h + 1) * Q_BLOCK)
        heads.append(gates[:, 3 * h:3 * h + 1] * o_c[r] + gates[:, 3 * h + 1:3 * h + 2] * o_s[r]
                     + gates[:, 3 * h + 2:3 * h + 3] * o_w[r])
    low64 = lax.broadcasted_iota(jnp.int32, (Q_BLOCK, LANES), 1) < A_HEAD_DIM
    for k, chunk in enumerate(_assemble_heads(heads, low64)):
        o_ref[0, :, k * LANES:(k + 1) * LANES] = chunk.astype(BF16)


def _cover_matrix(n_cmp, ncp, n_blocks, nsp):
    c = np.arange(ncp)[:, None]
    s = np.arange(nsp)[None, :]
    cover = ((c * CMP_STRIDE < s * SLC_BLOCK + SLC_BLOCK) & (c * CMP_STRIDE + CMP_BLOCK > s * SLC_BLOCK)
             & (c < n_cmp) & (s < n_blocks))
    return jnp.asarray(np.concatenate([cover, cover], axis=0), BF16)


def _expand_matrix(n_keys):
    return (np.arange(n_keys)[None, :] // SLC_BLOCK) == np.arange(LANES)[:, None]


def _expand_tiles(n_keys, tk):
    e = _expand_matrix(n_keys).reshape(LANES, n_keys // tk, tk)
    return jnp.asarray(np.transpose(e, (1, 0, 2)), BF16)


def _nsa_prompt(score_bound, q, gates, kc, vc, ks_bf, vs_bf, kw_bf, vw_bf, t):
    b = q.shape[0]
    ncp = kc.shape[1]
    n_cmp = t // CMP_STRIDE - CMP_BLOCK // CMP_STRIDE + 1
    n_blocks = t // SLC_BLOCK
    assert n_blocks <= LANES and t % SEL_TK == 0 and t >= WINDOW + Q_BLOCK
    cover = _cover_matrix(n_cmp, ncp, n_blocks, LANES)
    expand = _expand_tiles(t, S_TILE)
    per_b = lambda rows, w: pl.BlockSpec((1, rows, w), lambda bi, i: (bi, 0, 0))
    tiles = lambda a: pl.BlockSpec((1,) + a.shape[1:], lambda bi, i: (bi, 0, 0, 0))
    return pl.pallas_call(
        functools.partial(_nsa_prompt_kernel, n_blocks=n_blocks, n_sel=min(N_SELECT, n_blocks)),
        grid=(b, t // Q_BLOCK),
        in_specs=[pl.BlockSpec(memory_space=pltpu.SMEM),
                  pl.BlockSpec((1, Q_BLOCK, A_HEADS * LANES), lambda bi, i: (bi, i, 0)),
                  pl.BlockSpec((1, Q_BLOCK, LANES), lambda bi, i: (bi, i, 0)),
                  per_b(ncp, LANES), per_b(ncp, LANES),
                  tiles(ks_bf), tiles(vs_bf), tiles(kw_bf), tiles(vw_bf),
                  _const_spec(cover.shape), _const_spec(expand.shape)],
        out_specs=pl.BlockSpec((1, Q_BLOCK, A_HEADS * A_HEAD_DIM), lambda bi, i: (bi, i, 0)),
        out_shape=jax.ShapeDtypeStruct((b, t, A_HEADS * A_HEAD_DIM), BF16),
        compiler_params=_params(("parallel", "parallel")),
        name="nsa_prompt",
    )(score_bound, q, gates, kc, vc, ks_bf, vs_bf, kw_bf, vw_bf, cover, expand)


def _key_page_copy(pool_ref, page, bufs, half, slot, sems):
    dst = bufs.at[half, :, pl.ds(pl.multiple_of(slot * PAGE_SIZE, PAGE_SIZE), PAGE_SIZE)]
    return pltpu.make_async_copy(pool_ref.at[page], dst, sems.at[half])


def _nsa_sample_kernel(pt_ref, q_ref, gate_ref, kc_ref, vc_ref, pool_ref, new_ref, win_ref, wnew_ref,
                       cover_ref, exp_ref, o_ref, bufs, sems, *, n_pages, n_blocks, n_sel, key_chunk):
    b = pl.program_id(0)
    half = b % 2

    def start_row(row, into):
        def start(p, carry):
            _key_page_copy(pool_ref, pt_ref[row, p], bufs, into, p, sems).start()
            return carry
        lax.fori_loop(0, n_pages, start, 0)

    @pl.when(b == 0)
    def _():
        start_row(0, 0)

    @pl.when(b + 1 < pl.num_programs(0))
    def _():
        start_row(b + 1, 1 - half)

    buf = bufs.at[half]
    past = n_pages * PAGE_SIZE
    ts = q_ref.shape[1]
    buf[:, past:past + LANES] = new_ref[0]

    q = q_ref[0].astype(F32)
    q_all = jnp.concatenate([q[:, h * LANES:(h + 1) * LANES] for h in range(A_HEADS)], axis=0).astype(BF16)
    rows = A_HEADS * ts
    qpos1 = past + lax.broadcasted_iota(jnp.int32, (ts, 1), 0)
    qpos_all = jnp.concatenate([qpos1] * A_HEADS, axis=0)

    ncp = kc_ref.shape[1]
    c_end = lax.broadcasted_iota(jnp.int32, (1, ncp), 1) * CMP_STRIDE + (CMP_BLOCK - 1)
    p_c = _softmax_rows(_dot_nt(q_all, kc_ref[0]), c_end <= qpos_all)
    o_c = _dot(p_c.astype(BF16), vc_ref[0])

    p_groups = []
    for g in range(A_KV_GROUPS):
        base = g * A_HPG * ts
        p_sum = p_c[base:base + ts]
        for hh in range(1, A_HPG):
            p_sum = p_sum + p_c[base + hh * ts:base + (hh + 1) * ts]
        p_groups.append(p_sum)
    qpos_g = jnp.concatenate([qpos1] * A_KV_GROUPS, axis=0)
    score = _block_scores(jnp.concatenate(p_groups, axis=0), cover_ref, qpos_g, n_blocks)
    sel = _topk_select(score, n_blocks, n_sel).astype(BF16)

    def wait(p, carry):
        _key_page_copy(pool_ref, 0, bufs, half, p, sems).wait()
        return carry

    lax.fori_loop(0, n_pages, wait, 0)

    n_keys = buf.shape[1]
    carry = _online_init(rows)
    for ck in range(-(-n_keys // key_chunk)):
        k0 = ck * key_chunk
        kn = min(key_chunk, n_keys - k0)
        s = _dot(q_all, buf[0:LANES, k0:k0 + kn].astype(BF16))
        blk0 = ck * (key_chunk // SLC_BLOCK)
        em = _dot(sel[:, blk0:blk0 + LANES], exp_ref[:, 0:kn])
        kpos = k0 + lax.broadcasted_iota(jnp.int32, (1, kn), 1)
        bias = jnp.where((em > 0.5) & (kpos <= qpos_g), 0.0, NEG_INF)
        biases = []
        for g in range(A_KV_GROUPS):
            biases += [bias[g * ts:(g + 1) * ts]] * A_HPG
        carry = _online_step_t(carry, s + jnp.concatenate(biases, axis=0),
                               buf[LANES:KV_ROW, k0:k0 + kn].astype(BF16))
    o_s = _online_finish(carry)

    wb = win_ref.shape[2]
    kv_w = jnp.concatenate([win_ref[0], wnew_ref[0]], axis=1).astype(BF16)
    idx = lax.broadcasted_iota(jnp.int32, (1, wb + LANES), 1)
    w_pos = jnp.where(idx < wb, past - wb + idx, past + idx - wb)
    d = qpos_all - w_pos
    m_w = (d >= 0) & (d < WINDOW) & (w_pos >= 0)
    p_w = _softmax_rows(_dot(q_all, kv_w[0:LANES]), m_w)
    o_w = _dot_nt(p_w.astype(BF16), kv_w[LANES:KV_ROW])

    gates = gate_ref[0]
    heads = []
    for h in range(A_HEADS):
        r = slice(h * ts, (h + 1) * ts)
        heads.append(gates[:, 3 * h:3 * h + 1] * o_c[r] + gates[:, 3 * h + 1:3 * h + 2] * o_s[r]
                     + gates[:, 3 * h + 2:3 * h + 3] * o_w[r])
    low64 = lax.broadcasted_iota(jnp.int32, (ts, LANES), 1) < A_HEAD_DIM
    for k, chunk in enumerate(_assemble_heads(heads, low64)):
        o_ref[0, :, k * LANES:(k + 1) * LANES] = chunk.astype(BF16)


def _nsa_sample(page_table, q, gates, kc, vc, pool, new_tile, win_t, wnew_tile):
    db, n_pages = page_table.shape
    wb = win_t.shape[2]
    assert wb % LANES == 0
    ts = q.shape[1]
    past = n_pages * PAGE_SIZE
    ncp = kc.shape[1]
    t_pad = _round_up(past + ts, SLC_BLOCK)
    n_cmp = t_pad // CMP_STRIDE - CMP_BLOCK // CMP_STRIDE + 1
    n_blocks = t_pad // SLC_BLOCK
    key_chunk = LANES * SLC_BLOCK
    n_keys = past + LANES
    nsp = LANES * (-(-n_keys // key_chunk))
    assert nsp >= n_blocks and ts % SUBLANES == 0
    cover = _cover_matrix(n_cmp, ncp, n_blocks, nsp)
    expand = jnp.asarray(_expand_matrix(min(key_chunk, n_keys)), BF16)
    cm = lambda nd: (lambda i, pt: (0,) * nd)
    per_b = lambda r, w: pl.BlockSpec((1, r, w), lambda i, pt: (i, 0, 0))
    grid_spec = pltpu.PrefetchScalarGridSpec(
        num_scalar_prefetch=1,
        grid=(db,),
        in_specs=[per_b(ts, A_HEADS * LANES), per_b(ts, LANES), per_b(ncp, LANES), per_b(ncp, LANES),
                  pl.BlockSpec(memory_space=pl.ANY), per_b(KV_ROW, LANES), per_b(KV_ROW, wb), per_b(KV_ROW, LANES),
                  pl.BlockSpec(cover.shape, cm(2), pipeline_mode=pl.Buffered(1)),
                  pl.BlockSpec(expand.shape, cm(2), pipeline_mode=pl.Buffered(1))],
        out_specs=per_b(ts, A_HEADS * A_HEAD_DIM),
        scratch_shapes=[pltpu.VMEM((2, KV_ROW, n_keys), F32), pltpu.SemaphoreType.DMA((2,))],
    )
    return pl.pallas_call(
        functools.partial(_nsa_sample_kernel, n_pages=n_pages, n_blocks=n_blocks,
                          n_sel=min(N_SELECT, n_blocks), key_chunk=key_chunk),
        grid_spec=grid_spec,
        out_shape=jax.ShapeDtypeStruct((db, ts, A_HEADS * A_HEAD_DIM), BF16),
        compiler_params=_params(("arbitrary",)),
        name="nsa_sample",
    )(page_table, q, gates, kc, vc, pool, new_tile, win_t, wnew_tile, cover, expand)


def _hgrn_matrices():
    c = HGRN_CHUNK
    t = np.arange(c)[:, None]
    u = np.arange(c)[None, :]
    mats = [u <= t]
    masks = [t == u]
    for lvl in range(HGRN_LEVELS):
        m = 1 << lvl
        mid = (t // (2 * m)) * (2 * m) + m - 1
        mats.append((u > mid) & (u <= t))
        mats.append((u > t) & (u <= mid))
        masks.append((t // (2 * m) == u // (2 * m)) & (t % (2 * m) >= m) & (u % (2 * m) < m))
    mats.append(u > t)
    pm = np.concatenate(mats, axis=0)
    return (jnp.asarray(np.concatenate([pm, pm], axis=1), BF16),
            jnp.asarray(np.stack(masks).astype(np.float32)))


def _hgrn_kernel(qb_ref, kb_ref, vb_ref, lf_ref, gb_ref, s0_ref, gn_ref, pm_ref, lm_ref,
                 ob_ref, sout_ref, st_scr, pad_scr):
    j = pl.program_id(1)
    c = HGRN_CHUNK
    n_rows, t_blk = qb_ref.shape[0], qb_ref.shape[1]

    @pl.when(j == 0)
    def _():
        for r in range(n_rows):
            for h in range(B_HEADS):
                st_scr[r * B_HEADS + h] = s0_ref[r, h].T

    def load(ref, r, slot, rows):
        if t_blk >= c:
            return ref[r, rows, :]
        pad_scr[4 * r + slot] = jnp.zeros((c, B_HEADS * B_KEY_DIM), F32)
        pad_scr[4 * r + slot, 0:t_blk, :] = ref[r]
        return pad_scr[4 * r + slot]

    gn = gn_ref[...]
    n_sub = max(t_blk // c, 1)
    prepared = {}
    for r in range(n_rows):
        for sub in range(n_sub):
            rows = slice(sub * c, (sub + 1) * c)
            qb, kb, vb, lf = (load(ref, r, slot, rows) for slot, ref in enumerate((qb_ref, kb_ref, vb_ref, lf_ref)))
            lf_hi, lf_lo = _split_bf16(lf)
            ex = _dot(pm_ref[...], jnp.concatenate([lf_hi, lf_lo], axis=0))
            for h in range(B_HEADS):
                sl = slice(h * B_KEY_DIM, (h + 1) * B_KEY_DIM)
                q, k = qb[:, sl], kb[:, sl]
                v = vb[:, sl].astype(BF16)
                b_cum = ex[0:c, sl]
                a = lm_ref[0] * _dot_nt(q.astype(BF16), k.astype(BF16))
                for lvl in range(HGRN_LEVELS):
                    eq = ex[(2 * lvl + 1) * c:(2 * lvl + 2) * c, sl]
                    ek = ex[(2 * lvl + 2) * c:(2 * lvl + 3) * c, sl]
                    a = a + lm_ref[lvl + 1] * _dot_nt((q * jnp.exp(eq)).astype(BF16),
                                                      (k * jnp.exp(ek)).astype(BF16))
                e_end = ex[(2 * HGRN_LEVELS + 1) * c:(2 * HGRN_LEVELS + 2) * c, sl]
                prepared[r, sub, h] = ((q * jnp.exp(b_cum)).astype(BF16), _dot(a.astype(BF16), v),
                                       jnp.exp(b_cum[c - 1:c, :]), _dot_tn(v, (k * jnp.exp(e_end)).astype(BF16)))
    for r in range(n_rows):
        for h in range(B_HEADS):
            sl = slice(h * B_KEY_DIM, (h + 1) * B_KEY_DIM)
            st = st_scr[r * B_HEADS + h]
            for sub in range(n_sub):
                out_rows = slice(sub * c, (sub + 1) * c) if t_blk >= c else slice(0, t_blk)
                q_dec, o_intra, decay_end, kv_end = prepared[r, sub, h]
                o = _dot_nt(q_dec, st.astype(BF16)) + o_intra
                st = st * decay_end + kv_end
                y = o * lax.rsqrt(jnp.mean(o * o, axis=-1, keepdims=True) + EPS) * gn
                ob_ref[r, out_rows, sl] = y[0:min(t_blk, c)] * gb_ref[r, out_rows, sl]
            st_scr[r * B_HEADS + h] = st

    @pl.when(j == pl.num_programs(1) - 1)
    def _():
        for r in range(n_rows):
            for h in range(B_HEADS):
                sout_ref[r, h] = st_scr[r * B_HEADS + h].T


def _hgrn(qb, kb, vb, lf, gb, s0, consts):
    b, t, w = qb.shape
    t_blk = min(t, HGRN_STEP_CHUNKS * HGRN_CHUNK)
    assert t % t_blk == 0 and (t_blk % HGRN_CHUNK == 0 or t == t_blk < HGRN_CHUNK)
    n_rows = HGRN_SHORT_ROWS if (t < HGRN_CHUNK and b % HGRN_SHORT_ROWS == 0) else 1
    tok = pl.BlockSpec((n_rows, t_blk, w), lambda bi, j: (bi, j, 0))
    st = pl.BlockSpec((n_rows, B_HEADS, B_KEY_DIM, B_KEY_DIM), lambda bi, j: (bi, 0, 0, 0))
    return pl.pallas_call(
        _hgrn_kernel,
        grid=(b // n_rows, t // t_blk),
        in_specs=[tok, tok, tok, tok, tok, st, _const_spec((1, B_KEY_DIM)),
                  _const_spec(consts['hgrn_pm'].shape), _const_spec(consts['hgrn_lm'].shape)],
        out_specs=[tok, st],
        out_shape=[jax.ShapeDtypeStruct((b, t, w), F32),
                   jax.ShapeDtypeStruct((b, B_HEADS, B_KEY_DIM, B_KEY_DIM), F32)],
        scratch_shapes=[pltpu.VMEM((n_rows * B_HEADS, B_KEY_DIM, B_KEY_DIM), F32),
                        pltpu.VMEM((4 * n_rows, HGRN_CHUNK, w), F32)],
        compiler_params=_params(("parallel", "arbitrary")),
        name="hgrn",
    )(qb, kb, vb, lf, gb, s0, consts['hgrn_g'], consts['hgrn_pm'], consts['hgrn_lm'])


def _merge_ffn_kernel(x_ref, oa_ref, ob_ref, mg_ref, p1_ref, p2_ref, wa_ref, wb_ref, wo_ref, fg_ref, win_ref,
                      cw_ref, cb_ref, wout_ref, y_ref, a_ref, carry_scr, *, seq_len):
    tm = x_ref.shape[0]
    mg = mg_ref[...]
    m = (mg[:, :D_MODEL] * _dot(oa_ref[...], wa_ref[...])
         + mg[:, D_MODEL:] * _dot(ob_ref[...].astype(BF16), wb_ref[...]))
    x2 = x_ref[...] + _dot(m.astype(BF16), wo_ref[...])
    h = (x2 * lax.rsqrt(jnp.mean(x2 * x2, axis=-1, keepdims=True) + EPS) * fg_ref[...]).astype(BF16)
    row = lax.broadcasted_iota(jnp.int32, (tm, 1), 0)
    carried = seq_len >= tm
    if carried:
        j = pl.program_id(0) % (seq_len // tm)

        @pl.when(j == 0)
        def _():
            carry_scr[0:2, :] = p1_ref[0]
    else:
        t = row % seq_len

    y = x2
    chunks = range(0, D_FF, FFN_CHUNK)
    pre = [(_dot(h, win_ref[:, lo:lo + FFN_CHUNK]), _dot(h, win_ref[:, D_FF + lo:D_FF + lo + FFN_CHUNK]))
           for lo in chunks]
    for lo, (a, gate) in zip(chunks, pre):
        cols = slice(lo, lo + FFN_CHUNK)
        if carried:
            prev = carry_scr[:, cols]
            a1 = jnp.where(row == 0, prev[1:2], pltpu.roll(a, 1, 0))
            a2 = jnp.where(row == 0, prev[0:1], jnp.where(row == 1, prev[1:2], pltpu.roll(a, 2, 0)))
            carry_scr[0:2, cols] = a[tm - 2:tm]
            a_ref[0, :, cols] = a[tm - 2:tm]
        else:
            a1 = jnp.where(t == 0, p1_ref[:, cols], pltpu.roll(a, 1, 0))
            a2 = jnp.where(t < 2, p2_ref[:, cols], pltpu.roll(a, 2, 0))
            a_ref[:, cols] = a
        a_conv = cb_ref[:, cols] + a2 * cw_ref[0:1, cols] + a1 * cw_ref[1:2, cols] + a * cw_ref[2:3, cols]
        act = a_conv * jax.nn.sigmoid(a_conv) * gate
        y = y + _dot(act.astype(BF16), wout_ref[cols, :])
    y_ref[...] = y


def _merge_ffn(x2d, oa, ob, mg, conv_state, seq_len, consts, tm):
    n = x2d.shape[0]
    b = n // seq_len
    row = lambda w: pl.BlockSpec((tm, w), lambda i: (i, 0))
    if seq_len >= tm:
        assert seq_len % tm == 0
        per_seq = seq_len // tm
        p1, p2 = conv_state, conv_state
        p_spec = pl.BlockSpec((1, FFN_CONV - 1, D_FF), lambda i: (i // per_seq, 0, 0))
        a_shape = jax.ShapeDtypeStruct((b, FFN_CONV - 1, D_FF), F32)
        a_spec = pl.BlockSpec((1, FFN_CONV - 1, D_FF), lambda i: (i // per_seq, 0, 0))
    else:
        assert tm % seq_len == 0 and seq_len >= FFN_CONV - 1
        zeros = jnp.zeros((b, seq_len - 1, D_FF), F32)
        p1 = jnp.concatenate([conv_state[:, 1:2], zeros], axis=1).reshape(n, D_FF)
        p2 = jnp.concatenate([conv_state, zeros[:, 1:]], axis=1).reshape(n, D_FF)
        p_spec = row(D_FF)
        a_shape = jax.ShapeDtypeStruct((n, D_FF), F32)
        a_spec = row(D_FF)
    y, a_out = pl.pallas_call(
        functools.partial(_merge_ffn_kernel, seq_len=seq_len),
        grid=(n // tm,),
        in_specs=[row(D_MODEL), row(A_HEADS * A_HEAD_DIM), row(B_HEADS * B_KEY_DIM), row(2 * D_MODEL),
                  p_spec, p_spec,
                  _const_spec((A_HEADS * A_HEAD_DIM, D_MODEL)), _const_spec((B_HEADS * B_KEY_DIM, D_MODEL)),
                  _const_spec((D_MODEL, D_MODEL)), _const_spec((1, D_MODEL)),
                  _const_spec((D_MODEL, 2 * D_FF)), _const_spec((FFN_CONV, D_FF)), _const_spec((1, D_FF)),
                  _const_spec((D_FF, D_MODEL))],
        out_specs=[row(D_MODEL), a_spec],
        out_shape=[jax.ShapeDtypeStruct((n, D_MODEL), F32), a_shape],
        scratch_shapes=[pltpu.VMEM((SUBLANES, D_FF), F32)],
        compiler_params=_params(("arbitrary",)),
        name="merge_ffn",
    )(x2d, oa, ob, mg, p1, p2, consts['w_a'], consts['w_b'], consts['w_out'], consts['ffn_g'],
      consts['ffn_w_in'], consts['conv_w'], consts['conv_b'], consts['ffn_w_out'])
    if seq_len >= tm:
        return y, a_out
    return y, a_out.reshape(b, seq_len, D_FF)[:, seq_len - (FFN_CONV - 1):]


def _prepare_consts(attn_norm_g, w_in, q_norm_g, k_norm_g, cmp_pos_emb, cmp_w1, cmp_w2, hgrn_lb_logits,
                    hgrn_norm_g, w_branch, w_out, ffn_norm_g, ffn_w_in, ffn_conv_w, ffn_conv_b, ffn_w_out):
    n_q = A_HEADS * A_HEAD_DIM
    gate_lo = n_q + 3 * KV_ROW
    gate_hi = gate_lo + 3 * A_HEADS
    w_gate = jnp.pad(w_in[:, gate_lo:gate_hi], ((0, 0), (0, LANES - 3 * A_HEADS))).astype(BF16)
    seg = np.arange(LANES) // A_HEAD_DIM
    mseg = (seg[:, None] == seg[None, :]).astype(np.float32) / A_HEAD_DIM
    eye = jnp.eye(2 * A_KV_GROUPS, dtype=F32)
    jsel = np.repeat(np.arange(2), A_KV_GROUPS)
    w1 = cmp_w1.reshape(2, CMP_BLOCK, A_HEAD_DIM, A_HEAD_DIM)[jsel]
    w1_bd = jnp.einsum('ab,apde->padbe', eye, w1).reshape(CMP_W1_SHAPE).astype(BF16)
    w2_bd = jnp.einsum('ab,ade->adbe', eye, cmp_w2[jsel]).reshape(KV_ROW, KV_ROW).astype(BF16)
    pe = jnp.transpose(cmp_pos_emb[jsel], (1, 0, 2)).reshape(CMP_BLOCK, KV_ROW)
    pm, lm = _hgrn_matrices()
    return {
        'attn_g': attn_norm_g.reshape(1, D_MODEL), 'w_attn': w_in[:, :gate_lo].astype(BF16),
        'w_rest': w_in[:, gate_hi:].astype(BF16), 'w_gate': w_gate,
        'q_g': jnp.tile(q_norm_g, 2).reshape(1, LANES), 'k_g': jnp.tile(k_norm_g, (1, 2)),
        'lbl': hgrn_lb_logits.astype(F32), 'mseg': jnp.asarray(np.concatenate([mseg, mseg], axis=0), BF16),
        'cmp_pe': pe, 'cmp_w1': w1_bd, 'cmp_w2': w2_bd,
        'hgrn_g': hgrn_norm_g.reshape(1, B_KEY_DIM), 'hgrn_pm': pm, 'hgrn_lm': lm,
        'w_a': w_branch[:n_q].astype(BF16), 'w_b': w_branch[n_q:].astype(BF16), 'w_out': w_out.astype(BF16),
        'ffn_g': ffn_norm_g.reshape(1, D_MODEL), 'ffn_w_in': ffn_w_in.astype(BF16),
        'conv_w': ffn_conv_w, 'conv_b': ffn_conv_b.reshape(1, D_FF), 'ffn_w_out': ffn_w_out.astype(BF16),
    }


def _rope_tables(pos, reps):
    half = A_HEAD_DIM // 2
    inv = ROPE_THETA ** (-jnp.arange(half, dtype=F32) / half)
    ang = pos.astype(F32)[:, None] * inv[None, :]
    cos, sin = jnp.cos(ang), jnp.sin(ang)
    cos_t = jnp.tile(cos, (reps, LANES // half))
    sin_t = jnp.tile(jnp.concatenate([-sin, sin], axis=-1), (reps, LANES // A_HEAD_DIM))
    return cos_t, sin_t


def kernel(x_prompt, x_sample, cache_cmp_kv, cache_slc_kv, page_table, state_win_kv, state_hgrn, state_ffn_conv, attn_norm_g, w_in, q_norm_g, k_norm_g, cmp_pos_emb, cmp_w1, cmp_w2, hgrn_lb_logits, hgrn_norm_g, w_branch, w_out, ffn_norm_g, ffn_w_in, ffn_conv_w, ffn_conv_b, ffn_w_out):
    assert w_in.shape[0] == 1, "single-layer step"
    b, t, _ = x_prompt.shape
    db, ts, _ = x_sample.shape
    n_pool = cache_cmp_kv.shape[1]
    past = page_table.shape[1] * PAGE_SIZE
    wb = state_win_kv.shape[2]
    assert t % SLC_BLOCK == 0 and t % Q_BLOCK == 0
    consts = _prepare_consts(attn_norm_g[0], w_in[0], q_norm_g[0], k_norm_g[0], cmp_pos_emb[0], cmp_w1[0],
                             cmp_w2[0], hgrn_lb_logits, hgrn_norm_g[0], w_branch[0], w_out[0], ffn_norm_g[0],
                             ffn_w_in[0], ffn_conv_w[0], ffn_conv_b[0], ffn_w_out[0])
    kv_shape = (2, A_KV_GROUPS, A_HEAD_DIM)
    tm_p = min(256, t)
    n_s = db * ts

    fp = _inproj(x_prompt.reshape(b * t, D_MODEL), *_rope_tables(jnp.arange(t, dtype=jnp.int32), 1), consts, tm_p,
                 seq_len=t)
    seq = lambda a: a.reshape(b, t, a.shape[-1])
    kc_p, vc_p = _compress_dense(seq(fp['kvc']), consts)
    score_bound = (1.01 * A_HEAD_DIM ** 0.5 * jnp.max(jnp.abs(q_norm_g[0]))
                   * jnp.max(jnp.abs(k_norm_g[0, 1:]))).reshape(1).astype(F32)
    oa_p = _nsa_prompt(score_bound, seq(fp['q']), seq(fp['gate']), kc_p, vc_p, fp['ks_bf'], fp['vs_bf'],
                       fp['kw_bf'], fp['vw_bf'], t)
    ob_p, s_p = _hgrn(seq(fp['qb']), seq(fp['kb']), seq(fp['vb']), seq(fp['lf']), seq(fp['gb']),
                      jnp.zeros((b, B_HEADS, B_KEY_DIM, B_KEY_DIM), F32), consts)
    y_p, conv_p = _merge_ffn(x_prompt.reshape(b * t, D_MODEL), oa_p.reshape(b * t, -1), ob_p.reshape(b * t, -1),
                             fp['mg'], jnp.zeros((b, FFN_CONV - 1, D_FF), F32), t, consts, min(512, t))

    fs = _inproj(x_sample.reshape(n_s, D_MODEL), *_rope_tables(past + jnp.arange(ts, dtype=jnp.int32), db),
                 consts, n_s)
    sseq = lambda a: a.reshape(db, ts, a.shape[-1])
    feat = lambda c: jnp.transpose(c, (0, 2, 3, 4, 1)).reshape(c.shape[0], KV_ROW, c.shape[1])
    new_tile = lambda rows: jnp.pad(jnp.transpose(sseq(rows), (0, 2, 1)), ((0, 0), (0, 0), (0, LANES - ts)))
    kc_s, vc_s = _compress_paged(page_table, feat(cache_cmp_kv[0]), sseq(fs['kvc']), consts)
    oa_s = _nsa_sample(page_table, sseq(fs['q']), sseq(fs['gate']), kc_s, vc_s, feat(cache_slc_kv[0]),
                       new_tile(fs['kvs']), feat(state_win_kv[0]), new_tile(fs['kvw']))
    win_cat = jnp.concatenate([state_win_kv[0].reshape(db, wb, KV_ROW), sseq(fs['kvw'])], axis=1)
    ob_s, s_s = _hgrn(sseq(fs['qb']), sseq(fs['kb']), sseq(fs['vb']), sseq(fs['lf']), sseq(fs['gb']),
                      state_hgrn[0].astype(F32), consts)
    y_s, conv_s = _merge_ffn(x_sample.reshape(n_s, D_MODEL), oa_s.reshape(n_s, -1), ob_s.reshape(n_s, -1),
                             fs['mg'], state_ffn_conv[0], ts, consts, n_s)

    wkeep = min(WINDOW, t)
    unfeat = lambda a: jnp.transpose(a.reshape(b, *kv_shape, a.shape[-1]), (0, 4, 1, 2, 3))[None]
    return (y_p.reshape(b, t, D_MODEL), y_s.reshape(db, ts, D_MODEL),
            unfeat(fp['kvc_t']), fs['kvc'].reshape(1, db, ts, *kv_shape),
            unfeat(fp['kvs_t']), fs['kvs'].reshape(1, db, ts, *kv_shape),
            unfeat(fp['kvw_t'][:, :, t - wkeep:]),
            win_cat[:, ts:].reshape(1, db, wb, *kv_shape),
            s_p[None], s_s[None], conv_p[None], conv_s[None])
```

```python
import functools

import numpy as np
import jax
import jax.numpy as jnp
from jax import lax
from jax.experimental import pallas as pl
from jax.experimental.pallas import tpu as pltpu

F32 = jnp.float32
BF16 = jnp.bfloat16

D_MODEL = 1024
PAGE_SIZE = 128
A_HEADS = 8
A_KV_GROUPS = 2
A_HPG = A_HEADS // A_KV_GROUPS
A_HEAD_DIM = 64
CMP_BLOCK = 32
CMP_STRIDE = 16
SLC_BLOCK = 64
N_SELECT = 16
WINDOW = 512
Q_BLOCK = 128
ROPE_THETA = 10000.0
FORCE_SCORE = 1e4
NEG_INF = -1e30
B_HEADS = 4
B_KEY_DIM = 128
D_FF = 2816
FFN_CONV = 3
EPS = 1e-6

LANES = 128
SUBLANES = 8
KV_ROW = 2 * A_KV_GROUPS * A_HEAD_DIM
S_TILE = 256
W_TILE = 128
SEL_TK = 512
MASK_BIG = 1e30
SCORE_SAFE = 40.0
CHUNK_PITCH = 20
LAND_GROUP = 16
CMP_GROUP = 4
CMP_W1_SHAPE = (CMP_BLOCK // CMP_GROUP, CMP_GROUP * KV_ROW, KV_ROW)
FFN_CHUNK = 1408
HGRN_CHUNK = 128
HGRN_LEVELS = 7
HGRN_STEP_CHUNKS = 4
HGRN_SHORT_ROWS = 4
VMEM_LIMIT = 56 * 1024 * 1024

OFF_Q, OFF_KVC, OFF_KVS, OFF_KVW = 0, 512, 768, 1024
OFF_QB, OFF_FB, OFF_IB, OFF_GB, OFF_MG, OFF_GATE = 1280, 1792, 2304, 2816, 3328, 5376


def _dot(a, b):
    return jnp.dot(a, b, preferred_element_type=F32)


def _dot_nt(a, b):
    return lax.dot_general(a, b, (((1,), (1,)), ((), ())), preferred_element_type=F32)


def _dot_tn(a, b):
    return lax.dot_general(a, b, (((0,), (0,)), ((), ())), preferred_element_type=F32)


def _split_bf16(x):
    hi = x.astype(BF16)
    lo = (x - hi.astype(F32)).astype(BF16)
    return hi, lo


def _round_up(n, m):
    return -(-n // m) * m


def _const_spec(shape):
    nd = len(shape)
    return pl.BlockSpec(shape, lambda *_: (0,) * nd, pipeline_mode=pl.Buffered(1))


def _params(semantics):
    return pltpu.CompilerParams(dimension_semantics=semantics, vmem_limit_bytes=VMEM_LIMIT)


INPROJ_COMMON = ('q', 'gate', 'qb', 'kb', 'vb', 'lf', 'gb', 'mg')
INPROJ_PROMPT = INPROJ_COMMON + ('kvc', 'kvc_t', 'kvs_t', 'kvw_t', 'ks_bf', 'vs_bf', 'kw_bf', 'vw_bf')
INPROJ_SAMPLE = INPROJ_COMMON + ('kvc', 'kvs', 'kvw')


def _inproj_kernel(x_ref, g_ref, wa_ref, wb_ref, wg_ref, cos_ref, sin_ref, qg_ref, kg_ref, lbl_ref, mseg_ref,
                   *out_refs, names):
    o = dict(zip(names, out_refs))
    q_ref, gate_ref, mg_ref = o['q'], o['gate'], o['mg']
    qb_ref, kb_ref, vb_ref, lf_ref, gb_ref = o['qb'], o['kb'], o['vb'], o['lf'], o['gb']
    x = x_ref[...]
    ms = jnp.mean(x * x, axis=-1, keepdims=True)
    h = (x * lax.rsqrt(ms + EPS) * g_ref[...]).astype(BF16)
    cos = cos_ref[...]
    sin = sin_ref[...]
    tm = x.shape[0]
    lane = lax.broadcasted_iota(jnp.int32, (tm, LANES), 1)
    first_half = (lane & (A_HEAD_DIM // 2)) == 0
    low64 = lane < A_HEAD_DIM
    mseg = mseg_ref[...]

    def proj(lo, width):
        for start, ref in ((OFF_GATE, wg_ref), (OFF_QB, wb_ref), (OFF_Q, wa_ref)):
            if lo >= start:
                return _dot(h, ref[:, lo - start:lo - start + width])

    def head_norm_rope(chunk, gain):
        s_hi, s_lo = _split_bf16(chunk * chunk)
        mean = _dot(jnp.concatenate([s_hi, s_lo], axis=-1), mseg)
        y = chunk * lax.rsqrt(mean + EPS) * gain
        rot = jnp.where(first_half, pltpu.roll(y, LANES - A_HEAD_DIM // 2, 1),
                        pltpu.roll(y, A_HEAD_DIM // 2, 1))
        return y * cos + rot * sin

    zq = proj(OFF_Q, A_HEADS * A_HEAD_DIM)
    z_kv = [proj(off, KV_ROW) for off in (OFF_KVC, OFF_KVS, OFF_KVW)]
    qg = qg_ref[...]

    def emit_q(k):
        c = head_norm_rope(zq[:, k * LANES:(k + 1) * LANES], qg) * (A_HEAD_DIM ** -0.5)
        r = pltpu.roll(c, A_HEAD_DIM, 1)
        if k < A_HPG // 2:
            h0, h1 = jnp.where(low64, c, 0.0), jnp.where(low64, r, 0.0)
        else:
            h0, h1 = jnp.where(low64, 0.0, r), jnp.where(low64, 0.0, c)
        q_ref[:, (2 * k) * LANES:(2 * k + 1) * LANES] = h0.astype(BF16)
        q_ref[:, (2 * k + 1) * LANES:(2 * k + 2) * LANES] = h1.astype(BF16)

    def emit_kv(i, name):
        z = z_kv[i]
        rows = jnp.concatenate([head_norm_rope(z[:, :LANES], kg_ref[i:i + 1, :]), z[:, LANES:]], axis=-1)
        if name in o:
            o[name][...] = rows
        if name + '_t' in o:
            rows_t = rows.T
            o[name + '_t'][0] = rows_t
            for half, tag in ((rows_t[:LANES], 'k'), (rows_t[LANES:], 'v')):
                key = tag + name[-1] + '_bf'
                if key in o:
                    n_tiles, _, width = o[key].shape[1:]
                    for j in range(n_tiles):
                        o[key][0, j] = half[:, j * width:(j + 1) * width].astype(BF16)

    lbl = lbl_ref[...]
    e = jnp.exp(lbl - jnp.max(lbl, axis=0, keepdims=True))
    lb = e[0:1, :] / jnp.sum(e, axis=0, keepdims=True)

    emit_q(0)
    fz = proj(OFF_FB, B_HEADS * B_KEY_DIM)
    lf_ref[...] = jnp.log(lb + (1.0 - lb) * jax.nn.sigmoid(fz))
    kb_ref[...] = (1.0 - lb) * jax.nn.sigmoid(-fz)
    emit_q(1)
    zqb = proj(OFF_QB, B_HEADS * B_KEY_DIM)
    qb_ref[...] = zqb * jax.nn.sigmoid(zqb)
    emit_q(2)
    vb_ref[...] = proj(OFF_IB, B_HEADS * B_KEY_DIM)
    emit_q(3)
    zg = proj(OFF_GB, B_HEADS * B_KEY_DIM)
    gb_ref[...] = zg * jax.nn.sigmoid(zg)
    emit_kv(0, 'kvc')
    mg_ref[:, :D_MODEL] = jax.nn.sigmoid(proj(OFF_MG, D_MODEL))
    emit_kv(1, 'kvs')
    mg_ref[:, D_MODEL:] = jax.nn.sigmoid(proj(OFF_MG + D_MODEL, D_MODEL))
    emit_kv(2, 'kvw')
    gate_ref[...] = jax.nn.sigmoid(proj(OFF_GATE, LANES))


def _inproj(x2d, cos_tab, sin_tab, consts, tm, seq_len=None):
    n = x2d.shape[0]
    tab_blocks = cos_tab.shape[0] // tm
    row = lambda w: pl.BlockSpec((tm, w), lambda i: (i, 0))
    tab = pl.BlockSpec((tm, LANES), lambda i: (i % tab_blocks, 0))
    token_major = {'q': (A_HEADS * LANES, BF16), 'gate': (LANES, F32), 'qb': (512, F32), 'kb': (512, F32),
                   'vb': (512, F32), 'lf': (512, F32), 'gb': (512, F32), 'mg': (2 * D_MODEL, F32),
                   'kvc': (KV_ROW, F32), 'kvs': (KV_ROW, F32), 'kvw': (KV_ROW, F32)}
    names = INPROJ_SAMPLE if seq_len is None else INPROJ_PROMPT
    specs, shapes = [], []
    for name in names:
        if name in token_major:
            w, dt = token_major[name]
            specs.append(row(w))
            shapes.append(jax.ShapeDtypeStruct((n, w), dt))
        elif name.endswith('_t'):
            per_seq = seq_len // tm
            specs.append(pl.BlockSpec((1, KV_ROW, tm), lambda i: (i // per_seq, 0, i % per_seq)))
            shapes.append(jax.ShapeDtypeStruct((n // seq_len, KV_ROW, seq_len), F32))
        else:
            width = S_TILE if name[1] == 's' else W_TILE
            per_seq = seq_len // tm
            specs.append(pl.BlockSpec((1, tm // width, LANES, width), lambda i: (i // per_seq, i % per_seq, 0, 0)))
            shapes.append(jax.ShapeDtypeStruct((n // seq_len, seq_len // width, LANES, width), BF16))
    outs = pl.pallas_call(
        functools.partial(_inproj_kernel, names=names),
        grid=(n // tm,),
        in_specs=[row(D_MODEL), _const_spec((1, D_MODEL)), _const_spec((D_MODEL, OFF_QB)),
                  _const_spec((D_MODEL, OFF_GATE - OFF_QB)), _const_spec((D_MODEL, LANES)), tab, tab,
                  _const_spec((1, LANES)), _const_spec((3, LANES)),
                  _const_spec(consts['lbl'].shape), _const_spec((2 * LANES, LANES))],
        out_specs=specs,
        out_shape=shapes,
        compiler_params=_params(("parallel",)),
        name="inproj",
    )(x2d, consts['attn_g'], consts['w_attn'], consts['w_rest'], consts['w_gate'], cos_tab, sin_tab,
      consts['q_g'], consts['k_g'],
      consts['lbl'], consts['mseg'])
    return dict(zip(names, outs))


def _chunk_row(chunk):
    return chunk * CHUNK_PITCH


def _compress_rows(tok_k, tok_v, pe_ref, w1_ref, w2_ref, kc_ref, vc_ref, m_rows, n_cmp):
    acc = jnp.zeros((m_rows, KV_ROW), F32)
    for p0 in range(0, CMP_BLOCK, CMP_GROUP):
        xs = []
        for p in range(p0, p0 + CMP_GROUP):
            start = _chunk_row(p // CMP_STRIDE) + p % CMP_STRIDE
            xp = jnp.concatenate([tok_k[pl.ds(start, m_rows, stride=CHUNK_PITCH), :],
                                  tok_v[pl.ds(start, m_rows, stride=CHUNK_PITCH), :]], axis=-1) + pe_ref[p:p + 1, :]
            xs.append(xp.astype(BF16))
        acc = acc + _dot(jnp.concatenate(xs, axis=-1), w1_ref[p0 // CMP_GROUP])
    hid = acc * jax.nn.sigmoid(acc)
    out = _dot(hid.astype(BF16), w2_ref[...])
    row = lax.broadcasted_iota(jnp.int32, out.shape, 0)
    out = jnp.where(row < n_cmp, out, 0.0)
    ncp = kc_ref.shape[1]
    kc_ref[0, 0:m_rows, :] = out[:, :LANES].astype(BF16)
    vc_ref[0, 0:m_rows, :] = out[:, LANES:].astype(BF16)
    if ncp > m_rows:
        kc_ref[0, m_rows:ncp, :] = jnp.zeros((ncp - m_rows, LANES), BF16)
        vc_ref[0, m_rows:ncp, :] = jnp.zeros((ncp - m_rows, LANES), BF16)


def _zero_chunks(tok_k, tok_v, first_chunk):
    r0 = _chunk_row(first_chunk)
    tok_k[r0:, :] = jnp.zeros((tok_k.shape[0] - r0, LANES), F32)
    tok_v[r0:, :] = jnp.zeros((tok_v.shape[0] - r0, LANES), F32)


def _compress_dense_kernel(rows_ref, pe_ref, w1_ref, w2_ref, kc_ref, vc_ref, tok_k, tok_v, *, m_rows, n_cmp):
    n_chunks = rows_ref.shape[1] // CMP_STRIDE

    def body(c, carry):
        src = pl.multiple_of(c * CMP_STRIDE, CMP_STRIDE)
        dst = pl.multiple_of(_chunk_row(c), int(np.gcd(CHUNK_PITCH, SUBLANES)))
        tok_k[pl.ds(dst, CMP_STRIDE), :] = rows_ref[0, pl.ds(src, CMP_STRIDE), 0:LANES]
        tok_v[pl.ds(dst, CMP_STRIDE), :] = rows_ref[0, pl.ds(src, CMP_STRIDE), LANES:KV_ROW]
        return carry

    lax.fori_loop(0, n_chunks, body, 0)
    _zero_chunks(tok_k, tok_v, n_chunks)
    _compress_rows(tok_k, tok_v, pe_ref, w1_ref, w2_ref, kc_ref, vc_ref, m_rows, n_cmp)


def _page_copy(pool_ref, page, stage, slot, sems):
    return pltpu.make_async_copy(pool_ref.at[page], stage.at[slot], sems.at[slot])


def _compress_paged_kernel(pt_ref, pool_ref, new_ref, pe_ref, w1_ref, w2_ref, kc_ref, vc_ref,
                           stage, tok_k, tok_v, sems, *, n_pages, m_rows, n_cmp):
    b = pl.program_id(0)

    def start_row(row):
        def start(p, carry):
            _page_copy(pool_ref, pt_ref[row, p], stage, p, sems).start()
            return carry
        lax.fori_loop(0, n_pages, start, 0)

    @pl.when(b == 0)
    def _():
        start_row(0)

    chunks_per_page = PAGE_SIZE // CMP_STRIDE
    first_new = n_pages * chunks_per_page
    _zero_chunks(tok_k, tok_v, first_new)
    ts = new_ref.shape[1]
    assert ts <= CMP_STRIDE
    tok_k[_chunk_row(first_new):_chunk_row(first_new) + ts, :] = new_ref[0, :, 0:LANES]
    tok_v[_chunk_row(first_new):_chunk_row(first_new) + ts, :] = new_ref[0, :, LANES:KV_ROW]

    group = int(np.gcd(n_pages, LAND_GROUP))

    def land(i, carry):
        for j in range(group):
            _page_copy(pool_ref, 0, stage, i * group + j, sems).wait()
        for j in range(group):
            p = i * group + j
            page_t = stage[p].T
            base = pl.multiple_of(_chunk_row(p * chunks_per_page), SUBLANES)
            for c in range(chunks_per_page):
                rows = slice(c * CMP_STRIDE, (c + 1) * CMP_STRIDE)
                tok_k[pl.ds(base + _chunk_row(c), CMP_STRIDE), :] = page_t[rows, 0:LANES]
                tok_v[pl.ds(base + _chunk_row(c), CMP_STRIDE), :] = page_t[rows, LANES:KV_ROW]
        return carry

    lax.fori_loop(0, n_pages // group, land, 0)

    @pl.when(b + 1 < pl.num_programs(0))
    def _():
        start_row(b + 1)

    _compress_rows(tok_k, tok_v, pe_ref, w1_ref, w2_ref, kc_ref, vc_ref, m_rows, n_cmp)


def _compress_geometry(t_real):
    t_pad = _round_up(t_real, SLC_BLOCK)
    n_cmp = t_pad // CMP_STRIDE - CMP_BLOCK // CMP_STRIDE + 1
    m_rows = _round_up(n_cmp, SUBLANES)
    ncp = _round_up(n_cmp, LANES)
    n_chunks = max(m_rows + CMP_BLOCK // CMP_STRIDE - 1, -(-t_real // CMP_STRIDE))
    return n_cmp, m_rows, ncp, _chunk_row(n_chunks)


def _compress_dense(rows, consts):
    b, t, _ = rows.shape
    n_cmp, m_rows, ncp, tok_rows = _compress_geometry(t)
    return pl.pallas_call(
        functools.partial(_compress_dense_kernel, m_rows=m_rows, n_cmp=n_cmp),
        grid=(b,),
        in_specs=[pl.BlockSpec((1, t, KV_ROW), lambda i: (i, 0, 0)),
                  _const_spec((CMP_BLOCK, KV_ROW)), _const_spec(CMP_W1_SHAPE),
                  _const_spec((KV_ROW, KV_ROW))],
        out_specs=[pl.BlockSpec((1, ncp, LANES), lambda i: (i, 0, 0))] * 2,
        out_shape=[jax.ShapeDtypeStruct((b, ncp, LANES), BF16)] * 2,
        scratch_shapes=[pltpu.VMEM((tok_rows, LANES), F32)] * 2,
        compiler_params=_params(("parallel",)),
        name="compress_dense",
    )(rows, consts['cmp_pe'], consts['cmp_w1'], consts['cmp_w2'])


def _compress_paged(page_table, pool, new_rows, consts):
    db, n_pages = page_table.shape
    ts = new_rows.shape[1]
    n_cmp, m_rows, ncp, tok_rows = _compress_geometry(n_pages * PAGE_SIZE + ts)
    cm = lambda nd: (lambda i, pt: (0,) * nd)
    grid_spec = pltpu.PrefetchScalarGridSpec(
        num_scalar_prefetch=1,
        grid=(db,),
        in_specs=[pl.BlockSpec(memory_space=pl.ANY),
                  pl.BlockSpec((1, ts, KV_ROW), lambda i, pt: (i, 0, 0)),
                  pl.BlockSpec((CMP_BLOCK, KV_ROW), cm(2), pipeline_mode=pl.Buffered(1)),
                  pl.BlockSpec(CMP_W1_SHAPE, cm(3), pipeline_mode=pl.Buffered(1)),
                  pl.BlockSpec((KV_ROW, KV_ROW), cm(2), pipeline_mode=pl.Buffered(1))],
        out_specs=[pl.BlockSpec((1, ncp, LANES), lambda i, pt: (i, 0, 0))] * 2,
        scratch_shapes=[pltpu.VMEM((n_pages, KV_ROW, PAGE_SIZE), F32), pltpu.VMEM((tok_rows, LANES), F32),
                        pltpu.VMEM((tok_rows, LANES), F32), pltpu.SemaphoreType.DMA((n_pages,))],
    )
    return pl.pallas_call(
        functools.partial(_compress_paged_kernel, n_pages=n_pages, m_rows=m_rows, n_cmp=n_cmp),
        grid_spec=grid_spec,
        out_shape=[jax.ShapeDtypeStruct((db, ncp, LANES), BF16)] * 2,
        compiler_params=_params(("arbitrary",)),
        name="compress_paged",
    )(page_table, pool, new_rows, consts['cmp_pe'], consts['cmp_w1'], consts['cmp_w2'])


def _softmax_rows(s, mask):
    s = jnp.where(mask, s, NEG_INF)
    p = jnp.where(mask, jnp.exp(s - jnp.max(s, axis=-1, keepdims=True)), 0.0)
    return p / jnp.maximum(jnp.sum(p, axis=-1, keepdims=True), 1e-30)


def _online_init(rows):
    return (jnp.full((rows, 1), NEG_INF, F32), jnp.zeros((rows, 1), F32), jnp.zeros((rows, LANES), F32))


def _online_finish(carry):
    _, l, acc = carry
    return acc / jnp.maximum(l, 1e-30)


def _block_scores(p_sum, cover_ref, qpos, n_blocks):
    hi, lo = _split_bf16(p_sum)
    imp = _dot(jnp.concatenate([hi, lo], axis=-1), cover_ref[...])
    blk = lax.broadcasted_iota(jnp.int32, imp.shape, 1)
    cur = qpos // SLC_BLOCK
    forced = (blk == 0) | (blk == cur) | (blk == cur - 1)
    score = jnp.where(forced, FORCE_SCORE, jnp.where(blk <= cur, imp, -1.0))
    return jnp.where(blk < n_blocks, score, -2.0)


def _topk_select(score, n_blocks, n_sel):
    n_tiles = score.shape[1] // LANES
    tiles = [score[:, t * LANES:(t + 1) * LANES] for t in range(n_tiles)]
    lane = lax.broadcasted_iota(jnp.int32, tiles[0].shape, 1)
    cnt = [jnp.zeros(tiles[0].shape, F32) for _ in range(n_tiles)]
    for s in range(n_blocks):
        col = score[:, s:s + 1]
        for t in range(n_tiles):
            if t * LANES > s:
                beats = col >= tiles[t]
            elif (t + 1) * LANES - 1 < s:
                beats = col > tiles[t]
            else:
                beats = (col > tiles[t]) | ((col == tiles[t]) & (lane + t * LANES > s))
            cnt[t] = cnt[t] + jnp.where(beats, 1.0, 0.0)
    blk = lax.broadcasted_iota(jnp.int32, score.shape, 1)
    return jnp.where((jnp.concatenate(cnt, axis=1) < n_sel) & (blk < n_blocks), 1.0, 0.0)


def _assemble_heads(heads, low64):
    chunks = []
    for k in range(A_HEADS // 2):
        a, b = heads[2 * k], heads[2 * k + 1]
        if k < A_HPG // 2:
            chunks.append(jnp.where(low64, a, pltpu.roll(b, A_HEAD_DIM, 1)))
        else:
            chunks.append(jnp.where(low64, pltpu.roll(a, A_HEAD_DIM, 1), b))
    return chunks


def _topk_rows(score_t, n_blocks, n_sel):
    n_tiles = score_t.shape[0] // SUBLANES
    tiles = [score_t[t * SUBLANES:(t + 1) * SUBLANES] for t in range(n_tiles)]
    sub = lax.broadcasted_iota(jnp.int32, tiles[0].shape, 0)
    cnt = [jnp.zeros(tiles[0].shape, F32) for _ in range(n_tiles)]
    for s in range(n_blocks):
        row = score_t[s:s + 1, :]
        for t in range(n_tiles):
            if t * SUBLANES > s:
                beats = row >= tiles[t]
            elif (t + 1) * SUBLANES - 1 < s:
                beats = row > tiles[t]
            else:
                later = sub + t * SUBLANES > s
                beats = (row > tiles[t]) | ((row == tiles[t]) & later)
            cnt[t] = cnt[t] + jnp.where(beats, 1.0, 0.0)
    blk = lax.broadcasted_iota(jnp.int32, score_t.shape, 0)
    return jnp.where((jnp.concatenate(cnt, axis=0) < n_sel) & (blk < n_blocks), 1.0, 0.0)


def _online_step_t(carry, s, v_t):
    m, l, acc = carry
    m_new = jnp.maximum(m, jnp.max(s, axis=-1, keepdims=True))
    alpha = jnp.exp(m - m_new)
    p = jnp.exp(s - m_new)
    l = alpha * l + jnp.sum(p, axis=-1, keepdims=True)
    acc = alpha * acc + _dot_nt(p.astype(BF16), v_t)
    return m_new, l, acc


def _nsa_prompt_kernel(bound_ref, q_ref, gate_ref, kc_ref, vc_ref, ks_ref, vs_ref, kw_ref, vw_ref, cover_ref,
                       exp_ref, o_ref, *, n_blocks, n_sel):
    i = pl.program_id(1)
    s0 = i * Q_BLOCK
    q = q_ref[0]
    gates = gate_ref[0]
    rows = A_HEADS * Q_BLOCK
    q_all = jnp.concatenate([q[:, h * LANES:(h + 1) * LANES] for h in range(A_HEADS)], axis=0)
    qpos1 = s0 + lax.broadcasted_iota(jnp.int32, (Q_BLOCK, 1), 0)
    qpos_all = jnp.concatenate([qpos1] * A_HEADS, axis=0)

    ncp = kc_ref.shape[1]
    c_end = lax.broadcasted_iota(jnp.int32, (1, ncp), 1) * CMP_STRIDE + (CMP_BLOCK - 1)
    p_c = _softmax_rows(_dot_nt(q_all, kc_ref[0]), c_end <= qpos_all)
    o_c = _dot(p_c.astype(BF16), vc_ref[0])

    cur = (s0 + lax.broadcasted_iota(jnp.int32, (1, Q_BLOCK), 1)) // SLC_BLOCK
    blk = lax.broadcasted_iota(jnp.int32, (LANES, Q_BLOCK), 0)
    forced = (blk == 0) | (blk == cur) | (blk == cur - 1)
    scores = []
    for g in range(A_KV_GROUPS):
        base = g * A_HPG * Q_BLOCK
        p_sum = p_c[base:base + Q_BLOCK]
        for hh in range(1, A_HPG):
            p_sum = p_sum + p_c[base + hh * Q_BLOCK:base + (hh + 1) * Q_BLOCK]
        hi, lo = _split_bf16(p_sum)
        imp_t = _dot(jnp.concatenate([hi, lo], axis=-1), cover_ref[...]).T
        score = jnp.where(forced, FORCE_SCORE, jnp.where(blk <= cur, imp_t, -1.0))
        scores.append(jnp.where(blk < n_blocks, score, -2.0))
    nb8 = _round_up(n_blocks, SUBLANES)
    sel_t = _topk_rows(jnp.concatenate(scores, axis=1)[0:nb8], n_blocks, n_sel)
    if nb8 < LANES:
        sel_t = jnp.concatenate([sel_t, jnp.zeros((LANES - nb8, sel_t.shape[1]), F32)], axis=0)

    aug = []
    for g in range(A_KV_GROUPS):
        sel = sel_t[:, g * Q_BLOCK:(g + 1) * Q_BLOCK].T
        aug += [((sel - 1.0) * MASK_BIG).astype(BF16)] * A_HPG
    q_aug = jnp.concatenate([q_all, jnp.concatenate(aug, axis=0)], axis=1)
    per_tile = SEL_TK // S_TILE
    cat = lambda kt, ref_tile: jnp.concatenate([ref_tile(kt * per_tile + j) for j in range(per_tile)], axis=1)

    n_full = s0 // SEL_TK
    kpos = n_full * SEL_TK + lax.broadcasted_iota(jnp.int32, (1, SEL_TK), 1)
    causal = jnp.concatenate([jnp.where(kpos <= qpos1, 0.0, NEG_INF)] * A_HEADS, axis=0)

    st = jnp.maximum(i - WINDOW // Q_BLOCK, 0) * (Q_BLOCK // W_TILE)
    n_wt = (WINDOW + Q_BLOCK) // W_TILE
    d = qpos1 - (st * W_TILE + lax.broadcasted_iota(jnp.int32, (1, n_wt * W_TILE), 1))
    band = jnp.concatenate([jnp.where((d >= 0) & (d < WINDOW), 0.0, NEG_INF)] * A_HEADS, axis=0)

    def attend(shifted):
        half = rows // A_KV_GROUPS

        def weigh(p, v_t):
            ones = jnp.ones((A_HEAD_DIM, v_t.shape[1]), BF16)
            return jnp.concatenate(
                [_dot_nt(p[g * half:(g + 1) * half],
                         jnp.concatenate([v_t[g * A_HEAD_DIM:(g + 1) * A_HEAD_DIM], ones], axis=0))
                 for g in range(A_KV_GROUPS)], axis=0)

        def normalise(acc):
            swapped = pltpu.roll(acc, A_HEAD_DIM, 1)
            lane = lax.broadcasted_iota(jnp.int32, (half, LANES), 1)
            g0 = acc[:half] / jnp.maximum(swapped[:half], 1e-30)
            g1 = swapped[half:] / jnp.maximum(acc[half:], 1e-30)
            return jnp.concatenate([jnp.where(lane < A_HEAD_DIM, g0, 0.0), jnp.where(lane < A_HEAD_DIM, 0.0, g1)],
                                   axis=0)

        def step(state, s, v_t):
            if shifted:
                return _online_step_t(state, s, v_t)
            return state + weigh(jnp.exp(s).astype(BF16), v_t)

        def sel_step(kt, state, bias):
            k_aug = jnp.concatenate([cat(kt, lambda n: ks_ref[0, n]), cat(kt, lambda n: exp_ref[n])], axis=0)
            s = _dot(q_aug, k_aug)
            return step(state, s if bias is None else s + bias, cat(kt, lambda n: vs_ref[0, n]))

        init = _online_init(rows)
        state = lax.fori_loop(0, n_full, lambda kt, carry: sel_step(kt, carry, None), init if shifted else init[2])
        state = sel_step(n_full, state, causal)
        o_sel = _online_finish(state) if shifted else normalise(state)

        kw_t = jnp.concatenate([kw_ref[0, st + j] for j in range(n_wt)], axis=1)
        vw_t = jnp.concatenate([vw_ref[0, st + j] for j in range(n_wt)], axis=1)
        s_w = _dot(q_all, kw_t) + band
        if shifted:
            p_w = jnp.exp(s_w - jnp.max(s_w, axis=-1, keepdims=True))
            o_win = _dot_nt(p_w.astype(BF16), vw_t) / jnp.maximum(jnp.sum(p_w, axis=-1, keepdims=True), 1e-30)
        else:
            o_win = normalise(weigh(jnp.exp(s_w).astype(BF16), vw_t))
        return o_sel, o_win

    o_s, o_w = lax.cond(bound_ref[0] <= SCORE_SAFE, lambda: attend(False), lambda: attend(True))

    heads = []
    for h in range(A_HEADS):
        r = slice(h * Q_BLOCK, (h + 1) * Q_BLOCK)
        heads.append(gates[:, 3 * h:3 * h + 1] * o_c[r] + gates[:, 3 * h + 1:3 * h + 2] * o_s[r]
                     + gates[:, 3 * h + 2:3 * h + 3] * o_w[r])
    low64 = lax.broadcasted_iota(jnp.int32, (Q_BLOCK, LANES), 1) < A_HEAD_DIM
    for k, chunk in enumerate(_assemble_heads(heads, low64)):
        o_ref[0, :, k * LANES:(k + 1) * LANES] = chunk.astype(BF16)


def _cover_matrix(n_cmp, ncp, n_blocks, nsp):
    c = np.arange(ncp)[:, None]
    s = np.arange(nsp)[None, :]
    cover = ((c * CMP_STRIDE < s * SLC_BLOCK + SLC_BLOCK) & (c * CMP_STRIDE + CMP_BLOCK > s * SLC_BLOCK)
             & (c < n_cmp) & (s < n_blocks))
    return jnp.asarray(np.concatenate([cover, cover], axis=0), BF16)


def _expand_matrix(n_keys):
    return (np.arange(n_keys)[None, :] // SLC_BLOCK) == np.arange(LANES)[:, None]


def _expand_tiles(n_keys, tk):
    e = _expand_matrix(n_keys).reshape(LANES, n_keys // tk, tk)
    return jnp.asarray(np.transpose(e, (1, 0, 2)), BF16)


def _nsa_prompt(score_bound, q, gates, kc, vc, ks_bf, vs_bf, kw_bf, vw_bf, t):
    b = q.shape[0]
    ncp = kc.shape[1]
    n_cmp = t // CMP_STRIDE - CMP_BLOCK // CMP_STRIDE + 1
    n_blocks = t // SLC_BLOCK
    assert n_blocks <= LANES and t % SEL_TK == 0 and t >= WINDOW + Q_BLOCK
    cover = _cover_matrix(n_cmp, ncp, n_blocks, LANES)
    expand = _expand_tiles(t, S_TILE)
    per_b = lambda rows, w: pl.BlockSpec((1, rows, w), lambda bi, i: (bi, 0, 0))
    tiles = lambda a: pl.BlockSpec((1,) + a.shape[1:], lambda bi, i: (bi, 0, 0, 0))
    return pl.pallas_call(
        functools.partial(_nsa_prompt_kernel, n_blocks=n_blocks, n_sel=min(N_SELECT, n_blocks)),
        grid=(b, t // Q_BLOCK),
        in_specs=[pl.BlockSpec(memory_space=pltpu.SMEM),
                  pl.BlockSpec((1, Q_BLOCK, A_HEADS * LANES), lambda bi, i: (bi, i, 0)),
                  pl.BlockSpec((1, Q_BLOCK, LANES), lambda bi, i: (bi, i, 0)),
                  per_b(ncp, LANES), per_b(ncp, LANES),
                  tiles(ks_bf), tiles(vs_bf), tiles(kw_bf), tiles(vw_bf),
                  _const_spec(cover.shape), _const_spec(expand.shape)],
        out_specs=pl.BlockSpec((1, Q_BLOCK, A_HEADS * A_HEAD_DIM), lambda bi, i: (bi, i, 0)),
        out_shape=jax.ShapeDtypeStruct((b, t, A_HEADS * A_HEAD_DIM), BF16),
        compiler_params=_params(("parallel", "parallel")),
        name="nsa_prompt",
    )(score_bound, q, gates, kc, vc, ks_bf, vs_bf, kw_bf, vw_bf, cover, expand)


def _key_page_copy(pool_ref, page, bufs, half, slot, sems):
    dst = bufs.at[half, :, pl.ds(pl.multiple_of(slot * PAGE_SIZE, PAGE_SIZE), PAGE_SIZE)]
    return pltpu.make_async_copy(pool_ref.at[page], dst, sems.at[half])


def _nsa_sample_kernel(pt_ref, q_ref, gate_ref, kc_ref, vc_ref, pool_ref, new_ref, win_ref, wnew_ref,
                       cover_ref, exp_ref, o_ref, bufs, sems, *, n_pages, n_blocks, n_sel, key_chunk):
    b = pl.program_id(0)
    half = b % 2

    def start_row(row, into):
        def start(p, carry):
            _key_page_copy(pool_ref, pt_ref[row, p], bufs, into, p, sems).start()
            return carry
        lax.fori_loop(0, n_pages, start, 0)

    @pl.when(b == 0)
    def _():
        start_row(0, 0)

    @pl.when(b + 1 < pl.num_programs(0))
    def _():
        start_row(b + 1, 1 - half)

    buf = bufs.at[half]
    past = n_pages * PAGE_SIZE
    ts = q_ref.shape[1]
    buf[:, past:past + LANES] = new_ref[0]

    q = q_ref[0].astype(F32)
    q_all = jnp.concatenate([q[:, h * LANES:(h + 1) * LANES] for h in range(A_HEADS)], axis=0).astype(BF16)
    rows = A_HEADS * ts
    qpos1 = past + lax.broadcasted_iota(jnp.int32, (ts, 1), 0)
    qpos_all = jnp.concatenate([qpos1] * A_HEADS, axis=0)

    ncp = kc_ref.shape[1]
    c_end = lax.broadcasted_iota(jnp.int32, (1, ncp), 1) * CMP_STRIDE + (CMP_BLOCK - 1)
    p_c = _softmax_rows(_dot_nt(q_all, kc_ref[0]), c_end <= qpos_all)
    o_c = _dot(p_c.astype(BF16), vc_ref[0])

    p_groups = []
    for g in range(A_KV_GROUPS):
        base = g * A_HPG * ts
        p_sum = p_c[base:base + ts]
        for hh in range(1, A_HPG):
            p_sum = p_sum + p_c[base + hh * ts:base + (hh + 1) * ts]
        p_groups.append(p_sum)
    qpos_g = jnp.concatenate([qpos1] * A_KV_GROUPS, axis=0)
    score = _block_scores(jnp.concatenate(p_groups, axis=0), cover_ref, qpos_g, n_blocks)
    sel = _topk_select(score, n_blocks, n_sel).astype(BF16)

    def wait(p, carry):
        _key_page_copy(pool_ref, 0, bufs, half, p, sems).wait()
        return carry

    lax.fori_loop(0, n_pages, wait, 0)

    n_keys = buf.shape[1]
    carry = _online_init(rows)
    for ck in range(-(-n_keys // key_chunk)):
        k0 = ck * key_chunk
        kn = min(key_chunk, n_keys - k0)
        s = _dot(q_all, buf[0:LANES, k0:k0 + kn].astype(BF16))
        blk0 = ck * (key_chunk // SLC_BLOCK)
        em = _dot(sel[:, blk0:blk0 + LANES], exp_ref[:, 0:kn])
        kpos = k0 + lax.broadcasted_iota(jnp.int32, (1, kn), 1)
        bias = jnp.where((em > 0.5) & (kpos <= qpos_g), 0.0, NEG_INF)
        biases = []
        for g in range(A_KV_GROUPS):
            biases += [bias[g * ts:(g + 1) * ts]] * A_HPG
        carry = _online_step_t(carry, s + jnp.concatenate(biases, axis=0),
                               buf[LANES:KV_ROW, k0:k0 + kn].astype(BF16))
    o_s = _online_finish(carry)

    wb = win_ref.shape[2]
    kv_w = jnp.concatenate([win_ref[0], wnew_ref[0]], axis=1).astype(BF16)
    idx = lax.broadcasted_iota(jnp.int32, (1, wb + LANES), 1)
    w_pos = jnp.where(idx < wb, past - wb + idx, past + idx - wb)
    d = qpos_all - w_pos
    m_w = (d >= 0) & (d < WINDOW) & (w_pos >= 0)
    p_w = _softmax_rows(_dot(q_all, kv_w[0:LANES]), m_w)
    o_w = _dot_nt(p_w.astype(BF16), kv_w[LANES:KV_ROW])

    gates = gate_ref[0]
    heads = []
    for h in range(A_HEADS):
        r = slice(h * ts, (h + 1) * ts)
        heads.append(gates[:, 3 * h:3 * h + 1] * o_c[r] + gates[:, 3 * h + 1:3 * h + 2] * o_s[r]
                     + gates[:, 3 * h + 2:3 * h + 3] * o_w[r])
    low64 = lax.broadcasted_iota(jnp.int32, (ts, LANES), 1) < A_HEAD_DIM
    for k, chunk in enumerate(_assemble_heads(heads, low64)):
        o_ref[0, :, k * LANES:(k + 1) * LANES] = chunk.astype(BF16)


def _nsa_sample(page_table, q, gates, kc, vc, pool, new_tile, win_t, wnew_tile):
    db, n_pages = page_table.shape
    wb = win_t.shape[2]
    assert wb % LANES == 0
    ts = q.shape[1]
    past = n_pages * PAGE_SIZE
    ncp = kc.shape[1]
    t_pad = _round_up(past + ts, SLC_BLOCK)
    n_cmp = t_pad // CMP_STRIDE - CMP_BLOCK // CMP_STRIDE + 1
    n_blocks = t_pad // SLC_BLOCK
    key_chunk = LANES * SLC_BLOCK
    n_keys = past + LANES
    nsp = LANES * (-(-n_keys // key_chunk))
    assert nsp >= n_blocks and ts % SUBLANES == 0
    cover = _cover_matrix(n_cmp, ncp, n_blocks, nsp)
    expand = jnp.asarray(_expand_matrix(min(key_chunk, n_keys)), BF16)
    cm = lambda nd: (lambda i, pt: (0,) * nd)
    per_b = lambda r, w: pl.BlockSpec((1, r, w), lambda i, pt: (i, 0, 0))
    grid_spec = pltpu.PrefetchScalarGridSpec(
        num_scalar_prefetch=1,
        grid=(db,),
        in_specs=[per_b(ts, A_HEADS * LANES), per_b(ts, LANES), per_b(ncp, LANES), per_b(ncp, LANES),
                  pl.BlockSpec(memory_space=pl.ANY), per_b(KV_ROW, LANES), per_b(KV_ROW, wb), per_b(KV_ROW, LANES),
                  pl.BlockSpec(cover.shape, cm(2), pipeline_mode=pl.Buffered(1)),
                  pl.BlockSpec(expand.shape, cm(2), pipeline_mode=pl.Buffered(1))],
        out_specs=per_b(ts, A_HEADS * A_HEAD_DIM),
        scratch_shapes=[pltpu.VMEM((2, KV_ROW, n_keys), F32), pltpu.SemaphoreType.DMA((2,))],
    )
    return pl.pallas_call(
        functools.partial(_nsa_sample_kernel, n_pages=n_pages, n_blocks=n_blocks,
                          n_sel=min(N_SELECT, n_blocks), key_chunk=key_chunk),
        grid_spec=grid_spec,
        out_shape=jax.ShapeDtypeStruct((db, ts, A_HEADS * A_HEAD_DIM), BF16),
        compiler_params=_params(("arbitrary",)),
        name="nsa_sample",
    )(page_table, q, gates, kc, vc, pool, new_tile, win_t, wnew_tile, cover, expand)


def _hgrn_matrices():
    c = HGRN_CHUNK
    t = np.arange(c)[:, None]
    u = np.arange(c)[None, :]
    mats = [u <= t]
    masks = [t == u]
    for lvl in range(HGRN_LEVELS):
        m = 1 << lvl
        mid = (t // (2 * m)) * (2 * m) + m - 1
        mats.append((u > mid) & (u <= t))
        mats.append((u > t) & (u <= mid))
        masks.append((t // (2 * m) == u // (2 * m)) & (t % (2 * m) >= m) & (u % (2 * m) < m))
    mats.append(u > t)
    pm = np.concatenate(mats, axis=0)
    return (jnp.asarray(np.concatenate([pm, pm], axis=1), BF16),
            jnp.asarray(np.stack(masks).astype(np.float32)))


def _hgrn_kernel(qb_ref, kb_ref, vb_ref, lf_ref, gb_ref, s0_ref, gn_ref, pm_ref, lm_ref,
                 ob_ref, sout_ref, st_scr, pad_scr):
    j = pl.program_id(1)
    c = HGRN_CHUNK
    n_rows, t_blk = qb_ref.shape[0], qb_ref.shape[1]

    @pl.when(j == 0)
    def _():
        for r in range(n_rows):
            for h in range(B_HEADS):
                st_scr[r * B_HEADS + h] = s0_ref[r, h].T

    def load(ref, r, slot, rows):
        if t_blk >= c:
            return ref[r, rows, :]
        pad_scr[4 * r + slot] = jnp.zeros((c, B_HEADS * B_KEY_DIM), F32)
        pad_scr[4 * r + slot, 0:t_blk, :] = ref[r]
        return pad_scr[4 * r + slot]

    gn = gn_ref[...]
    n_sub = max(t_blk // c, 1)
    prepared = {}
    for r in range(n_rows):
        for sub in range(n_sub):
            rows = slice(sub * c, (sub + 1) * c)
            qb, kb, vb, lf = (load(ref, r, slot, rows) for slot, ref in enumerate((qb_ref, kb_ref, vb_ref, lf_ref)))
            lf_hi, lf_lo = _split_bf16(lf)
            ex = _dot(pm_ref[...], jnp.concatenate([lf_hi, lf_lo], axis=0))
            for h in range(B_HEADS):
                sl = slice(h * B_KEY_DIM, (h + 1) * B_KEY_DIM)
                q, k = qb[:, sl], kb[:, sl]
                v = vb[:, sl].astype(BF16)
                b_cum = ex[0:c, sl]
                a = lm_ref[0] * _dot_nt(q.astype(BF16), k.astype(BF16))
                for lvl in range(HGRN_LEVELS):
                    eq = ex[(2 * lvl + 1) * c:(2 * lvl + 2) * c, sl]
                    ek = ex[(2 * lvl + 2) * c:(2 * lvl + 3) * c, sl]
                    a = a + lm_ref[lvl + 1] * _dot_nt((q * jnp.exp(eq)).astype(BF16),
                                                      (k * jnp.exp(ek)).astype(BF16))
                e_end = ex[(2 * HGRN_LEVELS + 1) * c:(2 * HGRN_LEVELS + 2) * c, sl]
                prepared[r, sub, h] = ((q * jnp.exp(b_cum)).astype(BF16), _dot(a.astype(BF16), v),
                                       jnp.exp(b_cum[c - 1:c, :]), _dot_tn(v, (k * jnp.exp(e_end)).astype(BF16)))
    for r in range(n_rows):
        for h in range(B_HEADS):
            sl = slice(h * B_KEY_DIM, (h + 1) * B_KEY_DIM)
            st = st_scr[r * B_HEADS + h]
            for sub in range(n_sub):
                out_rows = slice(sub * c, (sub + 1) * c) if t_blk >= c else slice(0, t_blk)
                q_dec, o_intra, decay_end, kv_end = prepared[r, sub, h]
                o = _dot_nt(q_dec, st.astype(BF16)) + o_intra
                st = st * decay_end + kv_end
                y = o * lax.rsqrt(jnp.mean(o * o, axis=-1, keepdims=True) + EPS) * gn
                ob_ref[r, out_rows, sl] = y[0:min(t_blk, c)] * gb_ref[r, out_rows, sl]
            st_scr[r * B_HEADS + h] = st

    @pl.when(j == pl.num_programs(1) - 1)
    def _():
        for r in range(n_rows):
            for h in range(B_HEADS):
                sout_ref[r, h] = st_scr[r * B_HEADS + h].T


def _hgrn(qb, kb, vb, lf, gb, s0, consts):
    b, t, w = qb.shape
    t_blk = min(t, HGRN_STEP_CHUNKS * HGRN_CHUNK)
    assert t % t_blk == 0 and (t_blk % HGRN_CHUNK == 0 or t == t_blk < HGRN_CHUNK)
    n_rows = HGRN_SHORT_ROWS if (t < HGRN_CHUNK and b % HGRN_SHORT_ROWS == 0) else 1
    tok = pl.BlockSpec((n_rows, t_blk, w), lambda bi, j: (bi, j, 0))
    st = pl.BlockSpec((n_rows, B_HEADS, B_KEY_DIM, B_KEY_DIM), lambda bi, j: (bi, 0, 0, 0))
    return pl.pallas_call(
        _hgrn_kernel,
        grid=(b // n_rows, t // t_blk),
        in_specs=[tok, tok, tok, tok, tok, st, _const_spec((1, B_KEY_DIM)),
                  _const_spec(consts['hgrn_pm'].shape), _const_spec(consts['hgrn_lm'].shape)],
        out_specs=[tok, st],
        out_shape=[jax.ShapeDtypeStruct((b, t, w), F32),
                   jax.ShapeDtypeStruct((b, B_HEADS, B_KEY_DIM, B_KEY_DIM), F32)],
        scratch_shapes=[pltpu.VMEM((n_rows * B_HEADS, B_KEY_DIM, B_KEY_DIM), F32),
                        pltpu.VMEM((4 * n_rows, HGRN_CHUNK, w), F32)],
        compiler_params=_params(("parallel", "arbitrary")),
        name="hgrn",
    )(qb, kb, vb, lf, gb, s0, consts['hgrn_g'], consts['hgrn_pm'], consts['hgrn_lm'])


def _merge_ffn_kernel(x_ref, oa_ref, ob_ref, mg_ref, p1_ref, p2_ref, wa_ref, wb_ref, wo_ref, fg_ref, win_ref,
                      cw_ref, cb_ref, wout_ref, y_ref, a_ref, carry_scr, *, seq_len):
    tm = x_ref.shape[0]
    mg = mg_ref[...]
    m = (mg[:, :D_MODEL] * _dot(oa_ref[...], wa_ref[...])
         + mg[:, D_MODEL:] * _dot(ob_ref[...].astype(BF16), wb_ref[...]))
    x2 = x_ref[...] + _dot(m.astype(BF16), wo_ref[...])
    h = (x2 * lax.rsqrt(jnp.mean(x2 * x2, axis=-1, keepdims=True) + EPS) * fg_ref[...]).astype(BF16)
    row = lax.broadcasted_iota(jnp.int32, (tm, 1), 0)
    carried = seq_len >= tm
    if carried:
        j = pl.program_id(0) % (seq_len // tm)

        @pl.when(j == 0)
        def _():
            carry_scr[0:2, :] = p1_ref[0]
    else:
        t = row % seq_len

    y = x2
    chunks = range(0, D_FF, FFN_CHUNK)
    pre = [(_dot(h, win_ref[:, lo:lo + FFN_CHUNK]), _dot(h, win_ref[:, D_FF + lo:D_FF + lo + FFN_CHUNK]))
           for lo in chunks]
    for lo, (a, gate) in zip(chunks, pre):
        cols = slice(lo, lo + FFN_CHUNK)
        if carried:
            prev = carry_scr[:, cols]
            a1 = jnp.where(row == 0, prev[1:2], pltpu.roll(a, 1, 0))
            a2 = jnp.where(row == 0, prev[0:1], jnp.where(row == 1, prev[1:2], pltpu.roll(a, 2, 0)))
            carry_scr[0:2, cols] = a[tm - 2:tm]
            a_ref[0, :, cols] = a[tm - 2:tm]
        else:
            a1 = jnp.where(t == 0, p1_ref[:, cols], pltpu.roll(a, 1, 0))
            a2 = jnp.where(t < 2, p2_ref[:, cols], pltpu.roll(a, 2, 0))
            a_ref[:, cols] = a
        a_conv = cb_ref[:, cols] + a2 * cw_ref[0:1, cols] + a1 * cw_ref[1:2, cols] + a * cw_ref[2:3, cols]
        act = a_conv * jax.nn.sigmoid(a_conv) * gate
        y = y + _dot(act.astype(BF16), wout_ref[cols, :])
    y_ref[...] = y


def _merge_ffn(x2d, oa, ob, mg, conv_state, seq_len, consts, tm):
    n = x2d.shape[0]
    b = n // seq_len
    row = lambda w: pl.BlockSpec((tm, w), lambda i: (i, 0))
    if seq_len >= tm:
        assert seq_len % tm == 0
        per_seq = seq_len // tm
        p1, p2 = conv_state, conv_state
        p_spec = pl.BlockSpec((1, FFN_CONV - 1, D_FF), lambda i: (i // per_seq, 0, 0))
        a_shape = jax.ShapeDtypeStruct((b, FFN_CONV - 1, D_FF), F32)
        a_spec = pl.BlockSpec((1, FFN_CONV - 1, D_FF), lambda i: (i // per_seq, 0, 0))
    else:
        assert tm % seq_len == 0 and seq_len >= FFN_CONV - 1
        zeros = jnp.zeros((b, seq_len - 1, D_FF), F32)
        p1 = jnp.concatenate([conv_state[:, 1:2], zeros], axis=1).reshape(n, D_FF)
        p2 = jnp.concatenate([conv_state, zeros[:, 1:]], axis=1).reshape(n, D_FF)
        p_spec = row(D_FF)
        a_shape = jax.ShapeDtypeStruct((n, D_FF), F32)
        a_spec = row(D_FF)
    y, a_out = pl.pallas_call(
        functools.partial(_merge_ffn_kernel, seq_len=seq_len),
        grid=(n // tm,),
        in_specs=[row(D_MODEL), row(A_HEADS * A_HEAD_DIM), row(B_HEADS * B_KEY_DIM), row(2 * D_MODEL),
                  p_spec, p_spec,
                  _const_spec((A_HEADS * A_HEAD_DIM, D_MODEL)), _const_spec((B_HEADS * B_KEY_DIM, D_MODEL)),
                  _const_spec((D_MODEL, D_MODEL)), _const_spec((1, D_MODEL)),
                  _const_spec((D_MODEL, 2 * D_FF)), _const_spec((FFN_CONV, D_FF)), _const_spec((1, D_FF)),
                  _const_spec((D_FF, D_MODEL))],
        out_specs=[row(D_MODEL), a_spec],
        out_shape=[jax.ShapeDtypeStruct((n, D_MODEL), F32), a_shape],
        scratch_shapes=[pltpu.VMEM((SUBLANES, D_FF), F32)],
        compiler_params=_params(("arbitrary",)),
        name="merge_ffn",
    )(x2d, oa, ob, mg, p1, p2, consts['w_a'], consts['w_b'], consts['w_out'], consts['ffn_g'],
      consts['ffn_w_in'], consts['conv_w'], consts['conv_b'], consts['ffn_w_out'])
    if seq_len >= tm:
        return y, a_out
    return y, a_out.reshape(b, seq_len, D_FF)[:, seq_len - (FFN_CONV - 1):]


def _prepare_consts(attn_norm_g, w_in, q_norm_g, k_norm_g, cmp_pos_emb, cmp_w1, cmp_w2, hgrn_lb_logits,
                    hgrn_norm_g, w_branch, w_out, ffn_norm_g, ffn_w_in, ffn_conv_w, ffn_conv_b, ffn_w_out):
    n_q = A_HEADS * A_HEAD_DIM
    gate_lo = n_q + 3 * KV_ROW
    gate_hi = gate_lo + 3 * A_HEADS
    w_gate = jnp.pad(w_in[:, gate_lo:gate_hi], ((0, 0), (0, LANES - 3 * A_HEADS))).astype(BF16)
    seg = np.arange(LANES) // A_HEAD_DIM
    mseg = (seg[:, None] == seg[None, :]).astype(np.float32) / A_HEAD_DIM
    eye = jnp.eye(2 * A_KV_GROUPS, dtype=F32)
    jsel = np.repeat(np.arange(2), A_KV_GROUPS)
    w1 = cmp_w1.reshape(2, CMP_BLOCK, A_HEAD_DIM, A_HEAD_DIM)[jsel]
    w1_bd = jnp.einsum('ab,apde->padbe', eye, w1).reshape(CMP_W1_SHAPE).astype(BF16)
    w2_bd = jnp.einsum('ab,ade->adbe', eye, cmp_w2[jsel]).reshape(KV_ROW, KV_ROW).astype(BF16)
    pe = jnp.transpose(cmp_pos_emb[jsel], (1, 0, 2)).reshape(CMP_BLOCK, KV_ROW)
    pm, lm = _hgrn_matrices()
    return {
        'attn_g': attn_norm_g.reshape(1, D_MODEL), 'w_attn': w_in[:, :gate_lo].astype(BF16),
        'w_rest': w_in[:, gate_hi:].astype(BF16), 'w_gate': w_gate,
        'q_g': jnp.tile(q_norm_g, 2).reshape(1, LANES), 'k_g': jnp.tile(k_norm_g, (1, 2)),
        'lbl': hgrn_lb_logits.astype(F32), 'mseg': jnp.asarray(np.concatenate([mseg, mseg], axis=0), BF16),
        'cmp_pe': pe, 'cmp_w1': w1_bd, 'cmp_w2': w2_bd,
        'hgrn_g': hgrn_norm_g.reshape(1, B_KEY_DIM), 'hgrn_pm': pm, 'hgrn_lm': lm,
        'w_a': w_branch[:n_q].astype(BF16), 'w_b': w_branch[n_q:].astype(BF16), 'w_out': w_out.astype(BF16),
        'ffn_g': ffn_norm_g.reshape(1, D_MODEL), 'ffn_w_in': ffn_w_in.astype(BF16),
        'conv_w': ffn_conv_w, 'conv_b': ffn_conv_b.reshape(1, D_FF), 'ffn_w_out': ffn_w_out.astype(BF16),
    }


def _rope_tables(pos, reps):
    half = A_HEAD_DIM // 2
    inv = ROPE_THETA ** (-jnp.arange(half, dtype=F32) / half)
    ang = pos.astype(F32)[:, None] * inv[None, :]
    cos, sin = jnp.cos(ang), jnp.sin(ang)
    cos_t = jnp.tile(cos, (reps, LANES // half))
    sin_t = jnp.tile(jnp.concatenate([-sin, sin], axis=-1), (reps, LANES // A_HEAD_DIM))
    return cos_t, sin_t


def kernel(x_prompt, x_sample, cache_cmp_kv, cache_slc_kv, page_table, state_win_kv, state_hgrn, state_ffn_conv, attn_norm_g, w_in, q_norm_g, k_norm_g, cmp_pos_emb, cmp_w1, cmp_w2, hgrn_lb_logits, hgrn_norm_g, w_branch, w_out, ffn_norm_g, ffn_w_in, ffn_conv_w, ffn_conv_b, ffn_w_out):
    assert w_in.shape[0] == 1, "single-layer step"
    b, t, _ = x_prompt.shape
    db, ts, _ = x_sample.shape
    n_pool = cache_cmp_kv.shape[1]
    past = page_table.shape[1] * PAGE_SIZE
    wb = state_win_kv.shape[2]
    assert t % SLC_BLOCK == 0 and t % Q_BLOCK == 0
    consts = _prepare_consts(attn_norm_g[0], w_in[0], q_norm_g[0], k_norm_g[0], cmp_pos_emb[0], cmp_w1[0],
                             cmp_w2[0], hgrn_lb_logits, hgrn_norm_g[0], w_branch[0], w_out[0], ffn_norm_g[0],
                             ffn_w_in[0], ffn_conv_w[0], ffn_conv_b[0], ffn_w_out[0])
    kv_shape = (2, A_KV_GROUPS, A_HEAD_DIM)
    tm_p = min(256, t)
    n_s = db * ts

    fp = _inproj(x_prompt.reshape(b * t, D_MODEL), *_rope_tables(jnp.arange(t, dtype=jnp.int32), 1), consts, tm_p,
                 seq_len=t)
    seq = lambda a: a.reshape(b, t, a.shape[-1])
    kc_p, vc_p = _compress_dense(seq(fp['kvc']), consts)
    score_bound = (1.01 * A_HEAD_DIM ** 0.5 * jnp.max(jnp.abs(q_norm_g[0]))
                   * jnp.max(jnp.abs(k_norm_g[0, 1:]))).reshape(1).astype(F32)
    oa_p = _nsa_prompt(score_bound, seq(fp['q']), seq(fp['gate']), kc_p, vc_p, fp['ks_bf'], fp['vs_bf'],
                       fp['kw_bf'], fp['vw_bf'], t)
    ob_p, s_p = _hgrn(seq(fp['qb']), seq(fp['kb']), seq(fp['vb']), seq(fp['lf']), seq(fp['gb']),
                      jnp.zeros((b, B_HEADS, B_KEY_DIM, B_KEY_DIM), F32), consts)
    y_p, conv_p = _merge_ffn(x_prompt.reshape(b * t, D_MODEL), oa_p.reshape(b * t, -1), ob_p.reshape(b * t, -1),
                             fp['mg'], jnp.zeros((b, FFN_CONV - 1, D_FF), F32), t, consts, min(512, t))

    fs = _inproj(x_sample.reshape(n_s, D_MODEL), *_rope_tables(past + jnp.arange(ts, dtype=jnp.int32), db),
                 consts, n_s)
    sseq = lambda a: a.reshape(db, ts, a.shape[-1])
    feat = lambda c: jnp.transpose(c, (0, 2, 3, 4, 1)).reshape(c.shape[0], KV_ROW, c.shape[1])
    new_tile = lambda rows: jnp.pad(jnp.transpose(sseq(rows), (0, 2, 1)), ((0, 0), (0, 0), (0, LANES - ts)))
    kc_s, vc_s = _compress_paged(page_table, feat(cache_cmp_kv[0]), sseq(fs['kvc']), consts)
    oa_s = _nsa_sample(page_table, sseq(fs['q']), sseq(fs['gate']), kc_s, vc_s, feat(cache_slc_kv[0]),
                       new_tile(fs['kvs']), feat(state_win_kv[0]), new_tile(fs['kvw']))
    win_cat = jnp.concatenate([state_win_kv[0].reshape(db, wb, KV_ROW), sseq(fs['kvw'])], axis=1)
    ob_s, s_s = _hgrn(sseq(fs['qb']), sseq(fs['kb']), sseq(fs['vb']), sseq(fs['lf']), sseq(fs['gb']),
                      state_hgrn[0].astype(F32), consts)
    y_s, conv_s = _merge_ffn(x_sample.reshape(n_s, D_MODEL), oa_s.reshape(n_s, -1), ob_s.reshape(n_s, -1),
                             fs['mg'], state_ffn_conv[0], ts, consts, n_s)

    wkeep = min(WINDOW, t)
    unfeat = lambda a: jnp.transpose(a.reshape(b, *kv_shape, a.shape[-1]), (0, 4, 1, 2, 3))[None]
    return (y_p.reshape(b, t, D_MODEL), y_s.reshape(db, ts, D_MODEL),
            unfeat(fp['kvc_t']), fs['kvc'].reshape(1, db, ts, *kv_shape),
            unfeat(fp['kvs_t']), fs['kvs'].reshape(1, db, ts, *kv_shape),
            unfeat(fp['kvw_t'][:, :, t - wkeep:]),
            win_cat[:, ts:].reshape(1, db, wb, *kv_shape),
            s_p[None], s_s[None], conv_p[None], conv_s[None])
```
